```python
import jax, jax.numpy as jnp
from jax import lax
import numpy as np

D_MODEL = 2048
BATCH = 4
SEQ = 2048
DEPTH = 4

D_FF = 5504
RMS_EPS = 1e-6

S5_WIDTH = 1024
S5_GROUP = 16
S5_GROUPS = S5_WIDTH // S5_GROUP
S5_STATE = 64
DT_MIN = 1e-3
DT_MAX = 1e-1

ATTN_HEAD_DIM = 64
ATTN_HEADS_PER_GROUP = 8
DIL_PATTERNS = ((128, 1), (512, 4), (2048, 16))
N_DIL = len(DIL_PATTERNS)
ATTN_WIDTH = N_DIL * ATTN_HEADS_PER_GROUP * ATTN_HEAD_DIM
ATTN_OUT = ATTN_HEADS_PER_GROUP * ATTN_HEAD_DIM
Q_BLOCK = 128
ALIBI_MAX_BIAS = 8.0

RWKV_HEAD_DIM = 64
RWKV_HEADS = 16
RWKV_WIDTH = RWKV_HEADS * RWKV_HEAD_DIM
DECAY_LORA = 96
AAA_LORA = 96
MV_LORA = 64
GATE_LORA = 256
RWKV_GN_EPS = 64e-5
RWKV_SPLITS = (RWKV_WIDTH, RWKV_WIDTH, RWKV_WIDTH, DECAY_LORA, AAA_LORA, GATE_LORA)
RWKV_PART = sum(RWKV_SPLITS)

N_BRANCH = 3
IN_SPLITS = (S5_WIDTH, ATTN_WIDTH, ATTN_WIDTH, ATTN_WIDTH, RWKV_PART, N_BRANCH * D_MODEL)
D_IN = sum(IN_SPLITS)

kernel_name = 'hybrid_s5_dilattn_rwkv7_macaron'


def _split_points(sizes):
    return np.cumsum(sizes)[:-1].tolist()


def rms_norm(x, gain, eps=RMS_EPS):
    xf = x.astype(jnp.float32)
    y = xf * lax.rsqrt(jnp.mean(xf * xf, axis=-1, keepdims=True) + eps)
    return (y * gain.astype(jnp.float32)).astype(x.dtype)


def swiglu_ffn(x, norm_gain, w_up, w_down):
    h = rms_norm(x, norm_gain)
    a, b = jnp.split(h @ w_up, 2, axis=-1)
    return (jax.nn.silu(a) * b) @ w_down


def _complex_linear_combine(e1, e2):
    a1r, a1i, b1r, b1i = e1
    a2r, a2i, b2r, b2i = e2
    ar = a2r * a1r - a2i * a1i
    ai = a2r * a1i + a2i * a1r
    br = a2r * b1r - a2i * b1i + b2r
    bi = a2r * b1i + a2i * b1r + b2i
    return ar, ai, br, bi


def s5_mixer(u, a_re, a_im, b_re, b_im, c_re, c_im, log_dt, d_skip, w_glu):
    f32 = jnp.float32
    bsz, t, _ = u.shape
    uf = u.astype(f32)
    ug = uf.reshape(bsz, t, S5_GROUPS, S5_GROUP)
    lam_re = jnp.minimum(a_re.astype(f32), -1e-4)
    lam_im = a_im.astype(f32)
    dt = jnp.exp(log_dt.astype(f32))[:, None]
    mag = jnp.exp(lam_re * dt)
    ab_re = mag * jnp.cos(lam_im * dt)
    ab_im = mag * jnp.sin(lam_im * dt)
    den = lam_re * lam_re + lam_im * lam_im
    num_re = ab_re - 1.0
    coef_re = (num_re * lam_re + ab_im * lam_im) / den
    coef_im = (ab_im * lam_re - num_re * lam_im) / den
    br = b_re.astype(f32)
    bi = b_im.astype(f32)
    bb_re = coef_re[..., None] * br - coef_im[..., None] * bi
    bb_im = coef_re[..., None] * bi + coef_im[..., None] * br
    bu_re = jnp.einsum('gph,btgh->btgp', bb_re, ug)
    bu_im = jnp.einsum('gph,btgh->btgp', bb_im, ug)
    a_re_t = jnp.broadcast_to(ab_re, bu_re.shape)
    a_im_t = jnp.broadcast_to(ab_im, bu_im.shape)
    _, _, s_re, s_im = lax.associative_scan(
        _complex_linear_combine, (a_re_t, a_im_t, bu_re, bu_im), axis=1)
    y = (jnp.einsum('ghp,btgp->btgh', c_re.astype(f32), s_re)
         - jnp.einsum('ghp,btgp->btgh', c_im.astype(f32), s_im))
    y = y.reshape(bsz, t, S5_WIDTH) + d_skip.astype(f32) * uf
    z = jax.nn.gelu(y).astype(u.dtype)
    return z * jax.nn.sigmoid(z @ w_glu)


def alibi_slopes(n_heads):
    s = 2.0 ** (-ALIBI_MAX_BIAS * (np.arange(n_heads) + 1) / n_heads)
    return jnp.asarray(s, dtype=jnp.float32)


def dilated_group_attention(q, k, v, slopes, window, dilation):
    f32 = jnp.float32
    bsz, t, h, hd = q.shape
    n_keys = window // dilation + 1
    offs = dilation * jnp.arange(n_keys)
    bias = -slopes[:, None] * offs[None, :].astype(f32)
    scale = hd ** -0.5

    def block(t0):
        qb = lax.dynamic_slice_in_dim(q, t0, Q_BLOCK, axis=1)
        pos = t0 + jnp.arange(Q_BLOCK)
        kpos = pos[:, None] - offs[None, :]
        valid = kpos >= 0
        kidx = jnp.maximum(kpos, 0)
        kb = jnp.take(k, kidx, axis=1)
        vb = jnp.take(v, kidx, axis=1)
        s = jnp.einsum('bqhd,bqkhd->bhqk', qb, kb).astype(f32) * scale + bias[None, :, None, :]
        s = jnp.where(valid[None, None], s, -jnp.inf)
        lse = jax.nn.logsumexp(s, axis=-1)
        p = jnp.exp(s - lse[..., None])
        o = jnp.einsum('bhqk,bqkhd->bqhd', p, vb.astype(f32))
        return o, lse

    starts = jnp.arange(t // Q_BLOCK, dtype=jnp.int32) * Q_BLOCK
    o, lse = lax.map(block, starts)
    o = jnp.moveaxis(o, 0, 1).reshape(bsz, t, h, hd)
    lse = jnp.moveaxis(lse, 0, 2).reshape(bsz, h, t)
    return o, lse


def dilated_attention_mixer(q, k, v, q_gain, k_gain):
    bsz, t, _ = q.shape
    shp = (bsz, t, N_DIL, ATTN_HEADS_PER_GROUP, ATTN_HEAD_DIM)
    q = rms_norm(q.reshape(shp), q_gain)
    k = rms_norm(k.reshape(shp), k_gain)
    v = v.reshape(shp)
    slopes = alibi_slopes(N_DIL * ATTN_HEADS_PER_GROUP).reshape(N_DIL, ATTN_HEADS_PER_GROUP)
    outs, lses = [], []
    for gi, (window, dilation) in enumerate(DIL_PATTERNS):
        o, lse = dilated_group_attention(q[:, :, gi], k[:, :, gi], v[:, :, gi],
                                         slopes[gi], window, dilation)
        outs.append(o)
        lses.append(lse)
    w = jax.nn.softmax(jnp.stack(lses, axis=0), axis=0)
    w = jnp.transpose(w, (0, 1, 3, 2))[..., None]
    o = jnp.sum(w * jnp.stack(outs, axis=0), axis=0)
    return o.reshape(bsz, t, ATTN_OUT).astype(q.dtype)


def wkv7_scan(r, w, k, v, a, b):
    bsz, t, h, n = r.shape

    def step(S, inp):
        r_t, w_t, k_t, v_t, a_t, b_t = inp
        sa = jnp.einsum('bhij,bhj->bhi', S, a_t)
        S = (S * w_t[:, :, None, :] + sa[..., None] * b_t[:, :, None, :]
             + v_t[..., None] * k_t[:, :, None, :])
        return S, jnp.einsum('bhij,bhj->bhi', S, r_t)

    xs = tuple(jnp.moveaxis(z, 1, 0) for z in (r, w, k, v, a, b))
    S0 = jnp.zeros((bsz, h, n, n), jnp.float32)
    _, y = lax.scan(step, S0, xs)
    return jnp.moveaxis(y, 0, 1)


def rwkv7_mixer(p, v_first, shift_mu, w0, w2, a0, a2, g2, k_k, k_a, r_k, ln_w, ln_b, vres):
    f32 = jnp.float32
    bsz, t, _ = p.shape
    p_prev = jnp.pad(p, ((0, 0), (1, 0), (0, 0)))[:, :-1]
    p = p + (p_prev - p) * shift_mu
    r, k, v, wd, ad, gd = jnp.split(p, _split_points(RWKV_SPLITS), axis=-1)
    w = -jax.nn.softplus(-(w0 + jnp.tanh(wd) @ w2)) - 0.5
    a = jax.nn.sigmoid(a0 + ad @ a2)
    g = jax.nn.sigmoid(gd) @ g2
    if vres is None:
        v_first = v
    else:
        v0, v1, v2 = vres
        v = v + (v_first - v) * jax.nn.sigmoid(v0 + (v @ v1) @ v2)

    def heads(z):
        return z.reshape(bsz, t, RWKV_HEADS, RWKV_HEAD_DIM).astype(f32)

    kk = heads(k * k_k)
    kk = kk * lax.rsqrt(jnp.maximum(jnp.sum(kk * kk, axis=-1, keepdims=True), 1e-24))
    k = k * (1.0 + (a - 1.0) * k_a)
    decay = jnp.exp(-jnp.exp(heads(w)))
    rh, kh, vh = heads(r), heads(k), heads(v)
    y = wkv7_scan(rh, decay, kh, vh, -kk, kk * heads(a))
    mean = jnp.mean(y, axis=-1, keepdims=True)
    var = jnp.mean(jnp.square(y - mean), axis=-1, keepdims=True)
    y = (y - mean) * lax.rsqrt(var + RWKV_GN_EPS)
    y = y * ln_w.astype(f32).reshape(RWKV_HEADS, RWKV_HEAD_DIM) + ln_b.astype(f32).reshape(RWKV_HEADS, RWKV_HEAD_DIM)
    y = y + jnp.sum(rh * kh * r_k.astype(f32), axis=-1, keepdims=True) * vh
    y = y.reshape(bsz, t, RWKV_WIDTH).astype(p.dtype)
    return y * g, v_first


def setup_inputs(seed: int = 0) -> dict:
    key = jax.random.key(seed)
    keys = jax.random.split(key, 48)
    counter = [0]
    f32 = jnp.float32

    def nk():
        kk = keys[counter[0]]
        counter[0] += 1
        return kk

    def nrm(shape, scale):
        return scale * jax.random.normal(nk(), shape, f32)

    def gain(shape):
        return 1.0 + nrm(shape, 0.02)

    L, D = DEPTH, D_MODEL
    G, P, Hc = S5_GROUPS, S5_STATE, S5_GROUP
    RW = RWKV_WIDTH
    inp = {}
    inp['x'] = nrm((BATCH, SEQ, D), 1.0)
    inp['ffn1_norm'] = gain((L, D))
    inp['ffn1_up'] = nrm((L, D, 2 * D_FF), D ** -0.5)
    inp['ffn1_down'] = nrm((L, D_FF, D), D_FF ** -0.5)
    inp['mix_norm'] = gain((L, D))
    inp['w_in'] = nrm((L, D, D_IN), D ** -0.5)
    inp['s5_a_re'] = -0.5 + nrm((L, G, P), 0.01)
    inp['s5_a_im'] = np.pi * jnp.arange(P, dtype=f32) + nrm((L, G, P), 0.01)
    inp['s5_b_re'] = nrm((L, G, P, Hc), (2 * Hc) ** -0.5)
    inp['s5_b_im'] = nrm((L, G, P, Hc), (2 * Hc) ** -0.5)
    inp['s5_c_re'] = nrm((L, G, Hc, P), P ** -0.5)
    inp['s5_c_im'] = nrm((L, G, Hc, P), P ** -0.5)
    inp['s5_log_dt'] = jax.random.uniform(nk(), (L, G), f32, float(np.log(DT_MIN)), float(np.log(DT_MAX)))
    inp['s5_d'] = nrm((L, S5_WIDTH), 1.0)
    inp['s5_w_glu'] = nrm((L, S5_WIDTH, S5_WIDTH), S5_WIDTH ** -0.5)
    inp['attn_q_gain'] = gain((L, ATTN_HEAD_DIM))
    inp['attn_k_gain'] = gain((L, ATTN_HEAD_DIM))
    inp['rwkv_shift_mu'] = jax.random.uniform(nk(), (L, RWKV_PART), f32)
    inp['rwkv_w0'] = jnp.tile(jnp.linspace(-6.5, -1.5, RWKV_HEAD_DIM, dtype=f32), RWKV_HEADS) + nrm((L, RW), 0.1)
    inp['rwkv_w2'] = nrm((L, DECAY_LORA, RW), 0.5 * DECAY_LORA ** -0.5)
    inp['rwkv_a0'] = nrm((L, RW), 0.1)
    inp['rwkv_a2'] = nrm((L, AAA_LORA, RW), AAA_LORA ** -0.5)
    inp['rwkv_g2'] = nrm((L, GATE_LORA, RW), GATE_LORA ** -0.5)
    inp['rwkv_k_k'] = 0.85 + nrm((L, RW), 0.02)
    inp['rwkv_k_a'] = gain((L, RW))
    inp['rwkv_r_k'] = nrm((L, RWKV_HEADS, RWKV_HEAD_DIM), 0.1)
    inp['rwkv_ln_w'] = gain((L, RW))
    inp['rwkv_ln_b'] = nrm((L, RW), 0.01)
    inp['rwkv_v0'] = 1.0 + nrm((L - 1, RW), 0.1)
    inp['rwkv_v1'] = nrm((L - 1, RW, MV_LORA), RW ** -0.5)
    inp['rwkv_v2'] = nrm((L - 1, MV_LORA, RW), MV_LORA ** -0.5)
    inp['w_branch_s5'] = nrm((L, S5_WIDTH, D), S5_WIDTH ** -0.5)
    inp['w_branch_attn'] = nrm((L, ATTN_OUT, D), ATTN_OUT ** -0.5)
    inp['w_branch_rwkv'] = nrm((L, RW, D), RW ** -0.5)
    inp['w_out'] = nrm((L, D, D), D ** -0.5)
    inp['ffn2_norm'] = gain((L, D))
    inp['ffn2_up'] = nrm((L, D, 2 * D_FF), D ** -0.5)
    inp['ffn2_down'] = nrm((L, D_FF, D), D_FF ** -0.5)
    return inp


def reference(x, ffn1_norm, ffn1_up, ffn1_down, mix_norm, w_in,
              s5_a_re, s5_a_im, s5_b_re, s5_b_im, s5_c_re, s5_c_im, s5_log_dt, s5_d, s5_w_glu,
              attn_q_gain, attn_k_gain,
              rwkv_shift_mu, rwkv_w0, rwkv_w2, rwkv_a0, rwkv_a2, rwkv_g2, rwkv_k_k, rwkv_k_a,
              rwkv_r_k, rwkv_ln_w, rwkv_ln_b, rwkv_v0, rwkv_v1, rwkv_v2,
              w_branch_s5, w_branch_attn, w_branch_rwkv, w_out,
              ffn2_norm, ffn2_up, ffn2_down):
    bsz, t, _ = x.shape
    v_first = None
    for l in range(DEPTH):
        x = x + 0.5 * swiglu_ffn(x, ffn1_norm[l], ffn1_up[l], ffn1_down[l])
        h = rms_norm(x, mix_norm[l])
        proj = h @ w_in[l]
        u_s5, q, k, v, p_rwkv, gate_logits = jnp.split(proj, _split_points(IN_SPLITS), axis=-1)
        y_s5 = s5_mixer(u_s5, s5_a_re[l], s5_a_im[l], s5_b_re[l], s5_b_im[l],
                        s5_c_re[l], s5_c_im[l], s5_log_dt[l], s5_d[l], s5_w_glu[l])
        y_attn = dilated_attention_mixer(q, k, v, attn_q_gain[l], attn_k_gain[l])
        vres = None if l == 0 else (rwkv_v0[l - 1], rwkv_v1[l - 1], rwkv_v2[l - 1])
        y_rwkv, v_first = rwkv7_mixer(p_rwkv, v_first, rwkv_shift_mu[l], rwkv_w0[l], rwkv_w2[l],
                                      rwkv_a0[l], rwkv_a2[l], rwkv_g2[l], rwkv_k_k[l], rwkv_k_a[l],
                                      rwkv_r_k[l], rwkv_ln_w[l], rwkv_ln_b[l], vres)
        gates = jax.nn.sigmoid(gate_logits).reshape(bsz, t, N_BRANCH, D_MODEL)
        merged = (gates[:, :, 0] * (y_s5 @ w_branch_s5[l])
                  + gates[:, :, 1] * (y_attn @ w_branch_attn[l])
                  + gates[:, :, 2] * (y_rwkv @ w_branch_rwkv[l]))
        x = x + merged @ w_out[l]
        x = x + 0.5 * swiglu_ffn(x, ffn2_norm[l], ffn2_up[l], ffn2_down[l])
    return x
```

```python
import functools

import numpy as np
import jax
import jax.numpy as jnp
from jax import lax
from jax.experimental import pallas as pl
from jax.experimental.pallas import tpu as pltpu

F32 = jnp.float32
BF16 = jnp.bfloat16
HIGHEST = lax.Precision.HIGHEST

LANES = 128
SUBLANES = 8
VMEM_LIMIT_BYTES = 56 * 1024 * 1024

RMS_EPS = 1e-6
S5_GROUP = 16
S5_STATE = 64
ATTN_HEAD_DIM = 64
ATTN_HEADS = 8
DIL_PATTERNS = ((128, 1), (512, 4), (2048, 16))
ALIBI_MAX_BIAS = 8.0
ATTN_BLOCK = 128
RWKV_HEAD_DIM = 64
RWKV_GN_EPS = 64e-5
WKV_CHUNK = 64
NEG_BIG = -1e30


def _cparams(sem):
    return pltpu.CompilerParams(dimension_semantics=sem, vmem_limit_bytes=VMEM_LIMIT_BYTES)


def _dot(a, b, precision=None):
    return jnp.dot(a, b, preferred_element_type=F32, precision=precision)


def _dot_nt(a, b, precision=None):
    return lax.dot_general(a, b, (((1,), (1,)), ((), ())), preferred_element_type=F32, precision=precision)


def _dot_tn(a, b, precision=None):
    return lax.dot_general(a, b, (((0,), (0,)), ((), ())), preferred_element_type=F32, precision=precision)


def _sigmoid(x):
    return 1.0 / (1.0 + jnp.exp(-x))


def _layer_row(arr, layer):
    n = arr.shape[-1]
    a3 = arr.reshape(arr.shape[0], 1, n)
    return a3, pl.BlockSpec((None, 1, n), lambda *_: (layer, 0, 0))


def _mm_kernel(*refs, n_lhs, lhs_has_gain, lhs_needs_cast, rhs_lhs, n_extra, epilogue, tn):
    it = iter(refs)
    lhs_refs, gain_refs = [], []
    for p in range(n_lhs):
        lhs_refs.append(next(it))
        gain_refs.append(next(it) if lhs_has_gain[p] else None)
    rhs_refs = [next(it) for _ in rhs_lhs]
    extra_refs = [next(it) for _ in range(n_extra)]
    out_ref = next(it)
    scratch = {p: next(it) for p in range(n_lhs) if lhs_needs_cast[p]}

    @pl.when(pl.program_id(1) == 0)
    def _():
        for p in range(n_lhs):
            if not lhs_needs_cast[p]:
                continue
            a = lhs_refs[p][...].astype(F32)
            if lhs_has_gain[p]:
                a = a * lax.rsqrt(jnp.mean(a * a, axis=-1, keepdims=True) + RMS_EPS)
                a = a * gain_refs[p][...].astype(F32)
            scratch[p][...] = a.astype(BF16)

    accs = [None] * len(rhs_lhs)
    for p in range(n_lhs):
        mine = [r for r, q in enumerate(rhs_lhs) if q == p]
        a = scratch[p][...] if lhs_needs_cast[p] else lhs_refs[p][...]
        ws = [rhs_refs[r][...].astype(BF16) for r in mine]
        w = ws[0] if len(ws) == 1 else jnp.concatenate(ws, axis=1)
        acc = _dot(a, w)
        for n, r in enumerate(mine):
            accs[r] = acc[:, n * tn:(n + 1) * tn]
    out_ref[...] = epilogue(accs, [e[...] for e in extra_refs]).astype(out_ref.dtype)


def _mm(name, lhs, rhs, extras, epilogue, n_out, out_dtype, tm, tn):
    m = lhs[0][0].shape[0]
    operands, in_specs, scratch_shapes = [], [], []
    lhs_has_gain, lhs_needs_cast = [], []
    for a, gain in lhs:
        k = a.shape[1]
        operands.append(a)
        in_specs.append(pl.BlockSpec((tm, k), lambda i, j: (i, 0)))
        lhs_has_gain.append(gain is not None)
        if gain is not None:
            operands.append(gain[0])
            in_specs.append(gain[1])
        needs = gain is not None or a.dtype != BF16
        lhs_needs_cast.append(needs)
        if needs:
            scratch_shapes.append(pltpu.VMEM((tm, k), BF16))
    for w, layer, off, _ in rhs:
        operands.append(w)
        if layer is None:
            in_specs.append(pl.BlockSpec((w.shape[0], tn), lambda i, j, off=off: (0, off + j)))
        else:
            in_specs.append(pl.BlockSpec((None, w.shape[1], tn), lambda i, j, off=off, layer=layer: (layer, 0, off + j)))
    for e, off in extras:
        operands.append(e)
        in_specs.append(pl.BlockSpec((tm, tn), lambda i, j, off=off: (i, off + j)))
    kern = functools.partial(
        _mm_kernel, n_lhs=len(lhs), lhs_has_gain=tuple(lhs_has_gain), lhs_needs_cast=tuple(lhs_needs_cast),
        rhs_lhs=tuple(r[3] for r in rhs), n_extra=len(extras), epilogue=epilogue, tn=tn)
    return pl.pallas_call(
        kern,
        grid=(m // tm, n_out // tn),
        in_specs=in_specs,
        out_specs=pl.BlockSpec((tm, tn), lambda i, j: (i, j)),
        out_shape=jax.ShapeDtypeStruct((m, n_out), out_dtype),
        scratch_shapes=scratch_shapes,
        compiler_params=_cparams(("parallel", "arbitrary")),
        name=name,
    )(*operands)


def _ep_swiglu(accs, extras):
    a, b = accs
    return a * _sigmoid(a) * b


def _ep_half_residual(accs, extras):
    return extras[0] + 0.5 * accs[0]


def _ep_residual(accs, extras):
    return extras[0] + accs[0]


def _ep_plain(accs, extras):
    return accs[0]


def _ep_glu(accs, extras):
    z = extras[0]
    return z * _sigmoid(accs[0])


def _ep_gated_sum(accs, extras):
    out = _sigmoid(extras[0]) * accs[0]
    for g, a in zip(extras[1:], accs[1:]):
        out = out + _sigmoid(g) * a
    return out


def _swiglu_ffn(x, norm, w_up, w_down, layer, tm_up, tm_down):
    d_ff = w_down.shape[1]
    tn = LANES
    act = _mm("ffn_up", [(x, _layer_row(norm, layer))],
              [(w_up, layer, 0, 0), (w_up, layer, d_ff // tn, 0)], [], _ep_swiglu,
              d_ff, BF16, tm_up, tn)
    return _mm("ffn_down", [(act, None)], [(w_down, layer, 0, 0)], [(x, 0)], _ep_half_residual,
               x.shape[1], F32, tm_down, 256)


def _cmul(ar, ai, br, bi):
    return ar * br - ai * bi, ar * bi + ai * br


def _s5_prep_kernel(are_ref, aim_ref, ldt_ref, bre_ref, bim_ref, bbre_ref, bbim_ref, pw_ref):
    lam_re = jnp.minimum(are_ref[...], -1e-4)
    lam_im = aim_ref[...]
    dt = jnp.exp(ldt_ref[...])
    mag = jnp.exp(lam_re * dt)
    ab_re = mag * jnp.cos(lam_im * dt)
    ab_im = mag * jnp.sin(lam_im * dt)
    den = lam_re * lam_re + lam_im * lam_im
    num_re = ab_re - 1.0
    coef_re = (num_re * lam_re + ab_im * lam_im) / den
    coef_im = (ab_im * lam_re - num_re * lam_im) / den
    for h in range(bre_ref.shape[0]):
        br = bre_ref[h]
        bi = bim_ref[h]
        bbre_ref[h] = coef_re * br - coef_im * bi
        bbim_ref[h] = coef_re * bi + coef_im * br
    a2 = _cmul(ab_re, ab_im, ab_re, ab_im)
    a4 = _cmul(a2[0], a2[1], a2[0], a2[1])
    zero = jnp.zeros_like(ab_re)
    for lvl, (pr, pi) in enumerate(((ab_re, ab_im), a2, a4)):
        for r in range(SUBLANES):
            live = r >= (1 << lvl)
            pw_ref[0, lvl, r] = pr if live else zero
            pw_ref[1, lvl, r] = pi if live else zero
    pr, pi = ab_re, ab_im
    for r in range(SUBLANES):
        pw_ref[0, 3, r] = pr
        pw_ref[1, 3, r] = pi
        pr, pi = _cmul(pr, pi, ab_re, ab_im)


def _s5_scan_kernel(u_ref, bre_ref, bim_ref, cre_ref, cim_ref, pw_ref, d_ref, z_ref, sre, sim, *, seq):
    u = u_ref[...]
    sre[...] = _dot(u, bre_ref[...], HIGHEST)
    sim[...] = _dot(u, bim_ref[...], HIGHEST)
    n_state = sre.shape[1]

    def body(i, carry):
        c_re, c_im = carry
        off = pl.multiple_of(i * SUBLANES, SUBLANES)
        xr = sre[pl.ds(off, SUBLANES), :]
        xi = sim[pl.ds(off, SUBLANES), :]
        for lvl in range(3):
            ar = pw_ref[0, lvl]
            ai = pw_ref[1, lvl]
            yr = pltpu.roll(xr, 1 << lvl, 0)
            yi = pltpu.roll(xi, 1 << lvl, 0)
            xr, xi = xr + ar * yr - ai * yi, xi + ar * yi + ai * yr
        pr = pw_ref[0, 3]
        pi = pw_ref[1, 3]
        xr, xi = xr + pr * c_re - pi * c_im, xi + pr * c_im + pi * c_re
        sre[pl.ds(off, SUBLANES), :] = xr
        sim[pl.ds(off, SUBLANES), :] = xi
        return (jnp.broadcast_to(xr[SUBLANES - 1:SUBLANES, :], (SUBLANES, n_state)),
                jnp.broadcast_to(xi[SUBLANES - 1:SUBLANES, :], (SUBLANES, n_state)))

    zero = jnp.zeros((SUBLANES, n_state), F32)
    lax.fori_loop(0, seq // SUBLANES, body, (zero, zero), unroll=2)
    y = _dot(sre[...], cre_ref[...], HIGHEST) - _dot(sim[...], cim_ref[...], HIGHEST)
    y = y + d_ref[...] * u
    z_ref[...] = jax.nn.gelu(y)


def _s5_mixer_pre_glu(proj3, layer, a_re, a_im, b_re, b_im, c_re, c_im, log_dt, d_skip):
    bsz, seq, _ = proj3.shape
    n_grp, n_st = a_re.shape[1], a_re.shape[2]
    hc = b_re.shape[3]
    gpb = LANES // hc
    n_blk = n_grp // gpb
    nsl = gpb * n_st
    ldt = jnp.broadcast_to(log_dt[layer][:, None], (n_grp, n_st))
    b_re_t = jnp.transpose(b_re[layer], (2, 0, 1))
    b_im_t = jnp.transpose(b_im[layer], (2, 0, 1))
    bb_re, bb_im, pw = pl.pallas_call(
        _s5_prep_kernel,
        out_shape=(jax.ShapeDtypeStruct((hc, n_grp, n_st), F32), jax.ShapeDtypeStruct((hc, n_grp, n_st), F32),
                   jax.ShapeDtypeStruct((2, 4, SUBLANES, n_grp, n_st), F32)),
        name="s5_prep",
    )(a_re[layer], a_im[layer], ldt, b_re_t, b_im_t)
    eye = jnp.eye(gpb, dtype=F32)

    def blockdiag_in(bb):
        x = jnp.transpose(bb, (1, 0, 2)).reshape(n_blk, gpb, hc, n_st)
        x = x[:, :, :, None, :] * eye[None, :, None, :, None]
        return x.reshape(n_blk, gpb * hc, nsl)

    def blockdiag_out(c):
        x = jnp.transpose(c.reshape(n_blk, gpb, hc, n_st), (0, 1, 3, 2))
        x = x[:, :, :, None, :] * eye[None, :, None, :, None]
        return x.reshape(n_blk, nsl, gpb * hc)

    bmat_re, bmat_im = blockdiag_in(bb_re), blockdiag_in(bb_im)
    cmat_re, cmat_im = blockdiag_out(c_re[layer]), blockdiag_out(c_im[layer])
    pw_b = pw.reshape(2, 4, SUBLANES, n_blk, nsl).transpose(3, 0, 1, 2, 4)
    d3, d_spec = d_skip.reshape(d_skip.shape[0], 1, -1), pl.BlockSpec((None, 1, LANES), lambda b, g: (layer, 0, g))
    return pl.pallas_call(
        functools.partial(_s5_scan_kernel, seq=seq),
        grid=(bsz, n_blk),
        in_specs=[
            pl.BlockSpec((None, seq, LANES), lambda b, g: (b, 0, g)),
            pl.BlockSpec((None, LANES, nsl), lambda b, g: (g, 0, 0)),
            pl.BlockSpec((None, LANES, nsl), lambda b, g: (g, 0, 0)),
            pl.BlockSpec((None, nsl, LANES), lambda b, g: (g, 0, 0)),
            pl.BlockSpec((None, nsl, LANES), lambda b, g: (g, 0, 0)),
            pl.BlockSpec((None, 2, 4, SUBLANES, nsl), lambda b, g: (g, 0, 0, 0, 0)),
            d_spec,
        ],
        out_specs=pl.BlockSpec((None, seq, LANES), lambda b, g: (b, 0, g)),
        out_shape=jax.ShapeDtypeStruct((bsz, seq, n_grp * hc), F32),
        scratch_shapes=[pltpu.VMEM((seq, nsl), F32), pltpu.VMEM((seq, nsl), F32)],
        compiler_params=_cparams(("parallel", "parallel")),
        name="s5_scan",
    )(proj3, bmat_re, bmat_im, cmat_re, cmat_im, pw_b, d3)


def _attn_kernel(slopes_ref, qg_ref, kg_ref, *refs, seq):
    qkv = refs[:9]
    out_ref = refs[9]
    qm, km, vm, base, o_seq, l_seq, o_all, l_all = refs[10:]
    pair = pl.program_id(1)
    hd = ATTN_HEAD_DIM
    blk = ATTN_BLOCK
    lane = lax.broadcasted_iota(jnp.int32, (1, LANES), 1)
    head_mask = [(lane < hd).astype(F32), (lane >= hd).astype(F32)]
    qg = qg_ref[...]
    kg = kg_ref[...]

    def qk_norm(x, gain):
        x2 = x * x
        ms = [jnp.sum(x2 * head_mask[h], axis=-1, keepdims=True) * (1.0 / hd) for h in range(2)]
        inv = head_mask[0] * lax.rsqrt(ms[0] + RMS_EPS) + head_mask[1] * lax.rsqrt(ms[1] + RMS_EPS)
        return x * inv * gain

    for g, (window, dil) in enumerate(DIL_PATTERNS):
        q_ref, k_ref, v_ref = qkv[3 * g:3 * g + 3]
        n = seq // dil
        n_blocks = n // blk
        n_back = window // dil
        assert n_back == blk and n % blk == 0
        rows = lax.broadcasted_iota(jnp.int32, (blk, 2 * blk), 0)
        cols = lax.broadcasted_iota(jnp.int32, (blk, 2 * blk), 1)
        delta = rows + blk - cols
        valid = (delta >= 0) & (delta <= n_back)
        for h in range(2):
            slope = slopes_ref[g * ATTN_HEADS + 2 * pair + h]
            base[h] = jnp.where(valid, (-slope * dil) * delta.astype(F32), NEG_BIG)

        def one_block(t0, first):
            o_acc = jnp.zeros((blk, LANES), F32)
            l_acc = jnp.zeros((blk, LANES), F32)
            for h in range(2):
                qb = qm[h, pl.ds(t0, blk), :]
                if first:
                    kb = km[pl.ds(0, blk), :]
                    vb = vm[h, pl.ds(0, blk), :]
                    bias = base[h, :, blk:]
                else:
                    kb = km[pl.ds(t0 - blk, 2 * blk), :]
                    vb = vm[h, pl.ds(t0 - blk, 2 * blk), :]
                    bias = base[h]
                s = _dot_nt(qb, kb) * (hd ** -0.5) + bias
                m = jnp.max(s, axis=-1, keepdims=True)
                p = jnp.exp(s - m)
                l = jnp.sum(p, axis=-1, keepdims=True)
                pv = _dot(p.astype(BF16), vb)
                o_acc = o_acc + pv * (1.0 / l)
                l_acc = l_acc + head_mask[h] * (m + jnp.log(l))
            o_seq[pl.ds(t0, blk), :] = o_acc
            l_seq[pl.ds(t0, blk), :] = l_acc

        def one_residue(r, _):
            if dil == 1:
                q, k, v = q_ref[...], k_ref[...], v_ref[...]
            else:
                q = q_ref[pl.ds(r, n, stride=dil), :]
                k = k_ref[pl.ds(r, n, stride=dil), :]
                v = v_ref[pl.ds(r, n, stride=dil), :]
            qn = qk_norm(q, qg)
            km[pl.ds(0, n), :] = qk_norm(k, kg).astype(BF16)
            for h in range(2):
                qm[h, pl.ds(0, n), :] = (qn * head_mask[h]).astype(BF16)
                vm[h, pl.ds(0, n), :] = (v * head_mask[h]).astype(BF16)
            one_block(0, True)
            if n_blocks > 1:
                def blk_body(b, _):
                    one_block(pl.multiple_of(b * blk, blk), False)
                    return 0
                lax.fori_loop(1, n_blocks, blk_body, 0)
            if dil == 1:
                o_all[g] = o_seq[...]
                l_all[g] = l_seq[...]
            else:
                o_all[g, pl.ds(r, n, stride=dil), :] = o_seq[pl.ds(0, n), :]
                l_all[g, pl.ds(r, n, stride=dil), :] = l_seq[pl.ds(0, n), :]
            return 0

        if dil == 1:
            one_residue(0, 0)
        else:
            lax.fori_loop(0, dil, one_residue, 0)

    l0, l1, l2 = l_all[0], l_all[1], l_all[2]
    mx = jnp.maximum(jnp.maximum(l0, l1), l2)
    w0, w1, w2 = jnp.exp(l0 - mx), jnp.exp(l1 - mx), jnp.exp(l2 - mx)
    o = (w0 * o_all[0] + w1 * o_all[1] + w2 * o_all[2]) / (w0 + w1 + w2)
    out_ref[...] = o.astype(out_ref.dtype)


def _attention(proj3, q_off, layer, q_gain, k_gain):
    bsz, seq, _ = proj3.shape
    n_dil = len(DIL_PATTERNS)
    width = n_dil * ATTN_HEADS * ATTN_HEAD_DIM
    n_pairs = ATTN_HEADS * ATTN_HEAD_DIM // LANES
    n_heads = n_dil * ATTN_HEADS
    slopes = jnp.asarray(2.0 ** (-ALIBI_MAX_BIAS * (np.arange(n_heads) + 1) / n_heads), dtype=F32)
    qg = jnp.tile(q_gain[layer], 2)[None]
    kg = jnp.tile(k_gain[layer], 2)[None]
    in_specs = [pl.BlockSpec(memory_space=pltpu.SMEM),
                pl.BlockSpec((1, LANES), lambda b, p: (0, 0)), pl.BlockSpec((1, LANES), lambda b, p: (0, 0))]
    operands = [slopes, qg, kg]
    for g in range(n_dil):
        for which in range(3):
            cb = (q_off + which * width + g * ATTN_HEADS * ATTN_HEAD_DIM) // LANES
            in_specs.append(pl.BlockSpec((None, seq, LANES), lambda b, p, cb=cb: (b, 0, cb + p)))
            operands.append(proj3)
    return pl.pallas_call(
        functools.partial(_attn_kernel, seq=seq),
        grid=(bsz, n_pairs),
        in_specs=in_specs,
        out_specs=pl.BlockSpec((None, seq, LANES), lambda b, p: (b, 0, p)),
        out_shape=jax.ShapeDtypeStruct((bsz, seq, ATTN_HEADS * ATTN_HEAD_DIM), BF16),
        scratch_shapes=[
            pltpu.VMEM((2, seq, LANES), BF16), pltpu.VMEM((seq, LANES), BF16), pltpu.VMEM((2, seq, LANES), BF16),
            pltpu.VMEM((2, ATTN_BLOCK, 2 * ATTN_BLOCK), F32),
            pltpu.VMEM((seq, LANES), F32), pltpu.VMEM((seq, LANES), F32),
            pltpu.VMEM((n_dil, seq, LANES), F32), pltpu.VMEM((n_dil, seq, LANES), F32),
        ],
        compiler_params=_cparams(("parallel", "parallel")),
        name="dilated_attn",
    )(*operands)


def _head_sums(x, ones_bd):
    parts = [_dot(x[:, c:c + LANES], ones_bd, HIGHEST) for c in range(0, x.shape[1], LANES)]
    return jnp.concatenate(parts, axis=1)


def _rwkv_prep_kernel(*refs, has_vres, tiles_per_seq, width, half):
    it = iter(refs)
    cur = [next(it) for _ in range(7)]
    prev = [next(it) for _ in range(7)]
    mu_ref, ones_ref, w0_ref, w2_ref, a0_ref, a2_ref, g2_ref, kk_ref, ka_ref = [next(it) for _ in range(9)]
    if has_vres:
        vfirst_ref, v0_ref, v1_ref, v2_ref = [next(it) for _ in range(4)]
    r_out, lw_out, k_out, v_out, a_out, b_out, g_out = [next(it) for _ in range(7)]

    tm = cur[0].shape[0]
    first_of_seq = (pl.program_id(0) % tiles_per_seq) == 0
    row = lax.broadcasted_iota(jnp.int32, (tm, half), 0)

    def shifted(c):
        x = cur[c][...]
        last_prev = jnp.where(first_of_seq, 0.0, prev[c][SUBLANES - 1:SUBLANES, :])
        x_prev = jnp.where(row == 0, last_prev, pltpu.roll(x, 1, 0))
        mu = mu_ref[:, c * half:(c + 1) * half]
        return x + (x_prev - x) * mu

    sh = [shifted(c) for c in range(7)]
    r = jnp.concatenate(sh[0:2], axis=1)
    k = jnp.concatenate(sh[2:4], axis=1)
    v = jnp.concatenate(sh[4:6], axis=1)
    lora = sh[6]
    w_log = -jax.nn.softplus(-(w0_ref[...] + _dot(jnp.tanh(lora), w2_ref[...], HIGHEST))) - 0.5
    a = _sigmoid(a0_ref[...] + _dot(lora, a2_ref[...], HIGHEST))
    g = _dot(_sigmoid(lora), g2_ref[...], HIGHEST)
    if has_vres:
        mix = _sigmoid(v0_ref[...] + _dot(_dot(v, v1_ref[...], HIGHEST), v2_ref[...], HIGHEST))
        v = v + (vfirst_ref[...] - v) * mix
    ones_bd = ones_ref[...]
    kk = k * kk_ref[...]
    kk = kk * lax.rsqrt(jnp.maximum(_head_sums(kk * kk, ones_bd), 1e-24))
    k = k * (1.0 + (a - 1.0) * ka_ref[...])
    r_out[...] = r
    lw_out[...] = -jnp.exp(w_log)
    k_out[...] = k
    v_out[...] = v
    a_out[...] = -kk
    b_out[...] = kk * a
    g_out[...] = g


def _wkv_kernel(r_ref, lw_ref, k_ref, v_ref, a_ref, b_ref, g_ref, lnw_ref, lnb_ref, rk_ref, out_ref,
                qp_s, yi_s, m_s, n_s, st_s, *, seq):
    L = WKV_CHUNK
    N = RWKV_HEAD_DIM
    n_chunks = seq // L
    ti = lax.broadcasted_iota(jnp.int32, (L, L), 0)
    si = lax.broadcasted_iota(jnp.int32, (L, L), 1)
    lag = (ti - si).astype(F32)
    tri_incl = jnp.clip(lag + 1.0, 0.0, 1.0)
    strict = si < ti
    incl = si <= ti
    eye = si == ti
    eye_f = tri_incl - jnp.clip(lag, 0.0, 1.0)

    def head_slice(ref, rows, h):
        return ref[rows, pl.ds(h * N, N)]

    def pass1(c, _):
        rows = pl.ds(pl.multiple_of(c * L, L), L)
        for h in range(2):
            lw = head_slice(lw_ref, rows, h)
            r = head_slice(r_ref, rows, h)
            k = head_slice(k_ref, rows, h)
            v = head_slice(v_ref, rows, h)
            a = head_slice(a_ref, rows, h)
            b = head_slice(b_ref, rows, h)
            cum = _dot(tri_incl, lw, HIGHEST)
            cum_end = cum[L - 1:L, :]
            g_inv = jnp.exp(-cum)
            g_end = jnp.exp(cum_end - cum)
            r_t = r * jnp.exp(cum)
            a_t = a * jnp.exp(cum - lw)
            b_t = b * g_inv
            k_t = k * g_inv
            b_e = b * g_end
            k_e = k * g_end
            a_ab = jnp.where(strict, _dot_nt(a_t, b_t, HIGHEST), 0.0)
            a_ak = jnp.where(strict, _dot_nt(a_t, k_t, HIGHEST), 0.0)
            a_rb = jnp.where(incl, _dot_nt(r_t, b_t, HIGHEST), 0.0)
            a_rk = jnp.where(incl, _dot_nt(r_t, k_t, HIGHEST), 0.0)
            tinv = eye_f + a_ab
            pw = a_ab
            for _ in range(5):
                pw = _dot(pw, pw, HIGHEST)
                tinv = tinv + _dot(tinv, pw, HIGHEST)
            w_m = _dot(tinv, a_t, HIGHEST)
            u_m = _dot(tinv, _dot(a_ak, v, HIGHEST), HIGHEST)
            qp_s[h, rows, :] = r_t + _dot(a_rb, w_m, HIGHEST)
            yi_s[h, rows, :] = _dot(a_rb, u_m, HIGHEST) + _dot(a_rk, v, HIGHEST)
            d_end = jnp.where(eye, jnp.broadcast_to(jnp.exp(cum_end), (N, N)), 0.0)
            m_s[h, c] = d_end + _dot_tn(b_e, w_m, HIGHEST)
            n_s[h, c] = _dot_tn(b_e, u_m, HIGHEST) + _dot_tn(k_e, v, HIGHEST)
        return 0

    lax.fori_loop(0, n_chunks, pass1, 0)

    def pass2(c, _):
        for h in range(2):
            st_s[h, c + 1] = _dot(m_s[h, c], st_s[h, c], HIGHEST) + n_s[h, c]
        return 0

    st_s[:, 0] = jnp.zeros((2, N, N), F32)
    lax.fori_loop(0, n_chunks - 1, pass2, 0)

    def pass3(c, _):
        rows = pl.ds(pl.multiple_of(c * L, L), L)
        outs = []
        for h in range(2):
            y = _dot(qp_s[h, rows, :], st_s[h, c], HIGHEST) + yi_s[h, rows, :]
            mean = jnp.mean(y, axis=-1, keepdims=True)
            yc = y - mean
            var = jnp.mean(yc * yc, axis=-1, keepdims=True)
            y = yc * lax.rsqrt(var + RWKV_GN_EPS)
            y = y * lnw_ref[:, pl.ds(h * N, N)] + lnb_ref[:, pl.ds(h * N, N)]
            r = head_slice(r_ref, rows, h)
            k = head_slice(k_ref, rows, h)
            v = head_slice(v_ref, rows, h)
            bonus = jnp.sum(r * k * rk_ref[:, pl.ds(h * N, N)], axis=-1, keepdims=True)
            outs.append((y + bonus * v) * head_slice(g_ref, rows, h))
        out_ref[rows, :] = jnp.concatenate(outs, axis=1).astype(out_ref.dtype)
        return 0

    lax.fori_loop(0, n_chunks, pass3, 0)


def _rwkv_mixer(proj, p_off, bsz, seq, layer, v_first, shift_mu, w0, w2, a0, a2, g2, k_k, k_a, r_k, ln_w, ln_b,
                v0, v1, v2, tm):
    m = proj.shape[0]
    width = w0.shape[1]
    half = width // 2
    n_lora = w2.shape[1] + a2.shape[1] + g2.shape[1]
    assert n_lora <= half and p_off % half == 0 and 3 * width % half == 0
    has_vres = layer > 0
    cb0 = p_off // half
    rows8 = tm // SUBLANES
    operands, in_specs = [], []
    for c in range(7):
        operands.append(proj)
        in_specs.append(pl.BlockSpec((tm, half), lambda i, c=c: (i, cb0 + c)))
    for c in range(7):
        operands.append(proj)
        in_specs.append(pl.BlockSpec((SUBLANES, half), lambda i, c=c: (jnp.maximum(i * rows8 - 1, 0), cb0 + c)))

    def full2(x):
        operands.append(x)
        in_specs.append(pl.BlockSpec(x.shape, lambda i: (0, 0)))

    def lrow(x):
        a3, spec = _layer_row(x, layer)
        operands.append(a3)
        in_specs.append(spec)

    mu = jnp.pad(shift_mu[layer], (0, 7 * half - shift_mu.shape[1]))[None]
    full2(mu)
    head_of_lane = np.arange(LANES) // RWKV_HEAD_DIM
    full2(jnp.asarray(head_of_lane[:, None] == head_of_lane[None, :], dtype=F32))
    d_lo, a_lo = w2.shape[1], a2.shape[1]
    w2p = jnp.pad(w2[layer], ((0, half - d_lo), (0, 0)))
    a2p = jnp.pad(a2[layer], ((d_lo, half - d_lo - a_lo), (0, 0)))
    g2p = jnp.pad(g2[layer], ((d_lo + a_lo, half - n_lora), (0, 0)))
    lrow(w0)
    full2(w2p)
    lrow(a0)
    full2(a2p)
    full2(g2p)
    lrow(k_k)
    lrow(k_a)
    if has_vres:
        operands.append(v_first)
        in_specs.append(pl.BlockSpec((tm, width), lambda i: (i, 0)))
        lrow_layer = layer - 1
        for x in (v0,):
            a3 = x.reshape(x.shape[0], 1, width)
            operands.append(a3)
            in_specs.append(pl.BlockSpec((None, 1, width), lambda i: (lrow_layer, 0, 0)))
        full2(v1[layer - 1])
        full2(v2[layer - 1])
    tile = pl.BlockSpec((tm, width), lambda i: (i, 0))
    shp = jax.ShapeDtypeStruct((m, width), F32)
    r, lw, k, v, a, b, g = pl.pallas_call(
        functools.partial(_rwkv_prep_kernel, has_vres=has_vres, tiles_per_seq=seq // tm, width=width, half=half),
        grid=(m // tm,),
        in_specs=in_specs,
        out_specs=[tile] * 7,
        out_shape=[shp] * 7,
        compiler_params=_cparams(("parallel",)),
        name="rwkv_prep",
    )(*operands)

    n_pairs = width // LANES
    seq_spec = pl.BlockSpec((None, seq, LANES), lambda bi, p: (bi, 0, p))
    row_spec = pl.BlockSpec((None, 1, LANES), lambda bi, p: (layer, 0, p))
    n_chunks = seq // WKV_CHUNK
    N = RWKV_HEAD_DIM
    y = pl.pallas_call(
        functools.partial(_wkv_kernel, seq=seq),
        grid=(bsz, n_pairs),
        in_specs=[seq_spec] * 7 + [row_spec] * 3,
        out_specs=seq_spec,
        out_shape=jax.ShapeDtypeStruct((bsz, seq, width), BF16),
        scratch_shapes=[
            pltpu.VMEM((2, seq, N), F32), pltpu.VMEM((2, seq, N), F32),
            pltpu.VMEM((2, n_chunks, N, N), F32), pltpu.VMEM((2, n_chunks, N, N), F32),
            pltpu.VMEM((2, n_chunks, N, N), F32),
        ],
        compiler_params=_cparams(("parallel", "parallel")),
        name="wkv7",
    )(*[x.reshape(bsz, seq, width) for x in (r, lw, k, v, a, b, g)],
      ln_w.reshape(-1, 1, width), ln_b.reshape(-1, 1, width), r_k.reshape(-1, 1, width))
    return y, (v_first if has_vres else v)


def _trunk(x, ffn1_norm, ffn1_up, ffn1_down, mix_norm, w_in, s5_a_re, s5_a_im, s5_b_re, s5_b_im, s5_c_re, s5_c_im,
           s5_log_dt, s5_d, s5_w_glu, attn_q_gain, attn_k_gain, rwkv_shift_mu, rwkv_w0, rwkv_w2, rwkv_a0, rwkv_a2,
           rwkv_g2, rwkv_k_k, rwkv_k_a, rwkv_r_k, rwkv_ln_w, rwkv_ln_b, rwkv_v0, rwkv_v1, rwkv_v2,
           w_branch_s5, w_branch_attn, w_branch_rwkv, w_out, ffn2_norm, ffn2_up, ffn2_down, *, tm):
    bsz, seq, d_model = x.shape
    depth = w_in.shape[0]
    m = bsz * seq
    s5_width = s5_d.shape[1]
    attn_width = len(DIL_PATTERNS) * ATTN_HEADS * ATTN_HEAD_DIM
    rwkv_width = rwkv_w0.shape[1]
    rwkv_part = rwkv_shift_mu.shape[1]
    q_off = s5_width
    p_off = s5_width + 3 * attn_width
    gate_off = p_off + rwkv_part
    tn_in = 512
    n_front = -(-gate_off // tn_in) * tn_in
    rk3 = rwkv_r_k.reshape(depth, rwkv_width)

    xf = x.reshape(m, d_model)
    v_first = None
    for l in range(depth):
        xf = _swiglu_ffn(xf, ffn1_norm, ffn1_up, ffn1_down, l, tm, tm)
        gain = _layer_row(mix_norm, l)
        proj = _mm("in_proj", [(xf, gain)], [(w_in, l, 0, 0)], [], _ep_plain, n_front, F32, tm, tn_in)
        gates = _mm("in_proj_gates", [(xf, gain)], [(w_in[l][:, gate_off:], None, 0, 0)], [], _ep_plain,
                    3 * d_model, F32, tm, tn_in)
        proj3 = proj.reshape(bsz, seq, n_front)

        z = _s5_mixer_pre_glu(proj3, l, s5_a_re, s5_a_im, s5_b_re, s5_b_im, s5_c_re, s5_c_im, s5_log_dt, s5_d)
        z = z.reshape(m, s5_width)
        y_s5 = _mm("s5_glu", [(z, None)], [(s5_w_glu, l, 0, 0)], [(z, 0)], _ep_glu, s5_width, BF16, tm, 256)

        y_attn = _attention(proj3, q_off, l, attn_q_gain, attn_k_gain).reshape(m, -1)

        y_rwkv, v_first = _rwkv_mixer(proj, p_off, bsz, seq, l, v_first, rwkv_shift_mu, rwkv_w0, rwkv_w2, rwkv_a0,
                                      rwkv_a2, rwkv_g2, rwkv_k_k, rwkv_k_a, rk3, rwkv_ln_w, rwkv_ln_b,
                                      rwkv_v0, rwkv_v1, rwkv_v2, min(tm, 256))
        y_rwkv = y_rwkv.reshape(m, rwkv_width)

        tn = 256
        gb = d_model // tn
        merged = _mm("branch_merge", [(y_s5, None), (y_attn, None), (y_rwkv, None)],
                     [(w_branch_s5, l, 0, 0), (w_branch_attn, l, 0, 1), (w_branch_rwkv, l, 0, 2)],
                     [(gates, 0), (gates, gb), (gates, 2 * gb)], _ep_gated_sum, d_model, BF16, tm, tn)
        xf = _mm("mix_out", [(merged, None)], [(w_out, l, 0, 0)], [(xf, 0)], _ep_residual, d_model, F32, tm, tn)
        xf = _swiglu_ffn(xf, ffn2_norm, ffn2_up, ffn2_down, l, tm, tm)
    return xf.reshape(bsz, seq, d_model)


def kernel(x, ffn1_norm, ffn1_up, ffn1_down, mix_norm, w_in, s5_a_re, s5_a_im, s5_b_re, s5_b_im, s5_c_re, s5_c_im, s5_log_dt, s5_d, s5_w_glu, attn_q_gain, attn_k_gain, rwkv_shift_mu, rwkv_w0, rwkv_w2, rwkv_a0, rwkv_a2, rwkv_g2, rwkv_k_k, rwkv_k_a, rwkv_r_k, rwkv_ln_w, rwkv_ln_b, rwkv_v0, rwkv_v1, rwkv_v2, w_branch_s5, w_branch_attn, w_branch_rwkv, w_out, ffn2_norm, ffn2_up, ffn2_down):
    return _trunk(x, ffn1_norm, ffn1_up, ffn1_down, mix_norm, w_in, s5_a_re, s5_a_im, s5_b_re, s5_b_im, s5_c_re,
                  s5_c_im, s5_log_dt, s5_d, s5_w_glu, attn_q_gain, attn_k_gain, rwkv_shift_mu, rwkv_w0, rwkv_w2,
                  rwkv_a0, rwkv_a2, rwkv_g2, rwkv_k_k, rwkv_k_a, rwkv_r_k, rwkv_ln_w, rwkv_ln_b, rwkv_v0, rwkv_v1,
                  rwkv_v2, w_branch_s5, w_branch_attn, w_branch_rwkv, w_out, ffn2_norm, ffn2_up, ffn2_down, tm=1024)
```

```python
import functools

import numpy as np
import jax
import jax.numpy as jnp
from jax import lax
from jax.experimental import pallas as pl
from jax.experimental.pallas import tpu as pltpu

F32 = jnp.float32
BF16 = jnp.bfloat16
HIGHEST = lax.Precision.HIGHEST

LANES = 128
SUBLANES = 8
VMEM_LIMIT_BYTES = 56 * 1024 * 1024

RMS_EPS = 1e-6
S5_GROUP = 16
S5_STATE = 64
S5_TBL_LEVEL = SUBLANES
S5_TBL_SEG = SUBLANES + 3
S5_N_TBL = SUBLANES + 4
ATTN_HEAD_DIM = 64
ATTN_HEADS = 8
DIL_PATTERNS = ((128, 1), (512, 4), (2048, 16))
ALIBI_MAX_BIAS = 8.0
ATTN_BLOCK = 128
RWKV_HEAD_DIM = 64
RWKV_GN_EPS = 64e-5
WKV_CHUNK = 64
NEG_BIG = -1e30


def _cparams(sem):
    return pltpu.CompilerParams(dimension_semantics=sem, vmem_limit_bytes=VMEM_LIMIT_BYTES)


def _dot(a, b, precision=None):
    return jnp.dot(a, b, preferred_element_type=F32, precision=precision)


def _dot_nt(a, b, precision=None):
    return lax.dot_general(a, b, (((1,), (1,)), ((), ())), preferred_element_type=F32, precision=precision)


def _dot_tn(a, b, precision=None):
    return lax.dot_general(a, b, (((0,), (0,)), ((), ())), preferred_element_type=F32, precision=precision)


def _sigmoid(x):
    return 1.0 / (1.0 + jnp.exp(-x))


def _layer_row(arr, layer):
    n = arr.shape[-1]
    a3 = arr.reshape(arr.shape[0], 1, n)
    return a3, pl.BlockSpec((None, 1, n), lambda *_: (layer, 0, 0))


def _mm_kernel(*refs, n_lhs, lhs_has_gain, lhs_needs_cast, rhs_lhs, n_extra, epilogue, tn):
    it = iter(refs)
    lhs_refs, gain_refs = [], []
    for p in range(n_lhs):
        lhs_refs.append(next(it))
        gain_refs.append(next(it) if lhs_has_gain[p] else None)
    rhs_refs = [next(it) for _ in rhs_lhs]
    extra_refs = [next(it) for _ in range(n_extra)]
    out_ref = next(it)
    scratch = {p: next(it) for p in range(n_lhs) if lhs_needs_cast[p]}

    @pl.when(pl.program_id(1) == 0)
    def _():
        for p in range(n_lhs):
            if not lhs_needs_cast[p]:
                continue
            a = lhs_refs[p][...].astype(F32)
            if lhs_has_gain[p]:
                a = a * lax.rsqrt(jnp.mean(a * a, axis=-1, keepdims=True) + RMS_EPS)
                a = a * gain_refs[p][...].astype(F32)
            scratch[p][...] = a.astype(BF16)

    accs = [None] * len(rhs_lhs)
    for p in range(n_lhs):
        mine = [r for r, q in enumerate(rhs_lhs) if q == p]
        a = scratch[p][...] if lhs_needs_cast[p] else lhs_refs[p][...]
        ws = [rhs_refs[r][...].astype(BF16) for r in mine]
        w = ws[0] if len(ws) == 1 else jnp.concatenate(ws, axis=1)
        acc = _dot(a, w)
        for n, r in enumerate(mine):
            accs[r] = acc[:, n * tn:(n + 1) * tn]
    out_ref[...] = epilogue(accs, [e[...] for e in extra_refs]).astype(out_ref.dtype)


def _mm(name, lhs, rhs, extras, epilogue, n_out, out_dtype, tm, tn):
    m = lhs[0][0].shape[0]
    operands, in_specs, scratch_shapes = [], [], []
    lhs_has_gain, lhs_needs_cast = [], []
    for a, gain in lhs:
        k = a.shape[1]
        operands.append(a)
        in_specs.append(pl.BlockSpec((tm, k), lambda i, j: (i, 0)))
        lhs_has_gain.append(gain is not None)
        if gain is not None:
            operands.append(gain[0])
            in_specs.append(gain[1])
        needs = gain is not None or a.dtype != BF16
        lhs_needs_cast.append(needs)
        if needs:
            scratch_shapes.append(pltpu.VMEM((tm, k), BF16))
    for w, layer, off, _ in rhs:
        operands.append(w)
        if layer is None:
            in_specs.append(pl.BlockSpec((w.shape[0], tn), lambda i, j, off=off: (0, off + j)))
        else:
            in_specs.append(pl.BlockSpec((None, w.shape[1], tn), lambda i, j, off=off, layer=layer: (layer, 0, off + j)))
    for e, off in extras:
        operands.append(e)
        in_specs.append(pl.BlockSpec((tm, tn), lambda i, j, off=off: (i, off + j)))
    kern = functools.partial(
        _mm_kernel, n_lhs=len(lhs), lhs_has_gain=tuple(lhs_has_gain), lhs_needs_cast=tuple(lhs_needs_cast),
        rhs_lhs=tuple(r[3] for r in rhs), n_extra=len(extras), epilogue=epilogue, tn=tn)
    return pl.pallas_call(
        kern,
        grid=(m // tm, n_out // tn),
        in_specs=in_specs,
        out_specs=pl.BlockSpec((tm, tn), lambda i, j: (i, j)),
        out_shape=jax.ShapeDtypeStruct((m, n_out), out_dtype),
        scratch_shapes=scratch_shapes,
        compiler_params=_cparams(("parallel", "arbitrary")),
        name=name,
    )(*operands)


def _ep_swiglu(accs, extras):
    a, b = accs
    return a * _sigmoid(a) * b


def _ep_half_residual(accs, extras):
    return extras[0] + 0.5 * accs[0]


def _ep_residual(accs, extras):
    return extras[0] + accs[0]


def _ep_plain(accs, extras):
    return accs[0]


def _ep_glu(accs, extras):
    z = extras[0]
    return z * _sigmoid(accs[0])


def _ep_gated_sum(accs, extras):
    out = _sigmoid(extras[0]) * accs[0]
    for g, a in zip(extras[1:], accs[1:]):
        out = out + _sigmoid(g) * a
    return out


def _swiglu_ffn(x, norm, w_up, w_down, layer, tm_up, tm_down):
    d_ff = w_down.shape[1]
    tn = LANES
    act = _mm("ffn_up", [(x, _layer_row(norm, layer))],
              [(w_up, layer, 0, 0), (w_up, layer, d_ff // tn, 0)], [], _ep_swiglu,
              d_ff, BF16, tm_up, tn)
    return _mm("ffn_down", [(act, None)], [(w_down, layer, 0, 0)], [(x, 0)], _ep_half_residual,
               x.shape[1], F32, tm_down, 256)


def _cmul(ar, ai, br, bi):
    return ar * br - ai * bi, ar * bi + ai * br


def _s5_prep_kernel(are_ref, aim_ref, ldt_ref, bre_ref, bim_ref, bbre_ref, bbim_ref, pw_ref):
    lam_re = jnp.minimum(are_ref[...], -1e-4)
    lam_im = aim_ref[...]
    dt = jnp.exp(ldt_ref[...])
    mag = jnp.exp(lam_re * dt)
    ab_re = mag * jnp.cos(lam_im * dt)
    ab_im = mag * jnp.sin(lam_im * dt)
    den = lam_re * lam_re + lam_im * lam_im
    num_re = ab_re - 1.0
    coef_re = (num_re * lam_re + ab_im * lam_im) / den
    coef_im = (ab_im * lam_re - num_re * lam_im) / den
    for h in range(bre_ref.shape[0]):
        br = bre_ref[h]
        bi = bim_ref[h]
        bbre_ref[h] = coef_re * br - coef_im * bi
        bbim_ref[h] = coef_re * bi + coef_im * br
    powers = [(ab_re, ab_im)]
    for _ in range(SUBLANES - 1):
        powers.append(_cmul(*powers[-1], ab_re, ab_im))
    for t, (pr, pi) in enumerate(powers):
        for r in range(SUBLANES):
            pw_ref[0, t, r] = pr
            pw_ref[1, t, r] = pi
    a8 = powers[-1]
    zero = jnp.zeros_like(ab_re)
    lvl_pow = a8
    for lvl in range(3):
        for r in range(SUBLANES):
            live = r >= (1 << lvl)
            pw_ref[0, S5_TBL_LEVEL + lvl, r] = lvl_pow[0] if live else zero
            pw_ref[1, S5_TBL_LEVEL + lvl, r] = lvl_pow[1] if live else zero
        lvl_pow = _cmul(*lvl_pow, *lvl_pow)
    pr, pi = a8
    for r in range(SUBLANES):
        pw_ref[0, S5_TBL_SEG, r] = pr
        pw_ref[1, S5_TBL_SEG, r] = pi
        pr, pi = _cmul(pr, pi, *a8)


def _s5_scan_kernel(u_ref, bre_ref, bim_ref, cre_ref, cim_ref, pw_ref, d_ref, z_ref, up_s, sre, sim, yp_s,
                    *, seq, lane_chunk):
    S = SUBLANES
    sb_rows = S * S
    n_sb = seq // sb_rows
    n_state = sre.shape[1]

    def transpose_superblocks(read_tile, write_tile):
        def body(sb, _):
            base = pl.multiple_of(sb * sb_rows, sb_rows)
            for t in range(S):
                write_tile(base, t, read_tile(base, t))
            return 0
        lax.fori_loop(0, n_sb, body, 0)

    def put_u(base, t, x):
        up_s[pl.ds(base + t * S, S), :] = x

    transpose_superblocks(lambda base, t: u_ref[pl.ds(base + t, S, stride=S), :], put_u)
    up = up_s[...].astype(BF16)
    sre[...] = _dot(up, bre_ref[...])
    sim[...] = _dot(up, bim_ref[...])

    def scan_superblock(sb, carry):
        base = pl.multiple_of(sb * sb_rows, sb_rows)
        new_carry = []
        for ch in range(n_state // lane_chunk):
            ln = pl.ds(ch * lane_chunk, lane_chunk)
            c_re, c_im = carry[2 * ch], carry[2 * ch + 1]
            ar, ai = pw_ref[0, 0, :, ln], pw_ref[1, 0, :, ln]
            sr, si = sre[pl.ds(base, S), ln], sim[pl.ds(base, S), ln]
            local = [(sr, si)]
            for t in range(1, S):
                rows = pl.ds(base + t * S, S)
                sr, si = sre[rows, ln] + ar * sr - ai * si, sim[rows, ln] + ar * si + ai * sr
                local.append((sr, si))
            er, ei = local[-1]
            for lvl in range(3):
                hr, hi = pw_ref[0, S5_TBL_LEVEL + lvl, :, ln], pw_ref[1, S5_TBL_LEVEL + lvl, :, ln]
                yr, yi = pltpu.roll(er, 1 << lvl, 0), pltpu.roll(ei, 1 << lvl, 0)
                er, ei = er + hr * yr - hi * yi, ei + hr * yi + hi * yr
            qr, qi = pw_ref[0, S5_TBL_SEG, :, ln], pw_ref[1, S5_TBL_SEG, :, ln]
            fr, fi = er + qr * c_re - qi * c_im, ei + qr * c_im + qi * c_re
            first = lax.broadcasted_iota(jnp.int32, fr.shape, 0) == 0
            in_r = jnp.where(first, c_re, pltpu.roll(fr, 1, 0))
            in_i = jnp.where(first, c_im, pltpu.roll(fi, 1, 0))
            for t in range(S):
                rows = pl.ds(base + t * S, S)
                pr, pi = pw_ref[0, t, :, ln], pw_ref[1, t, :, ln]
                sr, si = local[t]
                sre[rows, ln] = sr + pr * in_r - pi * in_i
                sim[rows, ln] = si + pr * in_i + pi * in_r
            new_carry.append(jnp.broadcast_to(fr[S - 1:S, :], fr.shape))
            new_carry.append(jnp.broadcast_to(fi[S - 1:S, :], fi.shape))
        return tuple(new_carry)

    zero = jnp.zeros((S, lane_chunk), F32)
    lax.fori_loop(0, n_sb, scan_superblock, (zero,) * (2 * (n_state // lane_chunk)))
    yp_s[...] = _dot(sre[...].astype(BF16), cre_ref[...]) - _dot(sim[...].astype(BF16), cim_ref[...])
    d = d_ref[...]

    def put_z(base, t, y):
        rows = pl.ds(base + t * S, S)
        z_ref[rows, :] = jax.nn.gelu(y + d * u_ref[rows, :])

    transpose_superblocks(lambda base, t: yp_s[pl.ds(base + t, S, stride=S), :], put_z)


def _s5_mixer_pre_glu(proj3, layer, a_re, a_im, b_re, b_im, c_re, c_im, log_dt, d_skip):
    bsz, seq, _ = proj3.shape
    n_grp, n_st = a_re.shape[1], a_re.shape[2]
    hc = b_re.shape[3]
    gpb = LANES // hc
    n_blk = n_grp // gpb
    nsl = gpb * n_st
    ldt = jnp.broadcast_to(log_dt[layer][:, None], (n_grp, n_st))
    b_re_t = jnp.transpose(b_re[layer], (2, 0, 1))
    b_im_t = jnp.transpose(b_im[layer], (2, 0, 1))
    bb_re, bb_im, pw = pl.pallas_call(
        _s5_prep_kernel,
        out_shape=(jax.ShapeDtypeStruct((hc, n_grp, n_st), F32), jax.ShapeDtypeStruct((hc, n_grp, n_st), F32),
                   jax.ShapeDtypeStruct((2, S5_N_TBL, SUBLANES, n_grp, n_st), F32)),
        name="s5_prep",
    )(a_re[layer], a_im[layer], ldt, b_re_t, b_im_t)
    eye = jnp.eye(gpb, dtype=F32)

    def blockdiag_in(bb):
        x = jnp.transpose(bb, (1, 0, 2)).reshape(n_blk, gpb, hc, n_st)
        x = x[:, :, :, None, :] * eye[None, :, None, :, None]
        return x.reshape(n_blk, gpb * hc, nsl)

    def blockdiag_out(c):
        x = jnp.transpose(c.reshape(n_blk, gpb, hc, n_st), (0, 1, 3, 2))
        x = x[:, :, :, None, :] * eye[None, :, None, :, None]
        return x.reshape(n_blk, nsl, gpb * hc)

    bmat_re, bmat_im = blockdiag_in(bb_re), blockdiag_in(bb_im)
    cmat_re, cmat_im = blockdiag_out(c_re[layer]), blockdiag_out(c_im[layer])
    bmat_re, bmat_im, cmat_re, cmat_im = (x.astype(BF16) for x in (bmat_re, bmat_im, cmat_re, cmat_im))
    pw_b = pw.reshape(2, S5_N_TBL, SUBLANES, n_blk, nsl).transpose(3, 0, 1, 2, 4)
    d3, d_spec = d_skip.reshape(d_skip.shape[0], 1, -1), pl.BlockSpec((None, 1, LANES), lambda b, g: (layer, 0, g))
    return pl.pallas_call(
        functools.partial(_s5_scan_kernel, seq=seq, lane_chunk=2 * LANES),
        grid=(bsz, n_blk),
        in_specs=[
            pl.BlockSpec((None, seq, LANES), lambda b, g: (b, 0, g)),
            pl.BlockSpec((None, LANES, nsl), lambda b, g: (g, 0, 0)),
            pl.BlockSpec((None, LANES, nsl), lambda b, g: (g, 0, 0)),
            pl.BlockSpec((None, nsl, LANES), lambda b, g: (g, 0, 0)),
            pl.BlockSpec((None, nsl, LANES), lambda b, g: (g, 0, 0)),
            pl.BlockSpec((None, 2, S5_N_TBL, SUBLANES, nsl), lambda b, g: (g, 0, 0, 0, 0)),
            d_spec,
        ],
        out_specs=pl.BlockSpec((None, seq, LANES), lambda b, g: (b, 0, g)),
        out_shape=jax.ShapeDtypeStruct((bsz, seq, n_grp * hc), F32),
        scratch_shapes=[pltpu.VMEM((seq, LANES), F32), pltpu.VMEM((seq, nsl), F32), pltpu.VMEM((seq, nsl), F32),
                        pltpu.VMEM((seq, LANES), F32)],
        compiler_params=_cparams(("parallel", "parallel")),
        name="s5_scan",
    )(proj3, bmat_re, bmat_im, cmat_re, cmat_im, pw_b, d3)


def _attn_kernel(slopes_ref, qg_ref, kg_ref, *refs, seq):
    qkv = refs[:9]
    out_ref = refs[9]
    qm, km, vm, base, o_seq, l_seq, o_all, l_all = refs[10:]
    pair = pl.program_id(1)
    hd = ATTN_HEAD_DIM
    blk = ATTN_BLOCK
    lane = lax.broadcasted_iota(jnp.int32, (1, LANES), 1)
    head_mask = [(lane < hd).astype(F32), (lane >= hd).astype(F32)]
    qg = qg_ref[...]
    kg = kg_ref[...]

    def qk_norm(x, gain):
        x2 = x * x
        ms = [jnp.sum(x2 * head_mask[h], axis=-1, keepdims=True) * (1.0 / hd) for h in range(2)]
        inv = head_mask[0] * lax.rsqrt(ms[0] + RMS_EPS) + head_mask[1] * lax.rsqrt(ms[1] + RMS_EPS)
        return x * inv * gain

    for g, (window, dil) in enumerate(DIL_PATTERNS):
        q_ref, k_ref, v_ref = qkv[3 * g:3 * g + 3]
        n = seq // dil
        n_blocks = n // blk
        n_back = window // dil
        assert n_back == blk and n % blk == 0
        rows = lax.broadcasted_iota(jnp.int32, (blk, 2 * blk), 0)
        cols = lax.broadcasted_iota(jnp.int32, (blk, 2 * blk), 1)
        delta = rows + blk - cols
        valid = (delta >= 0) & (delta <= n_back)
        for h in range(2):
            slope = slopes_ref[g * ATTN_HEADS + 2 * pair + h]
            base[h] = jnp.where(valid, (-slope * dil) * delta.astype(F32), NEG_BIG)

        def load_class(r, off):
            if dil == 1:
                q, k, v = q_ref[...], k_ref[...], v_ref[...]
            else:
                q = q_ref[pl.ds(r, n, stride=dil), :]
                k = k_ref[pl.ds(r, n, stride=dil), :]
                v = v_ref[pl.ds(r, n, stride=dil), :]
            qn = qk_norm(q, qg)
            km[pl.ds(off, n), :] = qk_norm(k, kg).astype(BF16)
            for h in range(2):
                qm[h, pl.ds(off, n), :] = (qn * head_mask[h]).astype(BF16)
                vm[h, pl.ds(off, n), :] = (v * head_mask[h]).astype(BF16)

        def one_block(off, t0, first, o_dst, l_dst):
            o_acc = jnp.zeros((blk, LANES), F32)
            l_acc = jnp.zeros((blk, LANES), F32)
            for h in range(2):
                qb = qm[h, pl.ds(off + t0, blk), :]
                if first:
                    kb = km[pl.ds(off + t0, blk), :]
                    vb = vm[h, pl.ds(off + t0, blk), :]
                    bias = base[h, :, blk:]
                else:
                    kb = km[pl.ds(off + t0 - blk, 2 * blk), :]
                    vb = vm[h, pl.ds(off + t0 - blk, 2 * blk), :]
                    bias = base[h]
                s = _dot_nt(qb, kb) * (hd ** -0.5) + bias
                m = jnp.max(s, axis=-1, keepdims=True)
                p = jnp.exp(s - m)
                l = jnp.sum(p, axis=-1, keepdims=True)
                pv = _dot(p.astype(BF16), vb)
                o_acc = o_acc + pv * (1.0 / l)
                l_acc = l_acc + head_mask[h] * (m + jnp.log(l))
            o_dst[pl.ds(off + t0, blk), :] = o_acc
            l_dst[pl.ds(off + t0, blk), :] = l_acc

        def store_class(r, off):
            o_all[g, pl.ds(r, n, stride=dil), :] = o_seq[pl.ds(off, n), :]
            l_all[g, pl.ds(r, n, stride=dil), :] = l_seq[pl.ds(off, n), :]

        if dil == 1:
            o_g, l_g = o_all.at[g], l_all.at[g]
            load_class(0, 0)
            one_block(0, 0, True, o_g, l_g)
            per_iter = 3
            assert (n_blocks - 1) % per_iter == 0

            def blocks_body(i, _):
                for j in range(per_iter):
                    one_block(0, pl.multiple_of((1 + i * per_iter + j) * blk, blk), False, o_g, l_g)
                return 0
            lax.fori_loop(0, (n_blocks - 1) // per_iter, blocks_body, 0)
        else:
            per_iter = max(1, 4 // n_blocks)
            assert dil % per_iter == 0 and per_iter * n <= seq

            def classes_body(i, _):
                for j in range(per_iter):
                    load_class(i * per_iter + j, j * n)
                for j in range(per_iter):
                    for b in range(n_blocks):
                        one_block(j * n, b * blk, b == 0, o_seq, l_seq)
                for j in range(per_iter):
                    store_class(i * per_iter + j, j * n)
                return 0
            lax.fori_loop(0, dil // per_iter, classes_body, 0)

    l0, l1, l2 = l_all[0], l_all[1], l_all[2]
    mx = jnp.maximum(jnp.maximum(l0, l1), l2)
    w0, w1, w2 = jnp.exp(l0 - mx), jnp.exp(l1 - mx), jnp.exp(l2 - mx)
    o = (w0 * o_all[0] + w1 * o_all[1] + w2 * o_all[2]) / (w0 + w1 + w2)
    out_ref[...] = o.astype(out_ref.dtype)


def _attention(proj3, q_off, layer, q_gain, k_gain):
    bsz, seq, _ = proj3.shape
    n_dil = len(DIL_PATTERNS)
    width = n_dil * ATTN_HEADS * ATTN_HEAD_DIM
    n_pairs = ATTN_HEADS * ATTN_HEAD_DIM // LANES
    n_heads = n_dil * ATTN_HEADS
    slopes = jnp.asarray(2.0 ** (-ALIBI_MAX_BIAS * (np.arange(n_heads) + 1) / n_heads), dtype=F32)
    qg = jnp.tile(q_gain[layer], 2)[None]
    kg = jnp.tile(k_gain[layer], 2)[None]
    in_specs = [pl.BlockSpec(memory_space=pltpu.SMEM),
                pl.BlockSpec((1, LANES), lambda b, p: (0, 0)), pl.BlockSpec((1, LANES), lambda b, p: (0, 0))]
    operands = [slopes, qg, kg]
    for g in range(n_dil):
        for which in range(3):
            cb = (q_off + which * width + g * ATTN_HEADS * ATTN_HEAD_DIM) // LANES
            in_specs.append(pl.BlockSpec((None, seq, LANES), lambda b, p, cb=cb: (b, 0, cb + p)))
            operands.append(proj3)
    return pl.pallas_call(
        functools.partial(_attn_kernel, seq=seq),
        grid=(bsz, n_pairs),
        in_specs=in_specs,
        out_specs=pl.BlockSpec((None, seq, LANES), lambda b, p: (b, 0, p)),
        out_shape=jax.ShapeDtypeStruct((bsz, seq, ATTN_HEADS * ATTN_HEAD_DIM), BF16),
        scratch_shapes=[
            pltpu.VMEM((2, seq, LANES), BF16), pltpu.VMEM((seq, LANES), BF16), pltpu.VMEM((2, seq, LANES), BF16),
            pltpu.VMEM((2, ATTN_BLOCK, 2 * ATTN_BLOCK), F32),
            pltpu.VMEM((seq, LANES), F32), pltpu.VMEM((seq, LANES), F32),
            pltpu.VMEM((n_dil, seq, LANES), F32), pltpu.VMEM((n_dil, seq, LANES), F32),
        ],
        compiler_params=_cparams(("parallel", "parallel")),
        name="dilated_attn",
    )(*operands)


def _split_bf16(x):
    hi = x.astype(BF16)
    return hi, (x - hi.astype(F32)).astype(BF16)


def _dot_x3(a, b):
    ah, al = _split_bf16(a)
    bh, bl = _split_bf16(b)
    return _dot(ah, bh) + (_dot(ah, bl) + _dot(al, bh))


def _head_sums(x, ones_bd):
    parts = []
    for c in range(0, x.shape[1], LANES):
        hi, lo = _split_bf16(x[:, c:c + LANES])
        parts.append(_dot(hi, ones_bd) + _dot(lo, ones_bd))
    return jnp.concatenate(parts, axis=1)


def _rwkv_prep_kernel(*refs, has_vres, tiles_per_seq, width, half):
    it = iter(refs)
    cur = [next(it) for _ in range(7)]
    prev = [next(it) for _ in range(7)]
    mu_ref, ones_ref, w0_ref, w2_ref, a0_ref, a2_ref, g2_ref, kk_ref, ka_ref = [next(it) for _ in range(9)]
    if has_vres:
        vfirst_ref, v0_ref, v1_ref, v2_ref = [next(it) for _ in range(4)]
    r_out, lw_out, k_out, v_out, a_out, b_out, g_out = [next(it) for _ in range(7)]

    tm = cur[0].shape[0]
    first_of_seq = (pl.program_id(0) % tiles_per_seq) == 0
    row = lax.broadcasted_iota(jnp.int32, (tm, half), 0)

    def shifted(c):
        x = cur[c][...]
        last_prev = jnp.where(first_of_seq, 0.0, prev[c][SUBLANES - 1:SUBLANES, :])
        x_prev = jnp.where(row == 0, last_prev, pltpu.roll(x, 1, 0))
        mu = mu_ref[:, c * half:(c + 1) * half]
        return x + (x_prev - x) * mu

    sh = [shifted(c) for c in range(7)]
    r = jnp.concatenate(sh[0:2], axis=1)
    k = jnp.concatenate(sh[2:4], axis=1)
    v = jnp.concatenate(sh[4:6], axis=1)
    lora = sh[6]
    w_log = -jax.nn.softplus(-(w0_ref[...] + _dot(jnp.tanh(lora).astype(BF16), w2_ref[...]))) - 0.5
    a = _sigmoid(a0_ref[...] + _dot(lora.astype(BF16), a2_ref[...]))
    g = _dot(_sigmoid(lora).astype(BF16), g2_ref[...])
    if has_vres:
        low = _dot(v.astype(BF16), v1_ref[...])
        mix = _sigmoid(v0_ref[...] + _dot(low.astype(BF16), v2_ref[...]))
        v = v + (vfirst_ref[...] - v) * mix
    ones_bd = ones_ref[...]
    kk = k * kk_ref[...]
    kk = kk * lax.rsqrt(jnp.maximum(_head_sums(kk * kk, ones_bd), 1e-24))
    k = k * (1.0 + (a - 1.0) * ka_ref[...])
    r_out[...] = r
    lw_out[...] = -jnp.exp(w_log)
    k_out[...] = k
    v_out[...] = v
    a_out[...] = -kk
    b_out[...] = kk * a
    g_out[...] = g


def _wkv_kernel(r_ref, lw_ref, k_ref, v_ref, a_ref, b_ref, g_ref, lnw_ref, lnb_ref, rk_ref, out_ref,
                qp_s, yi_s, m_s, n_s, st_s, pw_s, tinv_s, ak_s, rb_s, rk_s, ast_s, vst_s, be_s, ke_s, cend_s,
                *, seq, group):
    L = WKV_CHUNK
    N = RWKV_HEAD_DIM
    P = 2 * N
    assert P == LANES and L == N
    n_chunks = seq // L
    ti = lax.broadcasted_iota(jnp.int32, (L, L), 0)
    si = lax.broadcasted_iota(jnp.int32, (L, L), 1)
    tri_incl = jnp.clip((ti - si).astype(F32) + 1.0, 0.0, 1.0).astype(BF16)
    ri = lax.broadcasted_iota(jnp.int32, (P, P), 0)
    ci = lax.broadcasted_iota(jnp.int32, (P, P), 1)
    same_head = (ri // L) == (ci // N)
    strict = same_head & ((ci % L) < (ri % L))
    incl = same_head & ((ci % L) <= (ri % L))
    eye = ri == ci
    lane = lax.broadcasted_iota(jnp.int32, (1, P), 1)
    head_mask = [(lane < N).astype(F32), (lane >= N).astype(F32)]

    def stack(x):
        return jnp.concatenate([x * head_mask[0], x * head_mask[1]], axis=0)

    def dup(x):
        return jnp.concatenate([x, x], axis=0)

    def fold(x):
        return x[:L] + x[L:]

    def for_chunk_groups(fn, per_iter):
        def body(i, _):
            for j in range(per_iter):
                fn(i * per_iter + j)
            return 0
        lax.fori_loop(0, n_chunks // per_iter, body, 0)

    def chunk_rows(c):
        return pl.ds(pl.multiple_of(c * L, L), L)

    def stage_scores(c):
        rows = chunk_rows(c)
        lw, r, k, v, a, b = (ref[rows, :] for ref in (lw_ref, r_ref, k_ref, v_ref, a_ref, b_ref))
        lw_hi, lw_lo = _split_bf16(lw)
        cum = _dot(tri_incl, lw_hi) + _dot(tri_incl, lw_lo)
        cum_end = cum[L - 1:L, :]
        g_inv = jnp.exp(-cum)
        g_end = jnp.exp(cum_end - cum)
        r_t = r * jnp.exp(cum)
        a_st = stack(a * jnp.exp(cum - lw)).astype(BF16)
        r_st = stack(r_t).astype(BF16)
        bk_du = jnp.concatenate([dup(b * g_inv), dup(k * g_inv)], axis=0).astype(BF16)
        ast_s[c] = a_st
        vst_s[c] = stack(v).astype(BF16)
        be_s[c] = stack(b * g_end).astype(BF16)
        ke_s[c] = stack(k * g_end).astype(BF16)
        qp_s[rows, :] = r_t
        cend_s[c] = jnp.broadcast_to(cum_end, (SUBLANES, P))
        a_bk = _dot_nt(a_st, bk_du)
        r_bk = _dot_nt(r_st, bk_du)
        a_ab = jnp.where(strict, a_bk[:, :P], 0.0)
        pw_s[c] = a_ab.astype(BF16)
        tinv_s[c] = jnp.where(eye, 1.0, a_ab)
        ak_s[c] = jnp.where(strict, a_bk[:, P:], 0.0).astype(BF16)
        rb_s[c] = jnp.where(incl, r_bk[:, :P], 0.0).astype(BF16)
        rk_s[c] = jnp.where(incl, r_bk[:, P:], 0.0).astype(BF16)

    def stage_inverse_level(c):
        pw = pw_s[c]
        pw = _dot(pw, pw).astype(BF16)
        pw_s[c] = pw
        tinv = tinv_s[c]
        tinv_s[c] = tinv + _dot(tinv.astype(BF16), pw)

    def stage_summary(c):
        rows = chunk_rows(c)
        tinv = tinv_s[c].astype(BF16)
        v_st = vst_s[c]
        be_st = be_s[c]
        a_rb = rb_s[c]
        av = _dot(ak_s[c], v_st).astype(BF16)
        wu = _dot(tinv, jnp.concatenate([ast_s[c], av], axis=1)).astype(BF16)
        rb_wu = _dot(a_rb, wu)
        be_wu = _dot_tn(be_st, wu)
        qp_s[rows, :] = qp_s[rows, :] + fold(rb_wu[:, :P])
        yi_s[rows, :] = fold(rb_wu[:, P:] + _dot(rk_s[c], v_st))
        d_end = jnp.where(eye, jnp.broadcast_to(jnp.exp(cend_s[c][0:1, :]), (P, P)), 0.0)
        m_s[c] = d_end + be_wu[:, :P]
        n_s[c] = be_wu[:, P:] + _dot_tn(ke_s[c], v_st)

    for_chunk_groups(stage_scores, group)

    def inverse_levels(_, carry):
        for_chunk_groups(stage_inverse_level, group)
        return carry

    n_levels = (L - 1).bit_length() - 1
    lax.fori_loop(0, n_levels, inverse_levels, 0)
    for_chunk_groups(stage_summary, group)

    def pass2(c, _):
        st_s[c + 1] = _dot_x3(m_s[c], st_s[c]) + n_s[c]
        return 0

    st_s[0] = jnp.zeros((P, P), F32)
    lax.fori_loop(0, n_chunks - 1, pass2, 0)

    def head_stat(x):
        s0 = jnp.sum(x * head_mask[0], axis=-1, keepdims=True)
        s1 = jnp.sum(x * head_mask[1], axis=-1, keepdims=True)
        return s0 * head_mask[0] + s1 * head_mask[1]

    def emit(c):
        rows = chunk_rows(c)
        y = _dot(qp_s[rows, :].astype(BF16), st_s[c].astype(BF16)) + yi_s[rows, :]
        yc = y - head_stat(y) * (1.0 / N)
        var = head_stat(yc * yc) * (1.0 / N)
        y = yc * lax.rsqrt(var + RWKV_GN_EPS) * lnw_ref[...] + lnb_ref[...]
        bonus = head_stat(r_ref[rows, :] * k_ref[rows, :] * rk_ref[...])
        out_ref[rows, :] = ((y + bonus * v_ref[rows, :]) * g_ref[rows, :]).astype(out_ref.dtype)

    for_chunk_groups(emit, min(group, 4))


def _rwkv_mixer(proj, p_off, bsz, seq, layer, v_first, shift_mu, w0, w2, a0, a2, g2, k_k, k_a, r_k, ln_w, ln_b,
                v0, v1, v2, tm):
    m = proj.shape[0]
    width = w0.shape[1]
    half = width // 2
    n_lora = w2.shape[1] + a2.shape[1] + g2.shape[1]
    assert n_lora <= half and p_off % half == 0 and 3 * width % half == 0
    has_vres = layer > 0
    cb0 = p_off // half
    rows8 = tm // SUBLANES
    operands, in_specs = [], []
    for c in range(7):
        operands.append(proj)
        in_specs.append(pl.BlockSpec((tm, half), lambda i, c=c: (i, cb0 + c)))
    for c in range(7):
        operands.append(proj)
        in_specs.append(pl.BlockSpec((SUBLANES, half), lambda i, c=c: (jnp.maximum(i * rows8 - 1, 0), cb0 + c)))

    def full2(x):
        operands.append(x)
        in_specs.append(pl.BlockSpec(x.shape, lambda i: (0, 0)))

    def lrow(x):
        a3, spec = _layer_row(x, layer)
        operands.append(a3)
        in_specs.append(spec)

    mu = jnp.pad(shift_mu[layer], (0, 7 * half - shift_mu.shape[1]))[None]
    full2(mu)
    head_of_lane = np.arange(LANES) // RWKV_HEAD_DIM
    full2(jnp.asarray(head_of_lane[:, None] == head_of_lane[None, :], dtype=F32))
    d_lo, a_lo = w2.shape[1], a2.shape[1]
    w2p = jnp.pad(w2[layer], ((0, half - d_lo), (0, 0)))
    a2p = jnp.pad(a2[layer], ((d_lo, half - d_lo - a_lo), (0, 0)))
    g2p = jnp.pad(g2[layer], ((d_lo + a_lo, half - n_lora), (0, 0)))
    lrow(w0)
    full2(w2p.astype(BF16))
    lrow(a0)
    full2(a2p.astype(BF16))
    full2(g2p.astype(BF16))
    lrow(k_k)
    lrow(k_a)
    if has_vres:
        operands.append(v_first)
        in_specs.append(pl.BlockSpec((tm, width), lambda i: (i, 0)))
        lrow_layer = layer - 1
        for x in (v0,):
            a3 = x.reshape(x.shape[0], 1, width)
            operands.append(a3)
            in_specs.append(pl.BlockSpec((None, 1, width), lambda i: (lrow_layer, 0, 0)))
        full2(v1[layer - 1].astype(BF16))
        full2(v2[layer - 1].astype(BF16))
    tile = pl.BlockSpec((tm, width), lambda i: (i, 0))
    shp = jax.ShapeDtypeStruct((m, width), F32)
    r, lw, k, v, a, b, g = pl.pallas_call(
        functools.partial(_rwkv_prep_kernel, has_vres=has_vres, tiles_per_seq=seq // tm, width=width, half=half),
        grid=(m // tm,),
        in_specs=in_specs,
        out_specs=[tile] * 7,
        out_shape=[shp] * 7,
        compiler_params=_cparams(("parallel",)),
        name="rwkv_prep",
    )(*operands)

    n_pairs = width // LANES
    seq_spec = pl.BlockSpec((None, seq, LANES), lambda bi, p: (bi, 0, p))
    row_spec = pl.BlockSpec((None, 1, LANES), lambda bi, p: (layer, 0, p))
    n_chunks = seq // WKV_CHUNK
    N = RWKV_HEAD_DIM
    y = pl.pallas_call(
        functools.partial(_wkv_kernel, seq=seq, group=min(8, n_chunks)),
        grid=(bsz, n_pairs),
        in_specs=[seq_spec] * 7 + [row_spec] * 3,
        out_specs=seq_spec,
        out_shape=jax.ShapeDtypeStruct((bsz, seq, width), BF16),
        scratch_shapes=(
            [pltpu.VMEM((seq, LANES), F32)] * 2
            + [pltpu.VMEM((n_chunks, LANES, LANES), F32)] * 3
            + [pltpu.VMEM((n_chunks, LANES, LANES), BF16)]
            + [pltpu.VMEM((n_chunks, LANES, LANES), F32)]
            + [pltpu.VMEM((n_chunks, LANES, LANES), BF16)] * 7
            + [pltpu.VMEM((n_chunks, SUBLANES, LANES), F32)]
        ),
        compiler_params=_cparams(("parallel", "parallel")),
        name="wkv7",
    )(*[x.reshape(bsz, seq, width) for x in (r, lw, k, v, a, b, g)],
      ln_w.reshape(-1, 1, width), ln_b.reshape(-1, 1, width), r_k.reshape(-1, 1, width))
    return y, (v_first if has_vres else v)


def _trunk(x, ffn1_norm, ffn1_up, ffn1_down, mix_norm, w_in, s5_a_re, s5_a_im, s5_b_re, s5_b_im, s5_c_re, s5_c_im,
           s5_log_dt, s5_d, s5_w_glu, attn_q_gain, attn_k_gain, rwkv_shift_mu, rwkv_w0, rwkv_w2, rwkv_a0, rwkv_a2,
           rwkv_g2, rwkv_k_k, rwkv_k_a, rwkv_r_k, rwkv_ln_w, rwkv_ln_b, rwkv_v0, rwkv_v1, rwkv_v2,
           w_branch_s5, w_branch_attn, w_branch_rwkv, w_out, ffn2_norm, ffn2_up, ffn2_down, *, tm):
    bsz, seq, d_model = x.shape
    depth = w_in.shape[0]
    m = bsz * seq
    s5_width = s5_d.shape[1]
    attn_width = len(DIL_PATTERNS) * ATTN_HEADS * ATTN_HEAD_DIM
    rwkv_width = rwkv_w0.shape[1]
    rwkv_part = rwkv_shift_mu.shape[1]
    q_off = s5_width
    p_off = s5_width + 3 * attn_width
    gate_off = p_off + rwkv_part
    tn_in = 512
    n_front = -(-gate_off // tn_in) * tn_in
    rk3 = rwkv_r_k.reshape(depth, rwkv_width)

    xf = x.reshape(m, d_model)
    v_first = None
    for l in range(depth):
        xf = _swiglu_ffn(xf, ffn1_norm, ffn1_up, ffn1_down, l, tm, tm)
        gain = _layer_row(mix_norm, l)
        proj = _mm("in_proj", [(xf, gain)], [(w_in, l, 0, 0)], [], _ep_plain, n_front, F32, tm, tn_in)
        gates = _mm("in_proj_gates", [(xf, gain)], [(w_in[l][:, gate_off:], None, 0, 0)], [], _ep_plain,
                    3 * d_model, F32, tm, tn_in)
        proj3 = proj.reshape(bsz, seq, n_front)

        z = _s5_mixer_pre_glu(proj3, l, s5_a_re, s5_a_im, s5_b_re, s5_b_im, s5_c_re, s5_c_im, s5_log_dt, s5_d)
        z = z.reshape(m, s5_width)
        y_s5 = _mm("s5_glu", [(z, None)], [(s5_w_glu, l, 0, 0)], [(z, 0)], _ep_glu, s5_width, BF16, tm, 256)

        y_attn = _attention(proj3, q_off, l, attn_q_gain, attn_k_gain).reshape(m, -1)

        y_rwkv, v_first = _rwkv_mixer(proj, p_off, bsz, seq, l, v_first, rwkv_shift_mu, rwkv_w0, rwkv_w2, rwkv_a0,
                                      rwkv_a2, rwkv_g2, rwkv_k_k, rwkv_k_a, rk3, rwkv_ln_w, rwkv_ln_b,
                                      rwkv_v0, rwkv_v1, rwkv_v2, min(tm, 256))
        y_rwkv = y_rwkv.reshape(m, rwkv_width)

        tn = 256
        gb = d_model // tn
        merged = _mm("branch_merge", [(y_s5, None), (y_attn, None), (y_rwkv, None)],
                     [(w_branch_s5, l, 0, 0), (w_branch_attn, l, 0, 1), (w_branch_rwkv, l, 0, 2)],
                     [(gates, 0), (gates, gb), (gates, 2 * gb)], _ep_gated_sum, d_model, BF16, tm, tn)
        xf = _mm("mix_out", [(merged, None)], [(w_out, l, 0, 0)], [(xf, 0)], _ep_residual, d_model, F32, tm, tn)
        xf = _swiglu_ffn(xf, ffn2_norm, ffn2_up, ffn2_down, l, tm, tm)
    return xf.reshape(bsz, seq, d_model)


def kernel(x, ffn1_norm, ffn1_up, ffn1_down, mix_norm, w_in, s5_a_re, s5_a_im, s5_b_re, s5_b_im, s5_c_re, s5_c_im, s5_log_dt, s5_d, s5_w_glu, attn_q_gain, attn_k_gain, rwkv_shift_mu, rwkv_w0, rwkv_w2, rwkv_a0, rwkv_a2, rwkv_g2, rwkv_k_k, rwkv_k_a, rwkv_r_k, rwkv_ln_w, rwkv_ln_b, rwkv_v0, rwkv_v1, rwkv_v2, w_branch_s5, w_branch_attn, w_branch_rwkv, w_out, ffn2_norm, ffn2_up, ffn2_down):
    return _trunk(x, ffn1_norm, ffn1_up, ffn1_down, mix_norm, w_in, s5_a_re, s5_a_im, s5_b_re, s5_b_im, s5_c_re,
                  s5_c_im, s5_log_dt, s5_d, s5_w_glu, attn_q_gain, attn_k_gain, rwkv_shift_mu, rwkv_w0, rwkv_w2,
                  rwkv_a0, rwkv_a2, rwkv_g2, rwkv_k_k, rwkv_k_a, rwkv_r_k, rwkv_ln_w, rwkv_ln_b, rwkv_v0, rwkv_v1,
                  rwkv_v2, w_branch_s5, w_branch_attn, w_branch_rwkv, w_out, ffn2_norm, ffn2_up, ffn2_down, tm=1024)
```

```python
import functools

import numpy as np
import jax
import jax.numpy as jnp
from jax import lax
from jax.experimental import pallas as pl
from jax.experimental.pallas import tpu as pltpu

F32 = jnp.float32
BF16 = jnp.bfloat16
HIGHEST = lax.Precision.HIGHEST

LANES = 128
SUBLANES = 8
VMEM_LIMIT_BYTES = 56 * 1024 * 1024

RMS_EPS = 1e-6
S5_GROUP = 16
S5_STATE = 64
S5_TBL_LEVEL = SUBLANES
S5_TBL_SEG = SUBLANES + 3
S5_N_TBL = SUBLANES + 4
ATTN_HEAD_DIM = 64
ATTN_HEADS = 8
DIL_PATTERNS = ((128, 1), (512, 4), (2048, 16))
ALIBI_MAX_BIAS = 8.0
ATTN_BLOCK = 128
RWKV_HEAD_DIM = 64
RWKV_GN_EPS = 64e-5
WKV_CHUNK = 64
NEG_BIG = -1e30


def _cparams(sem):
    return pltpu.CompilerParams(dimension_semantics=sem, vmem_limit_bytes=VMEM_LIMIT_BYTES)


def _dot(a, b, precision=None):
    return jnp.dot(a, b, preferred_element_type=F32, precision=precision)


def _dot_nt(a, b, precision=None):
    return lax.dot_general(a, b, (((1,), (1,)), ((), ())), preferred_element_type=F32, precision=precision)


def _dot_tn(a, b, precision=None):
    return lax.dot_general(a, b, (((0,), (0,)), ((), ())), preferred_element_type=F32, precision=precision)


def _sigmoid(x):
    return 1.0 / (1.0 + jnp.exp(-x))


def _layer_row(arr, layer):
    n = arr.shape[-1]
    a3 = arr.reshape(arr.shape[0], 1, n)
    return a3, pl.BlockSpec((None, 1, n), lambda *_: (layer, 0, 0))


def _rmsnorm_kernel(x_ref, g_ref, o_ref):
    x = x_ref[...]
    y = x * lax.rsqrt(jnp.mean(x * x, axis=-1, keepdims=True) + RMS_EPS)
    o_ref[...] = (y * g_ref[...]).astype(o_ref.dtype)


def _rmsnorm(x, gain, layer):
    m, d = x.shape
    tm = min(m, 512)
    g3, g_spec = _layer_row(gain, layer)
    return pl.pallas_call(
        _rmsnorm_kernel,
        grid=(m // tm,),
        in_specs=[pl.BlockSpec((tm, d), lambda i: (i, 0)), g_spec],
        out_specs=pl.BlockSpec((tm, d), lambda i: (i, 0)),
        out_shape=jax.ShapeDtypeStruct((m, d), BF16),
        compiler_params=_cparams(("parallel",)),
        name="rmsnorm",
    )(x, g3)


def _mm_kernel(*refs, n_lhs, lhs_needs_cast, rhs_lhs, n_extra, epilogue, tn, row_chunk):
    it = iter(refs)
    lhs_refs = [next(it) for _ in range(n_lhs)]
    rhs_refs = [next(it) for _ in rhs_lhs]
    extra_refs = [next(it) for _ in range(n_extra)]
    out_ref = next(it)
    lhs_scr = {p: next(it) for p in range(n_lhs) if lhs_needs_cast[p]}
    w_scr = [next(it) for _ in range(n_lhs)]

    @pl.when(pl.program_id(1) == 0)
    def _():
        for p, scr in lhs_scr.items():
            scr[...] = lhs_refs[p][...].astype(BF16)

    for p in range(n_lhs):
        mine = [r for r, q in enumerate(rhs_lhs) if q == p]
        for n, r in enumerate(mine):
            w_scr[p][:, n * tn:(n + 1) * tn] = rhs_refs[r][...].astype(BF16)

    def chunk(c):
        rows = pl.ds(c * row_chunk, row_chunk)
        accs = [None] * len(rhs_lhs)
        for p in range(n_lhs):
            mine = [r for r, q in enumerate(rhs_lhs) if q == p]
            a = (lhs_scr[p] if lhs_needs_cast[p] else lhs_refs[p])[rows, :]
            acc = _dot(a, w_scr[p][...])
            for n, r in enumerate(mine):
                accs[r] = acc[:, n * tn:(n + 1) * tn]
        out_ref[rows, :] = epilogue(accs, [e[rows, :] for e in extra_refs]).astype(out_ref.dtype)

    for c in range(out_ref.shape[0] // row_chunk):
        chunk(c)


def _mm(name, lhs, rhs, extras, epilogue, n_out, out_dtype, tm, tn, row_chunk=1024):
    m = lhs[0].shape[0]
    tm = min(tm, m)
    row_chunk = min(row_chunk, tm)
    operands, in_specs, lhs_scratch, w_scratch = [], [], [], []
    lhs_needs_cast = []
    for p, a in enumerate(lhs):
        k = a.shape[1]
        operands.append(a)
        in_specs.append(pl.BlockSpec((tm, k), lambda i, j: (i, 0), pipeline_mode=pl.Buffered(1)))
        lhs_needs_cast.append(a.dtype != BF16)
        if a.dtype != BF16:
            lhs_scratch.append(pltpu.VMEM((tm, k), BF16))
        w_scratch.append(pltpu.VMEM((k, tn * sum(1 for r in rhs if r[3] == p)), BF16))
    for w, layer, off, _ in rhs:
        operands.append(w)
        if layer is None:
            in_specs.append(pl.BlockSpec((w.shape[0], tn), lambda i, j, off=off: (0, off + j)))
        else:
            in_specs.append(pl.BlockSpec((None, w.shape[1], tn), lambda i, j, off=off, layer=layer: (layer, 0, off + j)))
    for e, off in extras:
        operands.append(e)
        in_specs.append(pl.BlockSpec((tm, tn), lambda i, j, off=off: (i, off + j)))
    kern = functools.partial(
        _mm_kernel, n_lhs=len(lhs), lhs_needs_cast=tuple(lhs_needs_cast), rhs_lhs=tuple(r[3] for r in rhs),
        n_extra=len(extras), epilogue=epilogue, tn=tn, row_chunk=row_chunk)
    return pl.pallas_call(
        kern,
        grid=(m // tm, n_out // tn),
        in_specs=in_specs,
        out_specs=pl.BlockSpec((tm, tn), lambda i, j: (i, j)),
        out_shape=jax.ShapeDtypeStruct((m, n_out), out_dtype),
        scratch_shapes=lhs_scratch + w_scratch,
        compiler_params=_cparams(("parallel", "arbitrary")),
        name=name,
    )(*operands)


def _ep_swiglu(accs, extras):
    a, b = accs
    return a * _sigmoid(a) * b


def _ep_half_residual(accs, extras):
    return extras[0] + 0.5 * accs[0]


def _ep_residual(accs, extras):
    return extras[0] + accs[0]


def _ep_plain(accs, extras):
    return accs[0]


def _ep_glu(accs, extras):
    z = extras[0]
    return z * _sigmoid(accs[0])


def _ep_gated_sum(accs, extras):
    out = _sigmoid(extras[0]) * accs[0]
    for g, a in zip(extras[1:], accs[1:]):
        out = out + _sigmoid(g) * a
    return out


def _swiglu_ffn(x, norm, w_up, w_down, layer, tm_up, tm_down):
    d_ff = w_down.shape[1]
    tn = LANES
    h = _rmsnorm(x, norm, layer)
    act = _mm("ffn_up", [h], [(w_up, layer, 0, 0), (w_up, layer, d_ff // tn, 0)], [], _ep_swiglu,
              d_ff, BF16, tm_up, tn)
    return _mm("ffn_down", [act], [(w_down, layer, 0, 0)], [(x, 0)], _ep_half_residual,
               x.shape[1], F32, tm_down, 256, row_chunk=512)


def _cmul(ar, ai, br, bi):
    return ar * br - ai * bi, ar * bi + ai * br


def _s5_prep_kernel(are_ref, aim_ref, ldt_ref, bre_ref, bim_ref, bbre_ref, bbim_ref, pw_ref):
    lam_re = jnp.minimum(are_ref[...], -1e-4)
    lam_im = aim_ref[...]
    dt = jnp.exp(ldt_ref[...])
    mag = jnp.exp(lam_re * dt)
    ab_re = mag * jnp.cos(lam_im * dt)
    ab_im = mag * jnp.sin(lam_im * dt)
    den = lam_re * lam_re + lam_im * lam_im
    num_re = ab_re - 1.0
    coef_re = (num_re * lam_re + ab_im * lam_im) / den
    coef_im = (ab_im * lam_re - num_re * lam_im) / den
    for h in range(bre_ref.shape[0]):
        br = bre_ref[h]
        bi = bim_ref[h]
        bbre_ref[h] = coef_re * br - coef_im * bi
        bbim_ref[h] = coef_re * bi + coef_im * br
    powers = [(ab_re, ab_im)]
    for _ in range(SUBLANES - 1):
        powers.append(_cmul(*powers[-1], ab_re, ab_im))
    for t, (pr, pi) in enumerate(powers):
        for r in range(SUBLANES):
            pw_ref[0, t, r] = pr
            pw_ref[1, t, r] = pi
    a8 = powers[-1]
    zero = jnp.zeros_like(ab_re)
    lvl_pow = a8
    for lvl in range(3):
        for r in range(SUBLANES):
            live = r >= (1 << lvl)
            pw_ref[0, S5_TBL_LEVEL + lvl, r] = lvl_pow[0] if live else zero
            pw_ref[1, S5_TBL_LEVEL + lvl, r] = lvl_pow[1] if live else zero
        lvl_pow = _cmul(*lvl_pow, *lvl_pow)
    pr, pi = a8
    for r in range(SUBLANES):
        pw_ref[0, S5_TBL_SEG, r] = pr
        pw_ref[1, S5_TBL_SEG, r] = pi
        pr, pi = _cmul(pr, pi, *a8)


def _s5_scan_kernel(u_ref, bre_ref, bim_ref, cre_ref, cim_ref, pw_ref, d_ref, z_ref, up_s, sre, sim, yp_s,
                    *, seq, lane_chunk):
    S = SUBLANES
    sb_rows = S * S
    n_sb = seq // sb_rows
    n_state = sre.shape[1]

    def transpose_superblocks(read_tile, write_tile):
        def body(sb, _):
            base = pl.multiple_of(sb * sb_rows, sb_rows)
            for t in range(S):
                write_tile(base, t, read_tile(base, t))
            return 0
        lax.fori_loop(0, n_sb, body, 0)

    def put_u(base, t, x):
        up_s[pl.ds(base + t * S, S), :] = x

    transpose_superblocks(lambda base, t: u_ref[pl.ds(base + t, S, stride=S), :], put_u)
    up = up_s[...].astype(BF16)
    sre[...] = _dot(up, bre_ref[...])
    sim[...] = _dot(up, bim_ref[...])

    def scan_superblock(sb, carry):
        base = pl.multiple_of(sb * sb_rows, sb_rows)
        new_carry = []
        for ch in range(n_state // lane_chunk):
            ln = pl.ds(ch * lane_chunk, lane_chunk)
            c_re, c_im = carry[2 * ch], carry[2 * ch + 1]
            ar, ai = pw_ref[0, 0, :, ln], pw_ref[1, 0, :, ln]
            sr, si = sre[pl.ds(base, S), ln], sim[pl.ds(base, S), ln]
            local = [(sr, si)]
            for t in range(1, S):
                rows = pl.ds(base + t * S, S)
                sr, si = sre[rows, ln] + ar * sr - ai * si, sim[rows, ln] + ar * si + ai * sr
                local.append((sr, si))
            er, ei = local[-1]
            for lvl in range(3):
                hr, hi = pw_ref[0, S5_TBL_LEVEL + lvl, :, ln], pw_ref[1, S5_TBL_LEVEL + lvl, :, ln]
                yr, yi = pltpu.roll(er, 1 << lvl, 0), pltpu.roll(ei, 1 << lvl, 0)
                er, ei = er + hr * yr - hi * yi, ei + hr * yi + hi * yr
            qr, qi = pw_ref[0, S5_TBL_SEG, :, ln], pw_ref[1, S5_TBL_SEG, :, ln]
            fr, fi = er + qr * c_re - qi * c_im, ei + qr * c_im + qi * c_re
            first = lax.broadcasted_iota(jnp.int32, fr.shape, 0) == 0
            in_r = jnp.where(first, c_re, pltpu.roll(fr, 1, 0))
            in_i = jnp.where(first, c_im, pltpu.roll(fi, 1, 0))
            for t in range(S):
                rows = pl.ds(base + t * S, S)
                pr, pi = pw_ref[0, t, :, ln], pw_ref[1, t, :, ln]
                sr, si = local[t]
                sre[rows, ln] = sr + pr * in_r - pi * in_i
                sim[rows, ln] = si + pr * in_i + pi * in_r
            new_carry.append(jnp.broadcast_to(fr[S - 1:S, :], fr.shape))
            new_carry.append(jnp.broadcast_to(fi[S - 1:S, :], fi.shape))
        return tuple(new_carry)

    zero = jnp.zeros((S, lane_chunk), F32)
    lax.fori_loop(0, n_sb, scan_superblock, (zero,) * (2 * (n_state // lane_chunk)))
    yp_s[...] = _dot(sre[...].astype(BF16), cre_ref[...]) - _dot(sim[...].astype(BF16), cim_ref[...])
    d = d_ref[...]

    def put_z(base, t, y):
        rows = pl.ds(base + t * S, S)
        z_ref[rows, :] = jax.nn.gelu(y + d * u_ref[rows, :])

    transpose_superblocks(lambda base, t: yp_s[pl.ds(base + t, S, stride=S), :], put_z)


def _s5_mixer_pre_glu(proj3, layer, a_re, a_im, b_re, b_im, c_re, c_im, log_dt, d_skip):
    bsz, seq, _ = proj3.shape
    n_grp, n_st = a_re.shape[1], a_re.shape[2]
    hc = b_re.shape[3]
    gpb = LANES // hc
    n_blk = n_grp // gpb
    nsl = gpb * n_st
    ldt = jnp.broadcast_to(log_dt[layer][:, None], (n_grp, n_st))
    b_re_t = jnp.transpose(b_re[layer], (2, 0, 1))
    b_im_t = jnp.transpose(b_im[layer], (2, 0, 1))
    bb_re, bb_im, pw = pl.pallas_call(
        _s5_prep_kernel,
        out_shape=(jax.ShapeDtypeStruct((hc, n_grp, n_st), F32), jax.ShapeDtypeStruct((hc, n_grp, n_st), F32),
                   jax.ShapeDtypeStruct((2, S5_N_TBL, SUBLANES, n_grp, n_st), F32)),
        name="s5_prep",
    )(a_re[layer], a_im[layer], ldt, b_re_t, b_im_t)
    eye = jnp.eye(gpb, dtype=F32)

    def blockdiag_in(bb):
        x = jnp.transpose(bb, (1, 0, 2)).reshape(n_blk, gpb, hc, n_st)
        x = x[:, :, :, None, :] * eye[None, :, None, :, None]
        return x.reshape(n_blk, gpb * hc, nsl)

    def blockdiag_out(c):
        x = jnp.transpose(c.reshape(n_blk, gpb, hc, n_st), (0, 1, 3, 2))
        x = x[:, :, :, None, :] * eye[None, :, None, :, None]
        return x.reshape(n_blk, nsl, gpb * hc)

    bmat_re, bmat_im = blockdiag_in(bb_re), blockdiag_in(bb_im)
    cmat_re, cmat_im = blockdiag_out(c_re[layer]), blockdiag_out(c_im[layer])
    bmat_re, bmat_im, cmat_re, cmat_im = (x.astype(BF16) for x in (bmat_re, bmat_im, cmat_re, cmat_im))
    pw_b = pw.reshape(2, S5_N_TBL, SUBLANES, n_blk, nsl).transpose(3, 0, 1, 2, 4)
    d3, d_spec = d_skip.reshape(d_skip.shape[0], 1, -1), pl.BlockSpec((None, 1, LANES), lambda b, g: (layer, 0, g))
    return pl.pallas_call(
        functools.partial(_s5_scan_kernel, seq=seq, lane_chunk=2 * LANES),
        grid=(bsz, n_blk),
        in_specs=[
            pl.BlockSpec((None, seq, LANES), lambda b, g: (b, 0, g)),
            pl.BlockSpec((None, LANES, nsl), lambda b, g: (g, 0, 0)),
            pl.BlockSpec((None, LANES, nsl), lambda b, g: (g, 0, 0)),
            pl.BlockSpec((None, nsl, LANES), lambda b, g: (g, 0, 0)),
            pl.BlockSpec((None, nsl, LANES), lambda b, g: (g, 0, 0)),
            pl.BlockSpec((None, 2, S5_N_TBL, SUBLANES, nsl), lambda b, g: (g, 0, 0, 0, 0)),
            d_spec,
        ],
        out_specs=pl.BlockSpec((None, seq, LANES), lambda b, g: (b, 0, g)),
        out_shape=jax.ShapeDtypeStruct((bsz, seq, n_grp * hc), F32),
        scratch_shapes=[pltpu.VMEM((seq, LANES), F32), pltpu.VMEM((seq, nsl), F32), pltpu.VMEM((seq, nsl), F32),
                        pltpu.VMEM((seq, LANES), F32)],
        compiler_params=_cparams(("parallel", "parallel")),
        name="s5_scan",
    )(proj3, bmat_re, bmat_im, cmat_re, cmat_im, pw_b, d3)


def _attn_kernel(slopes_ref, qg_ref, kg_ref, *refs, seq):
    qkv = refs[:9]
    out_ref = refs[9]
    qm, km, vm, base, o_seq, l_seq, o_all, l_all = refs[10:]
    pair = pl.program_id(1)
    hd = ATTN_HEAD_DIM
    blk = ATTN_BLOCK
    lane = lax.broadcasted_iota(jnp.int32, (1, LANES), 1)
    head_mask = [(lane < hd).astype(F32), (lane >= hd).astype(F32)]
    qg = qg_ref[...]
    kg = kg_ref[...]

    def qk_norm(x, gain):
        x2 = x * x
        ms = [jnp.sum(x2 * head_mask[h], axis=-1, keepdims=True) * (1.0 / hd) for h in range(2)]
        inv = head_mask[0] * lax.rsqrt(ms[0] + RMS_EPS) + head_mask[1] * lax.rsqrt(ms[1] + RMS_EPS)
        return x * inv * gain

    for g, (window, dil) in enumerate(DIL_PATTERNS):
        q_ref, k_ref, v_ref = qkv[3 * g:3 * g + 3]
        n = seq // dil
        n_blocks = n // blk
        n_back = window // dil
        assert n_back == blk and n % blk == 0
        rows = lax.broadcasted_iota(jnp.int32, (blk, 2 * blk), 0)
        cols = lax.broadcasted_iota(jnp.int32, (blk, 2 * blk), 1)
        delta = rows + blk - cols
        valid = (delta >= 0) & (delta <= n_back)
        for h in range(2):
            slope = slopes_ref[g * ATTN_HEADS + 2 * pair + h]
            base[h] = jnp.where(valid, (-slope * dil) * delta.astype(F32), NEG_BIG)

        def load_class(r, off):
            if dil == 1:
                q, k, v = q_ref[...], k_ref[...], v_ref[...]
            else:
                q = q_ref[pl.ds(r, n, stride=dil), :]
                k = k_ref[pl.ds(r, n, stride=dil), :]
                v = v_ref[pl.ds(r, n, stride=dil), :]
            qn = qk_norm(q, qg)
            km[pl.ds(off, n), :] = qk_norm(k, kg).astype(BF16)
            for h in range(2):
                qm[h, pl.ds(off, n), :] = (qn * head_mask[h]).astype(BF16)
                vm[h, pl.ds(off, n), :] = (v * head_mask[h]).astype(BF16)

        def one_block(off, t0, first, o_dst, l_dst):
            o_acc = jnp.zeros((blk, LANES), F32)
            l_acc = jnp.zeros((blk, LANES), F32)
            for h in range(2):
                qb = qm[h, pl.ds(off + t0, blk), :]
                if first:
                    kb = km[pl.ds(off + t0, blk), :]
                    vb = vm[h, pl.ds(off + t0, blk), :]
                    bias = base[h, :, blk:]
                else:
                    kb = km[pl.ds(off + t0 - blk, 2 * blk), :]
                    vb = vm[h, pl.ds(off + t0 - blk, 2 * blk), :]
                    bias = base[h]
                s = _dot_nt(qb, kb) * (hd ** -0.5) + bias
                m = jnp.max(s, axis=-1, keepdims=True)
                p = jnp.exp(s - m)
                l = jnp.sum(p, axis=-1, keepdims=True)
                pv = _dot(p.astype(BF16), vb)
                o_acc = o_acc + pv * (1.0 / l)
                l_acc = l_acc + head_mask[h] * (m + jnp.log(l))
            o_dst[pl.ds(off + t0, blk), :] = o_acc
            l_dst[pl.ds(off + t0, blk), :] = l_acc

        def store_class(r, off):
            o_all[g, pl.ds(r, n, stride=dil), :] = o_seq[pl.ds(off, n), :]
            l_all[g, pl.ds(r, n, stride=dil), :] = l_seq[pl.ds(off, n), :]

        if dil == 1:
            o_g, l_g = o_all.at[g], l_all.at[g]
            load_class(0, 0)
            one_block(0, 0, True, o_g, l_g)
            per_iter = 3
            assert (n_blocks - 1) % per_iter == 0

            def blocks_body(i, _):
                for j in range(per_iter):
                    one_block(0, pl.multiple_of((1 + i * per_iter + j) * blk, blk), False, o_g, l_g)
                return 0
            lax.fori_loop(0, (n_blocks - 1) // per_iter, blocks_body, 0)
        else:
            per_iter = max(1, 4 // n_blocks)
            assert dil % per_iter == 0 and per_iter * n <= seq

            def classes_body(i, _):
                for j in range(per_iter):
                    load_class(i * per_iter + j, j * n)
                for j in range(per_iter):
                    for b in range(n_blocks):
                        one_block(j * n, b * blk, b == 0, o_seq, l_seq)
                for j in range(per_iter):
                    store_class(i * per_iter + j, j * n)
                return 0
            lax.fori_loop(0, dil // per_iter, classes_body, 0)

    l0, l1, l2 = l_all[0], l_all[1], l_all[2]
    mx = jnp.maximum(jnp.maximum(l0, l1), l2)
    w0, w1, w2 = jnp.exp(l0 - mx), jnp.exp(l1 - mx), jnp.exp(l2 - mx)
    o = (w0 * o_all[0] + w1 * o_all[1] + w2 * o_all[2]) / (w0 + w1 + w2)
    out_ref[...] = o.astype(out_ref.dtype)


def _attention(proj3, q_off, layer, q_gain, k_gain):
    bsz, seq, _ = proj3.shape
    n_dil = len(DIL_PATTERNS)
    width = n_dil * ATTN_HEADS * ATTN_HEAD_DIM
    n_pairs = ATTN_HEADS * ATTN_HEAD_DIM // LANES
    n_heads = n_dil * ATTN_HEADS
    slopes = jnp.asarray(2.0 ** (-ALIBI_MAX_BIAS * (np.arange(n_heads) + 1) / n_heads), dtype=F32)
    qg = jnp.tile(q_gain[layer], 2)[None]
    kg = jnp.tile(k_gain[layer], 2)[None]
    in_specs = [pl.BlockSpec(memory_space=pltpu.SMEM),
                pl.BlockSpec((1, LANES), lambda b, p: (0, 0)), pl.BlockSpec((1, LANES), lambda b, p: (0, 0))]
    operands = [slopes, qg, kg]
    for g in range(n_dil):
        for which in range(3):
            cb = (q_off + which * width + g * ATTN_HEADS * ATTN_HEAD_DIM) // LANES
            in_specs.append(pl.BlockSpec((None, seq, LANES), lambda b, p, cb=cb: (b, 0, cb + p)))
            operands.append(proj3)
    return pl.pallas_call(
        functools.partial(_attn_kernel, seq=seq),
        grid=(bsz, n_pairs),
        in_specs=in_specs,
        out_specs=pl.BlockSpec((None, seq, LANES), lambda b, p: (b, 0, p)),
        out_shape=jax.ShapeDtypeStruct((bsz, seq, ATTN_HEADS * ATTN_HEAD_DIM), BF16),
        scratch_shapes=[
            pltpu.VMEM((2, seq, LANES), BF16), pltpu.VMEM((seq, LANES), BF16), pltpu.VMEM((2, seq, LANES), BF16),
            pltpu.VMEM((2, ATTN_BLOCK, 2 * ATTN_BLOCK), F32),
            pltpu.VMEM((seq, LANES), F32), pltpu.VMEM((seq, LANES), F32),
            pltpu.VMEM((n_dil, seq, LANES), F32), pltpu.VMEM((n_dil, seq, LANES), F32),
        ],
        compiler_params=_cparams(("parallel", "parallel")),
        name="dilated_attn",
    )(*operands)


def _split_bf16(x):
    hi = x.astype(BF16)
    return hi, (x - hi.astype(F32)).astype(BF16)


def _dot_x3(a, b):
    ah, al = _split_bf16(a)
    bh, bl = _split_bf16(b)
    return _dot(ah, bh) + (_dot(ah, bl) + _dot(al, bh))


def _head_sums(x, ones_bd):
    parts = []
    for c in range(0, x.shape[1], LANES):
        hi, lo = _split_bf16(x[:, c:c + LANES])
        parts.append(_dot(hi, ones_bd) + _dot(lo, ones_bd))
    return jnp.concatenate(parts, axis=1)


def _rwkv_prep_kernel(*refs, has_vres, tiles_per_seq, width, half):
    it = iter(refs)
    cur = [next(it) for _ in range(7)]
    prev = [next(it) for _ in range(7)]
    mu_ref, ones_ref, w0_ref, w2_ref, a0_ref, a2_ref, g2_ref, kk_ref, ka_ref = [next(it) for _ in range(9)]
    if has_vres:
        vfirst_ref, v0_ref, v1_ref, v2_ref = [next(it) for _ in range(4)]
    r_out, lw_out, k_out, v_out, a_out, b_out, g_out = [next(it) for _ in range(7)]

    tm = cur[0].shape[0]
    first_of_seq = (pl.program_id(0) % tiles_per_seq) == 0
    row = lax.broadcasted_iota(jnp.int32, (tm, half), 0)

    def shifted(c):
        x = cur[c][...]
        last_prev = jnp.where(first_of_seq, 0.0, prev[c][SUBLANES - 1:SUBLANES, :])
        x_prev = jnp.where(row == 0, last_prev, pltpu.roll(x, 1, 0))
        mu = mu_ref[:, c * half:(c + 1) * half]
        return x + (x_prev - x) * mu

    sh = [shifted(c) for c in range(7)]
    r = jnp.concatenate(sh[0:2], axis=1)
    k = jnp.concatenate(sh[2:4], axis=1)
    v = jnp.concatenate(sh[4:6], axis=1)
    lora = sh[6]
    w_log = -jax.nn.softplus(-(w0_ref[...] + _dot(jnp.tanh(lora).astype(BF16), w2_ref[...]))) - 0.5
    a = _sigmoid(a0_ref[...] + _dot(lora.astype(BF16), a2_ref[...]))
    g = _dot(_sigmoid(lora).astype(BF16), g2_ref[...])
    if has_vres:
        low = _dot(v.astype(BF16), v1_ref[...])
        mix = _sigmoid(v0_ref[...] + _dot(low.astype(BF16), v2_ref[...]))
        v = v + (vfirst_ref[...] - v) * mix
    ones_bd = ones_ref[...]
    kk = k * kk_ref[...]
    kk = kk * lax.rsqrt(jnp.maximum(_head_sums(kk * kk, ones_bd), 1e-24))
    k = k * (1.0 + (a - 1.0) * ka_ref[...])
    r_out[...] = r
    lw_out[...] = -jnp.exp(w_log)
    k_out[...] = k
    v_out[...] = v
    a_out[...] = -kk
    b_out[...] = kk * a
    g_out[...] = g


def _wkv_kernel(r_ref, lw_ref, k_ref, v_ref, a_ref, b_ref, g_ref, lnw_ref, lnb_ref, rk_ref, out_ref,
                qp_s, yi_s, m_s, n_s, st_s, pw_s, tinv_s, ak_s, rb_s, rk_s, ast_s, vst_s, be_s, ke_s, cend_s,
                *, seq, group):
    L = WKV_CHUNK
    N = RWKV_HEAD_DIM
    P = 2 * N
    assert P == LANES and L == N
    n_chunks = seq // L
    ti = lax.broadcasted_iota(jnp.int32, (L, L), 0)
    si = lax.broadcasted_iota(jnp.int32, (L, L), 1)
    tri_incl = jnp.clip((ti - si).astype(F32) + 1.0, 0.0, 1.0).astype(BF16)
    ri = lax.broadcasted_iota(jnp.int32, (P, P), 0)
    ci = lax.broadcasted_iota(jnp.int32, (P, P), 1)
    same_head = (ri // L) == (ci // N)
    strict = same_head & ((ci % L) < (ri % L))
    incl = same_head & ((ci % L) <= (ri % L))
    eye = ri == ci
    lane = lax.broadcasted_iota(jnp.int32, (1, P), 1)
    head_mask = [(lane < N).astype(F32), (lane >= N).astype(F32)]

    def stack(x):
        return jnp.concatenate([x * head_mask[0], x * head_mask[1]], axis=0)

    def dup(x):
        return jnp.concatenate([x, x], axis=0)

    def fold(x):
        return x[:L] + x[L:]

    def for_chunk_groups(fn, per_iter):
        def body(i, _):
            for j in range(per_iter):
                fn(i * per_iter + j)
            return 0
        lax.fori_loop(0, n_chunks // per_iter, body, 0)

    def chunk_rows(c):
        return pl.ds(pl.multiple_of(c * L, L), L)

    def stage_scores(c):
        rows = chunk_rows(c)
        lw, r, k, v, a, b = (ref[rows, :] for ref in (lw_ref, r_ref, k_ref, v_ref, a_ref, b_ref))
        lw_hi, lw_lo = _split_bf16(lw)
        cum = _dot(tri_incl, lw_hi) + _dot(tri_incl, lw_lo)
        cum_end = cum[L - 1:L, :]
        g_inv = jnp.exp(-cum)
        g_end = jnp.exp(cum_end - cum)
        r_t = r * jnp.exp(cum)
        a_st = stack(a * jnp.exp(cum - lw)).astype(BF16)
        r_st = stack(r_t).astype(BF16)
        bk_du = jnp.concatenate([dup(b * g_inv), dup(k * g_inv)], axis=0).astype(BF16)
        ast_s[c] = a_st
        vst_s[c] = stack(v).astype(BF16)
        be_s[c] = stack(b * g_end).astype(BF16)
        ke_s[c] = stack(k * g_end).astype(BF16)
        qp_s[rows, :] = r_t
        cend_s[c] = jnp.broadcast_to(cum_end, (SUBLANES, P))
        a_bk = _dot_nt(a_st, bk_du)
        r_bk = _dot_nt(r_st, bk_du)
        a_ab = jnp.where(strict, a_bk[:, :P], 0.0)
        pw_s[c] = a_ab.astype(BF16)
        tinv_s[c] = jnp.where(eye, 1.0, a_ab)
        ak_s[c] = jnp.where(strict, a_bk[:, P:], 0.0).astype(BF16)
        rb_s[c] = jnp.where(incl, r_bk[:, :P], 0.0).astype(BF16)
        rk_s[c] = jnp.where(incl, r_bk[:, P:], 0.0).astype(BF16)

    def stage_inverse_level(c):
        pw = pw_s[c]
        pw = _dot(pw, pw).astype(BF16)
        pw_s[c] = pw
        tinv = tinv_s[c]
        tinv_s[c] = tinv + _dot(tinv.astype(BF16), pw)

    def stage_summary(c):
        rows = chunk_rows(c)
        tinv = tinv_s[c].astype(BF16)
        v_st = vst_s[c]
        be_st = be_s[c]
        a_rb = rb_s[c]
        av = _dot(ak_s[c], v_st).astype(BF16)
        wu = _dot(tinv, jnp.concatenate([ast_s[c], av], axis=1)).astype(BF16)
        rb_wu = _dot(a_rb, wu)
        be_wu = _dot_tn(be_st, wu)
        qp_s[rows, :] = qp_s[rows, :] + fold(rb_wu[:, :P])
        yi_s[rows, :] = fold(rb_wu[:, P:] + _dot(rk_s[c], v_st))
        d_end = jnp.where(eye, jnp.broadcast_to(jnp.exp(cend_s[c][0:1, :]), (P, P)), 0.0)
        m_s[c] = d_end + be_wu[:, :P]
        n_s[c] = be_wu[:, P:] + _dot_tn(ke_s[c], v_st)

    for_chunk_groups(stage_scores, group)

    def inverse_levels(_, carry):
        for_chunk_groups(stage_inverse_level, group)
        return carry

    n_levels = (L - 1).bit_length() - 1
    lax.fori_loop(0, n_levels, inverse_levels, 0)
    for_chunk_groups(stage_summary, group)

    def pass2(c, _):
        st_s[c + 1] = _dot_x3(m_s[c], st_s[c]) + n_s[c]
        return 0

    st_s[0] = jnp.zeros((P, P), F32)
    lax.fori_loop(0, n_chunks - 1, pass2, 0)

    def head_stat(x):
        s0 = jnp.sum(x * head_mask[0], axis=-1, keepdims=True)
        s1 = jnp.sum(x * head_mask[1], axis=-1, keepdims=True)
        return s0 * head_mask[0] + s1 * head_mask[1]

    def emit(c):
        rows = chunk_rows(c)
        y = _dot(qp_s[rows, :].astype(BF16), st_s[c].astype(BF16)) + yi_s[rows, :]
        yc = y - head_stat(y) * (1.0 / N)
        var = head_stat(yc * yc) * (1.0 / N)
        y = yc * lax.rsqrt(var + RWKV_GN_EPS) * lnw_ref[...] + lnb_ref[...]
        bonus = head_stat(r_ref[rows, :] * k_ref[rows, :] * rk_ref[...])
        out_ref[rows, :] = ((y + bonus * v_ref[rows, :]) * g_ref[rows, :]).astype(out_ref.dtype)

    for_chunk_groups(emit, min(group, 4))


def _rwkv_mixer(proj, p_off, bsz, seq, layer, v_first, shift_mu, w0, w2, a0, a2, g2, k_k, k_a, r_k, ln_w, ln_b,
                v0, v1, v2, tm):
    m = proj.shape[0]
    width = w0.shape[1]
    half = width // 2
    n_lora = w2.shape[1] + a2.shape[1] + g2.shape[1]
    assert n_lora <= half and p_off % half == 0 and 3 * width % half == 0
    has_vres = layer > 0
    cb0 = p_off // half
    rows8 = tm // SUBLANES
    operands, in_specs = [], []
    for c in range(7):
        operands.append(proj)
        in_specs.append(pl.BlockSpec((tm, half), lambda i, c=c: (i, cb0 + c)))
    for c in range(7):
        operands.append(proj)
        in_specs.append(pl.BlockSpec((SUBLANES, half), lambda i, c=c: (jnp.maximum(i * rows8 - 1, 0), cb0 + c)))

    def full2(x):
        operands.append(x)
        in_specs.append(pl.BlockSpec(x.shape, lambda i: (0, 0)))

    def lrow(x):
        a3, spec = _layer_row(x, layer)
        operands.append(a3)
        in_specs.append(spec)

    mu = jnp.pad(shift_mu[layer], (0, 7 * half - shift_mu.shape[1]))[None]
    full2(mu)
    head_of_lane = np.arange(LANES) // RWKV_HEAD_DIM
    full2(jnp.asarray(head_of_lane[:, None] == head_of_lane[None, :], dtype=F32))
    d_lo, a_lo = w2.shape[1], a2.shape[1]
    w2p = jnp.pad(w2[layer], ((0, half - d_lo), (0, 0)))
    a2p = jnp.pad(a2[layer], ((d_lo, half - d_lo - a_lo), (0, 0)))
    g2p = jnp.pad(g2[layer], ((d_lo + a_lo, half - n_lora), (0, 0)))
    lrow(w0)
    full2(w2p.astype(BF16))
    lrow(a0)
    full2(a2p.astype(BF16))
    full2(g2p.astype(BF16))
    lrow(k_k)
    lrow(k_a)
    if has_vres:
        operands.append(v_first)
        in_specs.append(pl.BlockSpec((tm, width), lambda i: (i, 0)))
        lrow_layer = layer - 1
        for x in (v0,):
            a3 = x.reshape(x.shape[0], 1, width)
            operands.append(a3)
            in_specs.append(pl.BlockSpec((None, 1, width), lambda i: (lrow_layer, 0, 0)))
        full2(v1[layer - 1].astype(BF16))
        full2(v2[layer - 1].astype(BF16))
    tile = pl.BlockSpec((tm, width), lambda i: (i, 0))
    shp = jax.ShapeDtypeStruct((m, width), F32)
    r, lw, k, v, a, b, g = pl.pallas_call(
        functools.partial(_rwkv_prep_kernel, has_vres=has_vres, tiles_per_seq=seq // tm, width=width, half=half),
        grid=(m // tm,),
        in_specs=in_specs,
        out_specs=[tile] * 7,
        out_shape=[shp] * 7,
        compiler_params=_cparams(("parallel",)),
        name="rwkv_prep",
    )(*operands)

    n_pairs = width // LANES
    seq_spec = pl.BlockSpec((None, seq, LANES), lambda bi, p: (bi, 0, p))
    row_spec = pl.BlockSpec((None, 1, LANES), lambda bi, p: (layer, 0, p))
    n_chunks = seq // WKV_CHUNK
    N = RWKV_HEAD_DIM
    y = pl.pallas_call(
        functools.partial(_wkv_kernel, seq=seq, group=min(8, n_chunks)),
        grid=(bsz, n_pairs),
        in_specs=[seq_spec] * 7 + [row_spec] * 3,
        out_specs=seq_spec,
        out_shape=jax.ShapeDtypeStruct((bsz, seq, width), BF16),
        scratch_shapes=(
            [pltpu.VMEM((seq, LANES), F32)] * 2
            + [pltpu.VMEM((n_chunks, LANES, LANES), F32)] * 3
            + [pltpu.VMEM((n_chunks, LANES, LANES), BF16)]
            + [pltpu.VMEM((n_chunks, LANES, LANES), F32)]
            + [pltpu.VMEM((n_chunks, LANES, LANES), BF16)] * 7
            + [pltpu.VMEM((n_chunks, SUBLANES, LANES), F32)]
        ),
        compiler_params=_cparams(("parallel", "parallel")),
        name="wkv7",
    )(*[x.reshape(bsz, seq, width) for x in (r, lw, k, v, a, b, g)],
      ln_w.reshape(-1, 1, width), ln_b.reshape(-1, 1, width), r_k.reshape(-1, 1, width))
    return y, (v_first if has_vres else v)


def _row_tiles(m):
    plan = dict(ffn_up=8192, in_proj=4096, ffn_down=2048, narrow=2048, rwkv_prep=256)
    return {k: min(v, m) for k, v in plan.items()}


def _trunk(x, ffn1_norm, ffn1_up, ffn1_down, mix_norm, w_in, s5_a_re, s5_a_im, s5_b_re, s5_b_im, s5_c_re, s5_c_im,
           s5_log_dt, s5_d, s5_w_glu, attn_q_gain, attn_k_gain, rwkv_shift_mu, rwkv_w0, rwkv_w2, rwkv_a0, rwkv_a2,
           rwkv_g2, rwkv_k_k, rwkv_k_a, rwkv_r_k, rwkv_ln_w, rwkv_ln_b, rwkv_v0, rwkv_v1, rwkv_v2,
           w_branch_s5, w_branch_attn, w_branch_rwkv, w_out, ffn2_norm, ffn2_up, ffn2_down):
    bsz, seq, d_model = x.shape
    depth = w_in.shape[0]
    m = bsz * seq
    s5_width = s5_d.shape[1]
    attn_width = len(DIL_PATTERNS) * ATTN_HEADS * ATTN_HEAD_DIM
    rwkv_width = rwkv_w0.shape[1]
    rwkv_part = rwkv_shift_mu.shape[1]
    q_off = s5_width
    p_off = s5_width + 3 * attn_width
    gate_off = p_off + rwkv_part
    tn_in = 512
    tiles = _row_tiles(m)
    n_front = -(-gate_off // tn_in) * tn_in
    rk3 = rwkv_r_k.reshape(depth, rwkv_width)

    xf = x.reshape(m, d_model)
    v_first = None
    for l in range(depth):
        xf = _swiglu_ffn(xf, ffn1_norm, ffn1_up, ffn1_down, l, tiles["ffn_up"], tiles["ffn_down"])
        h = _rmsnorm(xf, mix_norm, l)
        proj = _mm("in_proj", [h], [(w_in, l, 0, 0)], [], _ep_plain, n_front, F32, tiles["in_proj"], tn_in)
        gates = _mm("in_proj_gates", [h], [(w_in[l][:, gate_off:], None, 0, 0)], [], _ep_plain,
                    3 * d_model, F32, tiles["in_proj"], tn_in)
        proj3 = proj.reshape(bsz, seq, n_front)

        z = _s5_mixer_pre_glu(proj3, l, s5_a_re, s5_a_im, s5_b_re, s5_b_im, s5_c_re, s5_c_im, s5_log_dt, s5_d)
        z = z.reshape(m, s5_width)
        y_s5 = _mm("s5_glu", [z], [(s5_w_glu, l, 0, 0)], [(z, 0)], _ep_glu, s5_width, BF16, tiles["narrow"], 256)

        y_attn = _attention(proj3, q_off, l, attn_q_gain, attn_k_gain).reshape(m, -1)

        y_rwkv, v_first = _rwkv_mixer(proj, p_off, bsz, seq, l, v_first, rwkv_shift_mu, rwkv_w0, rwkv_w2, rwkv_a0,
                                      rwkv_a2, rwkv_g2, rwkv_k_k, rwkv_k_a, rk3, rwkv_ln_w, rwkv_ln_b,
                                      rwkv_v0, rwkv_v1, rwkv_v2, tiles["rwkv_prep"])
        y_rwkv = y_rwkv.reshape(m, rwkv_width)

        tn = 256
        gb = d_model // tn
        merged = _mm("branch_merge", [y_s5, y_attn, y_rwkv],
                     [(w_branch_s5, l, 0, 0), (w_branch_attn, l, 0, 1), (w_branch_rwkv, l, 0, 2)],
                     [(gates, 0), (gates, gb), (gates, 2 * gb)], _ep_gated_sum, d_model, BF16, tiles["narrow"], tn)
        xf = _mm("mix_out", [merged], [(w_out, l, 0, 0)], [(xf, 0)], _ep_residual, d_model, F32, tiles["narrow"], tn)
        xf = _swiglu_ffn(xf, ffn2_norm, ffn2_up, ffn2_down, l, tiles["ffn_up"], tiles["ffn_down"])
    return xf.reshape(bsz, seq, d_model)


def kernel(x, ffn1_norm, ffn1_up, ffn1_down, mix_norm, w_in, s5_a_re, s5_a_im, s5_b_re, s5_b_im, s5_c_re, s5_c_im, s5_log_dt, s5_d, s5_w_glu, attn_q_gain, attn_k_gain, rwkv_shift_mu, rwkv_w0, rwkv_w2, rwkv_a0, rwkv_a2, rwkv_g2, rwkv_k_k, rwkv_k_a, rwkv_r_k, rwkv_ln_w, rwkv_ln_b, rwkv_v0, rwkv_v1, rwkv_v2, w_branch_s5, w_branch_attn, w_branch_rwkv, w_out, ffn2_norm, ffn2_up, ffn2_down):
    return _trunk(x, ffn1_norm, ffn1_up, ffn1_down, mix_norm, w_in, s5_a_re, s5_a_im, s5_b_re, s5_b_im, s5_c_re,
                  s5_c_im, s5_log_dt, s5_d, s5_w_glu, attn_q_gain, attn_k_gain, rwkv_shift_mu, rwkv_w0, rwkv_w2,
                  rwkv_a0, rwkv_a2, rwkv_g2, rwkv_k_k, rwkv_k_a, rwkv_r_k, rwkv_ln_w, rwkv_ln_b, rwkv_v0, rwkv_v1,
                  rwkv_v2, w_branch_s5, w_branch_attn, w_branch_rwkv, w_out, ffn2_norm, ffn2_up, ffn2_down)
```

```python
import functools
import math

import numpy as np
import jax
import jax.numpy as jnp
from jax import lax
from jax.experimental import pallas as pl
from jax.experimental.pallas import tpu as pltpu

F32 = jnp.float32
BF16 = jnp.bfloat16
HIGHEST = lax.Precision.HIGHEST

LANES = 128
SUBLANES = 8
VMEM_LIMIT_BYTES = 56 * 1024 * 1024

RMS_EPS = 1e-6
S5_GROUP = 16
S5_STATE = 64
S5_TBL_LEVEL = SUBLANES
S5_TBL_SEG = SUBLANES + 3
S5_N_TBL = SUBLANES + 4
ATTN_HEAD_DIM = 64
ATTN_HEADS = 8
DIL_PATTERNS = ((128, 1), (512, 4), (2048, 16))
ALIBI_MAX_BIAS = 8.0
ATTN_BLOCK = 128
RWKV_HEAD_DIM = 64
RWKV_GN_EPS = 64e-5
WKV_CHUNK = 64
NEG_BIG = -1e30


def _cparams(sem):
    return pltpu.CompilerParams(dimension_semantics=sem, vmem_limit_bytes=VMEM_LIMIT_BYTES)


def _dot(a, b, precision=None):
    return jnp.dot(a, b, preferred_element_type=F32, precision=precision)


def _dot_nt(a, b, precision=None):
    return lax.dot_general(a, b, (((1,), (1,)), ((), ())), preferred_element_type=F32, precision=precision)


def _dot_tn(a, b, precision=None):
    return lax.dot_general(a, b, (((0,), (0,)), ((), ())), preferred_element_type=F32, precision=precision)


def _sigmoid(x):
    return 1.0 / (1.0 + jnp.exp(-x))


def _layer_row(arr, layer):
    n = arr.shape[-1]
    a3 = arr.reshape(arr.shape[0], 1, n)
    return a3, pl.BlockSpec((None, 1, n), lambda *_: (layer, 0, 0))


def _rmsnorm_kernel(x_ref, g_ref, o_ref):
    x = x_ref[...]
    y = x * lax.rsqrt(jnp.mean(x * x, axis=-1, keepdims=True) + RMS_EPS)
    o_ref[...] = (y * g_ref[...]).astype(o_ref.dtype)


def _rmsnorm(x, gain, layer):
    m, d = x.shape
    tm = min(m, 512)
    g3, g_spec = _layer_row(gain, layer)
    return pl.pallas_call(
        _rmsnorm_kernel,
        grid=(m // tm,),
        in_specs=[pl.BlockSpec((tm, d), lambda i: (i, 0)), g_spec],
        out_specs=pl.BlockSpec((tm, d), lambda i: (i, 0)),
        out_shape=jax.ShapeDtypeStruct((m, d), BF16),
        compiler_params=_cparams(("parallel",)),
        name="rmsnorm",
    )(x, g3)


def _mm_kernel(*refs, n_lhs, lhs_needs_cast, rhs_lhs, rhs_parts, n_extra, epilogue, tn, row_chunk):
    it = iter(refs)
    lhs_refs = [next(it) for _ in range(n_lhs)]
    rhs_refs = [[next(it) for _ in range(n_parts)] for n_parts, _ in rhs_parts]
    extra_refs = [next(it) for _ in range(n_extra)]
    out_ref = next(it)
    lhs_scr = {p: next(it) for p in range(n_lhs) if lhs_needs_cast[p]}
    w_scr = [next(it) for _ in range(n_lhs)]

    @pl.when(pl.program_id(1) == 0)
    def _():
        for p, scr in lhs_scr.items():
            scr[...] = lhs_refs[p][...].astype(BF16)

    for p in range(n_lhs):
        mine = [r for r, q in enumerate(rhs_lhs) if q == p]
        for n, r in enumerate(mine):
            transposed = rhs_parts[r][1]
            parts = [t[...] for t in rhs_refs[r]]
            if transposed:
                w = (parts[0] if len(parts) == 1 else jnp.concatenate(parts, axis=0)).T
            else:
                w = parts[0] if len(parts) == 1 else jnp.concatenate(parts, axis=1)
            w_scr[p][:, n * tn:(n + 1) * tn] = w.astype(BF16)

    def chunk(c):
        rows = pl.ds(c * row_chunk, row_chunk)
        accs = [None] * len(rhs_lhs)
        for p in range(n_lhs):
            mine = [r for r, q in enumerate(rhs_lhs) if q == p]
            a = (lhs_scr[p] if lhs_needs_cast[p] else lhs_refs[p])[rows, :]
            acc = _dot(a, w_scr[p][...])
            for n, r in enumerate(mine):
                accs[r] = acc[:, n * tn:(n + 1) * tn]
        out_ref[rows, :] = epilogue(accs, [e[rows, :] for e in extra_refs]).astype(out_ref.dtype)

    for c in range(out_ref.shape[0] // row_chunk):
        chunk(c)


def _mm(name, lhs, rhs, extras, epilogue, n_out, out_dtype, tm, tn, row_chunk=1024):
    m = lhs[0].shape[0]
    tm = min(tm, m)
    row_chunk = min(row_chunk, tm)
    operands, in_specs, lhs_scratch, w_scratch = [], [], [], []
    lhs_needs_cast = []
    for p, a in enumerate(lhs):
        k = a.shape[1]
        operands.append(a)
        in_specs.append(pl.BlockSpec((tm, k), lambda i, j: (i, 0), pipeline_mode=pl.Buffered(1)))
        lhs_needs_cast.append(a.dtype != BF16)
        if a.dtype != BF16:
            lhs_scratch.append(pltpu.VMEM((tm, k), BF16))
        w_scratch.append(pltpu.VMEM((k, tn * sum(1 for r in rhs if r[3] == p)), BF16))
    rhs_parts = []
    for w, layer, off, _ in rhs:
        if isinstance(off, tuple):
            start = off[1]
            part = math.gcd(start, tn)
            assert part % SUBLANES == 0
            n_parts = tn // part
            rhs_parts.append((n_parts, True))
            for t in range(n_parts):
                operands.append(w)
                in_specs.append(pl.BlockSpec(
                    (None, part, w.shape[2]),
                    lambda i, j, b=start // part + t, s=n_parts, layer=layer: (layer, b + s * j, 0)))
        else:
            rhs_parts.append((1, False))
            operands.append(w)
            if layer is None:
                in_specs.append(pl.BlockSpec((w.shape[0], tn), lambda i, j, off=off: (0, off + j)))
            else:
                in_specs.append(pl.BlockSpec((None, w.shape[1], tn),
                                             lambda i, j, off=off, layer=layer: (layer, 0, off + j)))
    for e, off in extras:
        operands.append(e)
        in_specs.append(pl.BlockSpec((tm, tn), lambda i, j, off=off: (i, off + j)))
    kern = functools.partial(
        _mm_kernel, n_lhs=len(lhs), lhs_needs_cast=tuple(lhs_needs_cast), rhs_lhs=tuple(r[3] for r in rhs),
        rhs_parts=tuple(rhs_parts), n_extra=len(extras), epilogue=epilogue, tn=tn, row_chunk=row_chunk)
    return pl.pallas_call(
        kern,
        grid=(m // tm, n_out // tn),
        in_specs=in_specs,
        out_specs=pl.BlockSpec((tm, tn), lambda i, j: (i, j)),
        out_shape=jax.ShapeDtypeStruct((m, n_out), out_dtype),
        scratch_shapes=lhs_scratch + w_scratch,
        compiler_params=_cparams(("parallel", "arbitrary")),
        name=name,
    )(*operands)


def _ep_swiglu(accs, extras):
    a, b = accs
    return a * _sigmoid(a) * b


def _ep_half_residual(accs, extras):
    return extras[0] + 0.5 * accs[0]


def _ep_residual(accs, extras):
    return extras[0] + accs[0]


def _ep_plain(accs, extras):
    return accs[0]


def _ep_glu(accs, extras):
    z = extras[0]
    return z * _sigmoid(accs[0])


def _ep_gated_sum(accs, extras):
    out = _sigmoid(extras[0]) * accs[0]
    for g, a in zip(extras[1:], accs[1:]):
        out = out + _sigmoid(g) * a
    return out


def _swiglu_ffn(x, norm, w_up, w_down, layer, tm_up, tm_down):
    d_ff = w_down.shape[1]
    tn = LANES
    h = _rmsnorm(x, norm, layer)
    act = _mm("ffn_up", [h], [(w_up, layer, 0, 0), (w_up, layer, d_ff // tn, 0)], [], _ep_swiglu,
              d_ff, BF16, tm_up, tn)
    return _mm("ffn_down", [act], [(w_down, layer, 0, 0)], [(x, 0)], _ep_half_residual,
               x.shape[1], F32, tm_down, 256, row_chunk=512)


def _cmul(ar, ai, br, bi):
    return ar * br - ai * bi, ar * bi + ai * br


def _s5_prep_kernel(are_ref, aim_ref, ldt_ref, bre_ref, bim_ref, bbre_ref, bbim_ref, pw_ref):
    lam_re = jnp.minimum(are_ref[...], -1e-4)
    lam_im = aim_ref[...]
    dt = jnp.exp(ldt_ref[...])
    mag = jnp.exp(lam_re * dt)
    ab_re = mag * jnp.cos(lam_im * dt)
    ab_im = mag * jnp.sin(lam_im * dt)
    den = lam_re * lam_re + lam_im * lam_im
    num_re = ab_re - 1.0
    coef_re = (num_re * lam_re + ab_im * lam_im) / den
    coef_im = (ab_im * lam_re - num_re * lam_im) / den
    for h in range(bre_ref.shape[0]):
        br = bre_ref[h]
        bi = bim_ref[h]
        bbre_ref[h] = coef_re * br - coef_im * bi
        bbim_ref[h] = coef_re * bi + coef_im * br
    powers = [(ab_re, ab_im)]
    for _ in range(SUBLANES - 1):
        powers.append(_cmul(*powers[-1], ab_re, ab_im))
    for t, (pr, pi) in enumerate(powers):
        for r in range(SUBLANES):
            pw_ref[0, t, r] = pr
            pw_ref[1, t, r] = pi
    a8 = powers[-1]
    zero = jnp.zeros_like(ab_re)
    lvl_pow = a8
    for lvl in range(3):
        for r in range(SUBLANES):
            live = r >= (1 << lvl)
            pw_ref[0, S5_TBL_LEVEL + lvl, r] = lvl_pow[0] if live else zero
            pw_ref[1, S5_TBL_LEVEL + lvl, r] = lvl_pow[1] if live else zero
        lvl_pow = _cmul(*lvl_pow, *lvl_pow)
    pr, pi = a8
    for r in range(SUBLANES):
        pw_ref[0, S5_TBL_SEG, r] = pr
        pw_ref[1, S5_TBL_SEG, r] = pi
        pr, pi = _cmul(pr, pi, *a8)


def _s5_scan_kernel(u_ref, bre_ref, bim_ref, cre_ref, cim_ref, pw_ref, d_ref, z_ref, up_s, sre, sim, yp_s,
                    *, seq, lane_chunk):
    S = SUBLANES
    sb_rows = S * S
    n_sb = seq // sb_rows
    n_state = sre.shape[1]

    def transpose_superblocks(read_tile, write_tile):
        def body(sb, _):
            base = pl.multiple_of(sb * sb_rows, sb_rows)
            for t in range(S):
                write_tile(base, t, read_tile(base, t))
            return 0
        lax.fori_loop(0, n_sb, body, 0)

    def put_u(base, t, x):
        up_s[pl.ds(base + t * S, S), :] = x

    transpose_superblocks(lambda base, t: u_ref[pl.ds(base + t, S, stride=S), :], put_u)
    up = up_s[...].astype(BF16)
    sre[...] = _dot(up, bre_ref[...])
    sim[...] = _dot(up, bim_ref[...])

    def scan_superblock(sb, carry):
        base = pl.multiple_of(sb * sb_rows, sb_rows)
        new_carry = []
        for ch in range(n_state // lane_chunk):
            ln = pl.ds(ch * lane_chunk, lane_chunk)
            c_re, c_im = carry[2 * ch], carry[2 * ch + 1]
            ar, ai = pw_ref[0, 0, :, ln], pw_ref[1, 0, :, ln]
            sr, si = sre[pl.ds(base, S), ln], sim[pl.ds(base, S), ln]
            local = [(sr, si)]
            for t in range(1, S):
                rows = pl.ds(base + t * S, S)
                sr, si = sre[rows, ln] + ar * sr - ai * si, sim[rows, ln] + ar * si + ai * sr
                local.append((sr, si))
            er, ei = local[-1]
            for lvl in range(3):
                hr, hi = pw_ref[0, S5_TBL_LEVEL + lvl, :, ln], pw_ref[1, S5_TBL_LEVEL + lvl, :, ln]
                yr, yi = pltpu.roll(er, 1 << lvl, 0), pltpu.roll(ei, 1 << lvl, 0)
                er, ei = er + hr * yr - hi * yi, ei + hr * yi + hi * yr
            qr, qi = pw_ref[0, S5_TBL_SEG, :, ln], pw_ref[1, S5_TBL_SEG, :, ln]
            fr, fi = er + qr * c_re - qi * c_im, ei + qr * c_im + qi * c_re
            first = lax.broadcasted_iota(jnp.int32, fr.shape, 0) == 0
            in_r = jnp.where(first, c_re, pltpu.roll(fr, 1, 0))
            in_i = jnp.where(first, c_im, pltpu.roll(fi, 1, 0))
            for t in range(S):
                rows = pl.ds(base + t * S, S)
                pr, pi = pw_ref[0, t, :, ln], pw_ref[1, t, :, ln]
                sr, si = local[t]
                sre[rows, ln] = sr + pr * in_r - pi * in_i
                sim[rows, ln] = si + pr * in_i + pi * in_r
            new_carry.append(jnp.broadcast_to(fr[S - 1:S, :], fr.shape))
            new_carry.append(jnp.broadcast_to(fi[S - 1:S, :], fi.shape))
        return tuple(new_carry)

    zero = jnp.zeros((S, lane_chunk), F32)
    lax.fori_loop(0, n_sb, scan_superblock, (zero,) * (2 * (n_state // lane_chunk)))
    yp_s[...] = _dot(sre[...].astype(BF16), cre_ref[...]) - _dot(sim[...].astype(BF16), cim_ref[...])
    d = d_ref[...]

    def put_z(base, t, y):
        rows = pl.ds(base + t * S, S)
        z_ref[rows, :] = jax.nn.gelu(y + d * u_ref[rows, :])

    transpose_superblocks(lambda base, t: yp_s[pl.ds(base + t, S, stride=S), :], put_z)


def _s5_mixer_pre_glu(proj3, layer, a_re, a_im, b_re, b_im, c_re, c_im, log_dt, d_skip):
    bsz, seq, _ = proj3.shape
    n_grp, n_st = a_re.shape[1], a_re.shape[2]
    hc = b_re.shape[3]
    gpb = LANES // hc
    n_blk = n_grp // gpb
    nsl = gpb * n_st
    ldt = jnp.broadcast_to(log_dt[layer][:, None], (n_grp, n_st))
    b_re_t = jnp.transpose(b_re[layer], (2, 0, 1))
    b_im_t = jnp.transpose(b_im[layer], (2, 0, 1))
    bb_re, bb_im, pw = pl.pallas_call(
        _s5_prep_kernel,
        out_shape=(jax.ShapeDtypeStruct((hc, n_grp, n_st), F32), jax.ShapeDtypeStruct((hc, n_grp, n_st), F32),
                   jax.ShapeDtypeStruct((2, S5_N_TBL, SUBLANES, n_grp, n_st), F32)),
        name="s5_prep",
    )(a_re[layer], a_im[layer], ldt, b_re_t, b_im_t)
    eye = jnp.eye(gpb, dtype=F32)

    def blockdiag_in(bb):
        x = jnp.transpose(bb, (1, 0, 2)).reshape(n_blk, gpb, hc, n_st)
        x = x[:, :, :, None, :] * eye[None, :, None, :, None]
        return x.reshape(n_blk, gpb * hc, nsl)

    def blockdiag_out(c):
        x = jnp.transpose(c.reshape(n_blk, gpb, hc, n_st), (0, 1, 3, 2))
        x = x[:, :, :, None, :] * eye[None, :, None, :, None]
        return x.reshape(n_blk, nsl, gpb * hc)

    bmat_re, bmat_im = blockdiag_in(bb_re), blockdiag_in(bb_im)
    cmat_re, cmat_im = blockdiag_out(c_re[layer]), blockdiag_out(c_im[layer])
    bmat_re, bmat_im, cmat_re, cmat_im = (x.astype(BF16) for x in (bmat_re, bmat_im, cmat_re, cmat_im))
    pw_b = pw.reshape(2, S5_N_TBL, SUBLANES, n_blk, nsl).transpose(3, 0, 1, 2, 4)
    d3, d_spec = d_skip.reshape(d_skip.shape[0], 1, -1), pl.BlockSpec((None, 1, LANES), lambda b, g: (layer, 0, g))
    return pl.pallas_call(
        functools.partial(_s5_scan_kernel, seq=seq, lane_chunk=2 * LANES),
        grid=(bsz, n_blk),
        in_specs=[
            pl.BlockSpec((None, seq, LANES), lambda b, g: (b, 0, g)),
            pl.BlockSpec((None, LANES, nsl), lambda b, g: (g, 0, 0)),
            pl.BlockSpec((None, LANES, nsl), lambda b, g: (g, 0, 0)),
            pl.BlockSpec((None, nsl, LANES), lambda b, g: (g, 0, 0)),
            pl.BlockSpec((None, nsl, LANES), lambda b, g: (g, 0, 0)),
            pl.BlockSpec((None, 2, S5_N_TBL, SUBLANES, nsl), lambda b, g: (g, 0, 0, 0, 0)),
            d_spec,
        ],
        out_specs=pl.BlockSpec((None, seq, LANES), lambda b, g: (b, 0, g)),
        out_shape=jax.ShapeDtypeStruct((bsz, seq, n_grp * hc), F32),
        scratch_shapes=[pltpu.VMEM((seq, LANES), F32), pltpu.VMEM((seq, nsl), F32), pltpu.VMEM((seq, nsl), F32),
                        pltpu.VMEM((seq, LANES), F32)],
        compiler_params=_cparams(("parallel", "parallel")),
        name="s5_scan",
    )(proj3, bmat_re, bmat_im, cmat_re, cmat_im, pw_b, d3)


def _attn_kernel(slopes_ref, qg_ref, kg_ref, *refs, seq):
    qkv = refs[:9]
    out_ref = refs[9]
    qm, km, vm, base, o_seq, l_seq, o_all, l_all = refs[10:]
    pair = pl.program_id(1)
    hd = ATTN_HEAD_DIM
    blk = ATTN_BLOCK
    lane = lax.broadcasted_iota(jnp.int32, (1, LANES), 1)
    head_mask = [(lane < hd).astype(F32), (lane >= hd).astype(F32)]
    qg = qg_ref[...]
    kg = kg_ref[...]

    def qk_norm(x, gain):
        x2 = x * x
        ms = [jnp.sum(x2 * head_mask[h], axis=-1, keepdims=True) * (1.0 / hd) for h in range(2)]
        inv = head_mask[0] * lax.rsqrt(ms[0] + RMS_EPS) + head_mask[1] * lax.rsqrt(ms[1] + RMS_EPS)
        return x * inv * gain

    for g, (window, dil) in enumerate(DIL_PATTERNS):
        q_ref, k_ref, v_ref = qkv[3 * g:3 * g + 3]
        n = seq // dil
        n_blocks = n // blk
        n_back = window // dil
        assert n_back == blk and n % blk == 0
        rows = lax.broadcasted_iota(jnp.int32, (blk, 2 * blk), 0)
        cols = lax.broadcasted_iota(jnp.int32, (blk, 2 * blk), 1)
        delta = rows + blk - cols
        valid = (delta >= 0) & (delta <= n_back)
        for h in range(2):
            slope = slopes_ref[g * ATTN_HEADS + 2 * pair + h]
            base[h] = jnp.where(valid, (-slope * dil) * delta.astype(F32), NEG_BIG)

        def load_class(r, off):
            if dil == 1:
                q, k, v = q_ref[...], k_ref[...], v_ref[...]
            else:
                q = q_ref[pl.ds(r, n, stride=dil), :]
                k = k_ref[pl.ds(r, n, stride=dil), :]
                v = v_ref[pl.ds(r, n, stride=dil), :]
            qn = qk_norm(q, qg)
            km[pl.ds(off, n), :] = qk_norm(k, kg).astype(BF16)
            for h in range(2):
                qm[h, pl.ds(off, n), :] = (qn * head_mask[h]).astype(BF16)
                vm[h, pl.ds(off, n), :] = (v * head_mask[h]).astype(BF16)

        def one_block(off, t0, first, o_dst, l_dst):
            o_acc = jnp.zeros((blk, LANES), F32)
            l_acc = jnp.zeros((blk, LANES), F32)
            for h in range(2):
                qb = qm[h, pl.ds(off + t0, blk), :]
                if first:
                    kb = km[pl.ds(off + t0, blk), :]
                    vb = vm[h, pl.ds(off + t0, blk), :]
                    bias = base[h, :, blk:]
                else:
                    kb = km[pl.ds(off + t0 - blk, 2 * blk), :]
                    vb = vm[h, pl.ds(off + t0 - blk, 2 * blk), :]
                    bias = base[h]
                s = _dot_nt(qb, kb) * (hd ** -0.5) + bias
                m = jnp.max(s, axis=-1, keepdims=True)
                p = jnp.exp(s - m)
                l = jnp.sum(p, axis=-1, keepdims=True)
                pv = _dot(p.astype(BF16), vb)
                o_acc = o_acc + pv * (1.0 / l)
                l_acc = l_acc + head_mask[h] * (m + jnp.log(l))
            o_dst[pl.ds(off + t0, blk), :] = o_acc
            l_dst[pl.ds(off + t0, blk), :] = l_acc

        def store_class(r, off):
            o_all[g, pl.ds(r, n, stride=dil), :] = o_seq[pl.ds(off, n), :]
            l_all[g, pl.ds(r, n, stride=dil), :] = l_seq[pl.ds(off, n), :]

        if dil == 1:
            o_g, l_g = o_all.at[g], l_all.at[g]
            load_class(0, 0)
            one_block(0, 0, True, o_g, l_g)
            per_iter = 3
            assert (n_blocks - 1) % per_iter == 0

            def blocks_body(i, _):
                for j in range(per_iter):
                    one_block(0, pl.multiple_of((1 + i * per_iter + j) * blk, blk), False, o_g, l_g)
                return 0
            lax.fori_loop(0, (n_blocks - 1) // per_iter, blocks_body, 0)
        else:
            per_iter = max(1, 4 // n_blocks)
            assert dil % per_iter == 0 and per_iter * n <= seq

            def classes_body(i, _):
                for j in range(per_iter):
                    load_class(i * per_iter + j, j * n)
                for j in range(per_iter):
                    for b in range(n_blocks):
                        one_block(j * n, b * blk, b == 0, o_seq, l_seq)
                for j in range(per_iter):
                    store_class(i * per_iter + j, j * n)
                return 0
            lax.fori_loop(0, dil // per_iter, classes_body, 0)

    l0, l1, l2 = l_all[0], l_all[1], l_all[2]
    mx = jnp.maximum(jnp.maximum(l0, l1), l2)
    w0, w1, w2 = jnp.exp(l0 - mx), jnp.exp(l1 - mx), jnp.exp(l2 - mx)
    o = (w0 * o_all[0] + w1 * o_all[1] + w2 * o_all[2]) / (w0 + w1 + w2)
    out_ref[...] = o.astype(out_ref.dtype)


def _attention(proj3, q_off, layer, q_gain, k_gain):
    bsz, seq, _ = proj3.shape
    n_dil = len(DIL_PATTERNS)
    width = n_dil * ATTN_HEADS * ATTN_HEAD_DIM
    n_pairs = ATTN_HEADS * ATTN_HEAD_DIM // LANES
    n_heads = n_dil * ATTN_HEADS
    slopes = jnp.asarray(2.0 ** (-ALIBI_MAX_BIAS * (np.arange(n_heads) + 1) / n_heads), dtype=F32)
    qg = jnp.tile(q_gain[layer], 2)[None]
    kg = jnp.tile(k_gain[layer], 2)[None]
    in_specs = [pl.BlockSpec(memory_space=pltpu.SMEM),
                pl.BlockSpec((1, LANES), lambda b, p: (0, 0)), pl.BlockSpec((1, LANES), lambda b, p: (0, 0))]
    operands = [slopes, qg, kg]
    for g in range(n_dil):
        for which in range(3):
            cb = (q_off + which * width + g * ATTN_HEADS * ATTN_HEAD_DIM) // LANES
            in_specs.append(pl.BlockSpec((None, seq, LANES), lambda b, p, cb=cb: (b, 0, cb + p)))
            operands.append(proj3)
    return pl.pallas_call(
        functools.partial(_attn_kernel, seq=seq),
        grid=(bsz, n_pairs),
        in_specs=in_specs,
        out_specs=pl.BlockSpec((None, seq, LANES), lambda b, p: (b, 0, p)),
        out_shape=jax.ShapeDtypeStruct((bsz, seq, ATTN_HEADS * ATTN_HEAD_DIM), BF16),
        scratch_shapes=[
            pltpu.VMEM((2, seq, LANES), BF16), pltpu.VMEM((seq, LANES), BF16), pltpu.VMEM((2, seq, LANES), BF16),
            pltpu.VMEM((2, ATTN_BLOCK, 2 * ATTN_BLOCK), F32),
            pltpu.VMEM((seq, LANES), F32), pltpu.VMEM((seq, LANES), F32),
            pltpu.VMEM((n_dil, seq, LANES), F32), pltpu.VMEM((n_dil, seq, LANES), F32),
        ],
        compiler_params=_cparams(("parallel", "parallel")),
        name="dilated_attn",
    )(*operands)


def _split_bf16(x):
    hi = x.astype(BF16)
    return hi, (x - hi.astype(F32)).astype(BF16)


def _dot_x3(a, b):
    ah, al = _split_bf16(a)
    bh, bl = _split_bf16(b)
    return _dot(ah, bh) + (_dot(ah, bl) + _dot(al, bh))


def _head_sums(x, ones_bd):
    parts = []
    for c in range(0, x.shape[1], LANES):
        hi, lo = _split_bf16(x[:, c:c + LANES])
        parts.append(_dot(hi, ones_bd) + _dot(lo, ones_bd))
    return jnp.concatenate(parts, axis=1)


def _rwkv_prep_kernel(*refs, has_vres, tiles_per_seq, width, half):
    it = iter(refs)
    cur = [next(it) for _ in range(7)]
    prev = [next(it) for _ in range(7)]
    mu_ref, ones_ref, w0_ref, w2_ref, a0_ref, a2_ref, g2_ref, kk_ref, ka_ref = [next(it) for _ in range(9)]
    if has_vres:
        vfirst_ref, v0_ref, v1_ref, v2_ref = [next(it) for _ in range(4)]
    r_out, lw_out, k_out, v_out, a_out, b_out, g_out = [next(it) for _ in range(7)]

    tm = cur[0].shape[0]
    first_of_seq = (pl.program_id(0) % tiles_per_seq) == 0
    row = lax.broadcasted_iota(jnp.int32, (tm, half), 0)

    def shifted(c):
        x = cur[c][...]
        last_prev = jnp.where(first_of_seq, 0.0, prev[c][SUBLANES - 1:SUBLANES, :])
        x_prev = jnp.where(row == 0, last_prev, pltpu.roll(x, 1, 0))
        mu = mu_ref[:, c * half:(c + 1) * half]
        return x + (x_prev - x) * mu

    sh = [shifted(c) for c in range(7)]
    r = jnp.concatenate(sh[0:2], axis=1)
    k = jnp.concatenate(sh[2:4], axis=1)
    v = jnp.concatenate(sh[4:6], axis=1)
    lora = sh[6]
    w_log = -jax.nn.softplus(-(w0_ref[...] + _dot(jnp.tanh(lora).astype(BF16), w2_ref[...]))) - 0.5
    a = _sigmoid(a0_ref[...] + _dot(lora.astype(BF16), a2_ref[...]))
    g = _dot(_sigmoid(lora).astype(BF16), g2_ref[...])
    if has_vres:
        low = _dot(v.astype(BF16), v1_ref[...])
        mix = _sigmoid(v0_ref[...] + _dot(low.astype(BF16), v2_ref[...]))
        v = v + (vfirst_ref[...] - v) * mix
    ones_bd = ones_ref[...]
    kk = k * kk_ref[...]
    kk = kk * lax.rsqrt(jnp.maximum(_head_sums(kk * kk, ones_bd), 1e-24))
    k = k * (1.0 + (a - 1.0) * ka_ref[...])
    r_out[...] = r
    lw_out[...] = -jnp.exp(w_log)
    k_out[...] = k
    v_out[...] = v
    a_out[...] = -kk
    b_out[...] = kk * a
    g_out[...] = g


def _wkv_kernel(r_ref, lw_ref, k_ref, v_ref, a_ref, b_ref, g_ref, lnw_ref, lnb_ref, rk_ref, out_ref,
                qp_s, yi_s, m_s, n_s, st_s, pw_s, tinv_s, rb_s, ast_s, vst_s, be_s, ke_s, rst_s, av_s,
                akrk_s, bk_s, wu_s, cum_s, cend_s,
                *, seq, group):
    L = WKV_CHUNK
    N = RWKV_HEAD_DIM
    P = 2 * N
    assert P == LANES and L == N
    n_chunks = seq // L
    ti = lax.broadcasted_iota(jnp.int32, (L, L), 0)
    si = lax.broadcasted_iota(jnp.int32, (L, L), 1)
    tri_incl = jnp.clip((ti - si).astype(F32) + 1.0, 0.0, 1.0).astype(BF16)
    ri = lax.broadcasted_iota(jnp.int32, (P, P), 0)
    ci = lax.broadcasted_iota(jnp.int32, (P, P), 1)
    same_head = (ri // L) == (ci // N)
    strict = same_head & ((ci % L) < (ri % L))
    incl = same_head & ((ci % L) <= (ri % L))
    eye = ri == ci
    lane = lax.broadcasted_iota(jnp.int32, (1, P), 1)
    head_mask = [(lane < N).astype(F32), (lane >= N).astype(F32)]

    def stack(x):
        return jnp.concatenate([x * head_mask[0], x * head_mask[1]], axis=0)

    def dup(x):
        return jnp.concatenate([x, x], axis=0)

    def fold(x):
        return x[:L] + x[L:]

    def for_chunk_groups(fn, per_iter):
        def body(i, _):
            for j in range(per_iter):
                fn(i * per_iter + j)
            return 0
        lax.fori_loop(0, n_chunks // per_iter, body, 0)

    def chunk_rows(c):
        return pl.ds(c * L, L) if isinstance(c, int) else pl.ds(pl.multiple_of(c * L, L), L)

    def stage_cumdecay(c):
        lw_hi, lw_lo = _split_bf16(lw_ref[chunk_rows(c), :])
        cum_s[c] = _dot(tri_incl, lw_hi) + _dot(tri_incl, lw_lo)

    def stage_scale(c):
        rows = chunk_rows(c)
        cum = cum_s[c]
        cum_end = cum[L - 1:L, :]
        ast_s[c] = stack(a_ref[rows, :] * jnp.exp(cum - lw_ref[rows, :])).astype(BF16)
        r_t = r_ref[rows, :] * jnp.exp(cum)
        qp_s[rows, :] = r_t
        rst_s[c] = stack(r_t).astype(BF16)
        g_inv = jnp.exp(-cum)
        bk_s[c] = jnp.concatenate([dup(b_ref[rows, :] * g_inv), dup(k_ref[rows, :] * g_inv)], axis=0).astype(BF16)
        g_end = jnp.exp(cum_end - cum)
        be_s[c] = stack(b_ref[rows, :] * g_end).astype(BF16)
        ke_s[c] = stack(k_ref[rows, :] * g_end).astype(BF16)
        vst_s[c] = stack(v_ref[rows, :]).astype(BF16)
        cend_s[c] = jnp.broadcast_to(cum_end, (SUBLANES, P))

    def stage_scores(c):
        ar_st = jnp.concatenate([ast_s[c], rst_s[c]], axis=0)
        ar_bk = _dot_nt(ar_st, bk_s[c])
        a_ab = jnp.where(strict, ar_bk[:P, :P], 0.0)
        pw_s[c] = a_ab.astype(BF16)
        tinv_s[c] = jnp.where(eye, 1.0, a_ab)
        rb_s[c] = jnp.where(incl, ar_bk[P:, :P], 0.0).astype(BF16)
        akrk_s[c] = jnp.concatenate([jnp.where(strict, ar_bk[:P, P:], 0.0),
                                     jnp.where(incl, ar_bk[P:, P:], 0.0)], axis=0).astype(BF16)

    def stage_inverse_square(c):
        pw = pw_s[c]
        pw_s[c] = _dot(pw, pw).astype(BF16)

    def stage_inverse_level(c):
        pw = pw_s[c]
        tinv = tinv_s[c]
        both = _dot(jnp.concatenate([pw, tinv.astype(BF16)], axis=0), pw)
        pw_s[c] = both[:P].astype(BF16)
        tinv_s[c] = tinv + both[P:]

    def stage_inverse_last(c):
        tinv = tinv_s[c]
        tinv_s[c] = tinv + _dot(tinv.astype(BF16), pw_s[c])

    def stage_av(c):
        both = _dot(akrk_s[c], vst_s[c])
        av_s[c] = both[:P].astype(BF16)
        yi_s[chunk_rows(c), :] = fold(both[P:])

    def stage_wu(c):
        rhs = jnp.concatenate([ast_s[c], av_s[c]], axis=1)
        wu_s[c] = _dot(tinv_s[c].astype(BF16), rhs).astype(BF16)

    def stage_summary(c):
        rows = chunk_rows(c)
        wu = wu_s[c]
        v_st = vst_s[c]
        rb_wu = _dot(rb_s[c], wu)
        be_wu = _dot_tn(be_s[c], wu)
        qp_s[rows, :] = qp_s[rows, :] + fold(rb_wu[:, :P])
        yi_s[rows, :] = yi_s[rows, :] + fold(rb_wu[:, P:])
        d_end = jnp.where(eye, jnp.broadcast_to(jnp.exp(cend_s[c][0:1, :]), (P, P)), 0.0)
        m_s[c] = d_end + be_wu[:, :P]
        n_s[c] = be_wu[:, P:] + _dot_tn(ke_s[c], v_st)

    for_chunk_groups(stage_cumdecay, group)
    for_chunk_groups(stage_scale, min(group, 4))
    for_chunk_groups(stage_scores, group)

    def inverse_levels(_, carry):
        for_chunk_groups(stage_inverse_level, inv_group)
        return carry

    n_powers = (L - 1).bit_length()
    inv_group = min(2 * group, n_chunks)
    for_chunk_groups(stage_inverse_square, inv_group)
    lax.fori_loop(0, n_powers - 2, inverse_levels, 0)
    for_chunk_groups(stage_inverse_last, inv_group)
    for_chunk_groups(stage_av, group)
    for_chunk_groups(stage_wu, group)
    def head_stat(x):
        s0 = jnp.sum(x * head_mask[0], axis=-1, keepdims=True)
        s1 = jnp.sum(x * head_mask[1], axis=-1, keepdims=True)
        return s0 * head_mask[0] + s1 * head_mask[1]

    def emit(c):
        rows = chunk_rows(c)
        y = _dot(qp_s[rows, :].astype(BF16), st_s[c].astype(BF16)) + yi_s[rows, :]
        yc = y - head_stat(y) * (1.0 / N)
        var = head_stat(yc * yc) * (1.0 / N)
        y = yc * lax.rsqrt(var + RWKV_GN_EPS) * lnw_ref[...] + lnb_ref[...]
        bonus = head_stat(r_ref[rows, :] * k_ref[rows, :] * rk_ref[...])
        out_ref[rows, :] = ((y + bonus * v_ref[rows, :]) * g_ref[rows, :]).astype(out_ref.dtype)

    per = max(1, min(4, n_chunks // 2))
    n_groups = n_chunks // per

    def chain(g):
        for j in range(per):
            c = g * per + j
            st_s[c + 1] = _dot_x3(m_s[c], st_s[c]) + n_s[c]

    def emit_group(g):
        for j in range(per):
            emit(g * per + j)

    def summarise_group(g):
        for j in range(per):
            stage_summary(g * per + j)

    def steady(g, _):
        emit_group(g - 1)
        chain(g)
        summarise_group(g + 1)
        return 0

    st_s[0] = jnp.zeros((P, P), F32)
    summarise_group(0)
    chain(0)
    summarise_group(1)
    lax.fori_loop(1, n_groups - 1, steady, 0)
    emit_group(n_groups - 2)
    chain(n_groups - 1)
    emit_group(n_groups - 1)


def _rwkv_mixer(proj, p_off, bsz, seq, layer, v_first, shift_mu, w0, w2, a0, a2, g2, k_k, k_a, r_k, ln_w, ln_b,
                v0, v1, v2, tm):
    m = proj.shape[0]
    width = w0.shape[1]
    half = width // 2
    n_lora = w2.shape[1] + a2.shape[1] + g2.shape[1]
    assert n_lora <= half and p_off % half == 0 and 3 * width % half == 0
    has_vres = layer > 0
    cb0 = p_off // half
    rows8 = tm // SUBLANES
    operands, in_specs = [], []
    for c in range(7):
        operands.append(proj)
        in_specs.append(pl.BlockSpec((tm, half), lambda i, c=c: (i, cb0 + c)))
    for c in range(7):
        operands.append(proj)
        in_specs.append(pl.BlockSpec((SUBLANES, half), lambda i, c=c: (jnp.maximum(i * rows8 - 1, 0), cb0 + c)))

    def full2(x):
        operands.append(x)
        in_specs.append(pl.BlockSpec(x.shape, lambda i: (0, 0)))

    def lrow(x):
        a3, spec = _layer_row(x, layer)
        operands.append(a3)
        in_specs.append(spec)

    mu = jnp.pad(shift_mu[layer], (0, 7 * half - shift_mu.shape[1]))[None]
    full2(mu)
    head_of_lane = np.arange(LANES) // RWKV_HEAD_DIM
    full2(jnp.asarray(head_of_lane[:, None] == head_of_lane[None, :], dtype=F32))
    d_lo, a_lo = w2.shape[1], a2.shape[1]
    w2p = jnp.pad(w2[layer], ((0, half - d_lo), (0, 0)))
    a2p = jnp.pad(a2[layer], ((d_lo, half - d_lo - a_lo), (0, 0)))
    g2p = jnp.pad(g2[layer], ((d_lo + a_lo, half - n_lora), (0, 0)))
    lrow(w0)
    full2(w2p.astype(BF16))
    lrow(a0)
    full2(a2p.astype(BF16))
    full2(g2p.astype(BF16))
    lrow(k_k)
    lrow(k_a)
    if has_vres:
        operands.append(v_first)
        in_specs.append(pl.BlockSpec((tm, width), lambda i: (i, 0)))
        lrow_layer = layer - 1
        for x in (v0,):
            a3 = x.reshape(x.shape[0], 1, width)
            operands.append(a3)
            in_specs.append(pl.BlockSpec((None, 1, width), lambda i: (lrow_layer, 0, 0)))
        full2(v1[layer - 1].astype(BF16))
        full2(v2[layer - 1].astype(BF16))
    tile = pl.BlockSpec((tm, width), lambda i: (i, 0))
    shp = jax.ShapeDtypeStruct((m, width), F32)
    r, lw, k, v, a, b, g = pl.pallas_call(
        functools.partial(_rwkv_prep_kernel, has_vres=has_vres, tiles_per_seq=seq // tm, width=width, half=half),
        grid=(m // tm,),
        in_specs=in_specs,
        out_specs=[tile] * 7,
        out_shape=[shp] * 7,
        compiler_params=_cparams(("parallel",)),
        name="rwkv_prep",
    )(*operands)

    n_pairs = width // LANES
    seq_spec = pl.BlockSpec((None, seq, LANES), lambda bi, p: (bi, 0, p))
    row_spec = pl.BlockSpec((None, 1, LANES), lambda bi, p: (layer, 0, p))
    n_chunks = seq // WKV_CHUNK
    N = RWKV_HEAD_DIM
    y = pl.pallas_call(
        functools.partial(_wkv_kernel, seq=seq, group=min(8, n_chunks)),
        grid=(bsz, n_pairs),
        in_specs=[seq_spec] * 7 + [row_spec] * 3,
        out_specs=seq_spec,
        out_shape=jax.ShapeDtypeStruct((bsz, seq, width), BF16),
        scratch_shapes=(
            [pltpu.VMEM((seq, LANES), F32)] * 2
            + [pltpu.VMEM((n_chunks, LANES, LANES), F32)] * 2
            + [pltpu.VMEM((n_chunks + 1, LANES, LANES), F32)]
            + [pltpu.VMEM((n_chunks, LANES, LANES), BF16)]
            + [pltpu.VMEM((n_chunks, LANES, LANES), F32)]
            + [pltpu.VMEM((n_chunks, LANES, LANES), BF16)] * 7
            + [pltpu.VMEM((n_chunks, 2 * LANES, LANES), BF16)] * 2
            + [pltpu.VMEM((n_chunks, LANES, 2 * LANES), BF16)]
            + [pltpu.VMEM((n_chunks, WKV_CHUNK, LANES), F32)]
            + [pltpu.VMEM((n_chunks, SUBLANES, LANES), F32)]
        ),
        compiler_params=_cparams(("parallel", "parallel")),
        name="wkv7",
    )(*[x.reshape(bsz, seq, width) for x in (r, lw, k, v, a, b, g)],
      ln_w.reshape(-1, 1, width), ln_b.reshape(-1, 1, width), r_k.reshape(-1, 1, width))
    return y, (v_first if has_vres else v)


def _row_tiles(m):
    plan = dict(ffn_up=8192, in_proj=4096, ffn_down=2048, narrow=2048, rwkv_prep=256)
    return {k: min(v, m) for k, v in plan.items()}


def _trunk(x, ffn1_norm, ffn1_up, ffn1_down, mix_norm, w_in, s5_a_re, s5_a_im, s5_b_re, s5_b_im, s5_c_re, s5_c_im,
           s5_log_dt, s5_d, s5_w_glu, attn_q_gain, attn_k_gain, rwkv_shift_mu, rwkv_w0, rwkv_w2, rwkv_a0, rwkv_a2,
           rwkv_g2, rwkv_k_k, rwkv_k_a, rwkv_r_k, rwkv_ln_w, rwkv_ln_b, rwkv_v0, rwkv_v1, rwkv_v2,
           w_branch_s5, w_branch_attn, w_branch_rwkv, w_out, ffn2_norm, ffn2_up, ffn2_down):
    bsz, seq, d_model = x.shape
    depth = w_in.shape[0]
    m = bsz * seq
    s5_width = s5_d.shape[1]
    attn_width = len(DIL_PATTERNS) * ATTN_HEADS * ATTN_HEAD_DIM
    rwkv_width = rwkv_w0.shape[1]
    rwkv_part = rwkv_shift_mu.shape[1]
    q_off = s5_width
    p_off = s5_width + 3 * attn_width
    gate_off = p_off + rwkv_part
    tn_in = 512
    tiles = _row_tiles(m)
    w_in_t = jnp.swapaxes(w_in, 1, 2)
    n_front = -(-gate_off // tn_in) * tn_in
    rk3 = rwkv_r_k.reshape(depth, rwkv_width)

    xf = x.reshape(m, d_model)
    v_first = None
    for l in range(depth):
        xf = _swiglu_ffn(xf, ffn1_norm, ffn1_up, ffn1_down, l, tiles["ffn_up"], tiles["ffn_down"])
        h = _rmsnorm(xf, mix_norm, l)
        proj = _mm("in_proj", [h], [(w_in_t, l, ("T", 0), 0)], [], _ep_plain, n_front, F32, tiles["in_proj"], tn_in)
        gates = _mm("in_proj_gates", [h], [(w_in_t, l, ("T", gate_off), 0)], [], _ep_plain,
                    3 * d_model, F32, tiles["in_proj"], tn_in)
        proj3 = proj.reshape(bsz, seq, n_front)

        z = _s5_mixer_pre_glu(proj3, l, s5_a_re, s5_a_im, s5_b_re, s5_b_im, s5_c_re, s5_c_im, s5_log_dt, s5_d)
        z = z.reshape(m, s5_width)
        y_s5 = _mm("s5_glu", [z], [(s5_w_glu, l, 0, 0)], [(z, 0)], _ep_glu, s5_width, BF16, tiles["narrow"], 256)

        y_attn = _attention(proj3, q_off, l, attn_q_gain, attn_k_gain).reshape(m, -1)

        y_rwkv, v_first = _rwkv_mixer(proj, p_off, bsz, seq, l, v_first, rwkv_shift_mu, rwkv_w0, rwkv_w2, rwkv_a0,
                                      rwkv_a2, rwkv_g2, rwkv_k_k, rwkv_k_a, rk3, rwkv_ln_w, rwkv_ln_b,
                                      rwkv_v0, rwkv_v1, rwkv_v2, tiles["rwkv_prep"])
        y_rwkv = y_rwkv.reshape(m, rwkv_width)

        tn = 256
        gb = d_model // tn
        merged = _mm("branch_merge", [y_s5, y_attn, y_rwkv],
                     [(w_branch_s5, l, 0, 0), (w_branch_attn, l, 0, 1), (w_branch_rwkv, l, 0, 2)],
                     [(gates, 0), (gates, gb), (gates, 2 * gb)], _ep_gated_sum, d_model, BF16, tiles["narrow"], tn)
        xf = _mm("mix_out", [merged], [(w_out, l, 0, 0)], [(xf, 0)], _ep_residual, d_model, F32, tiles["narrow"], tn)
        xf = _swiglu_ffn(xf, ffn2_norm, ffn2_up, ffn2_down, l, tiles["ffn_up"], tiles["ffn_down"])
    return xf.reshape(bsz, seq, d_model)


def kernel(x, ffn1_norm, ffn1_up, ffn1_down, mix_norm, w_in, s5_a_re, s5_a_im, s5_b_re, s5_b_im, s5_c_re, s5_c_im, s5_log_dt, s5_d, s5_w_glu, attn_q_gain, attn_k_gain, rwkv_shift_mu, rwkv_w0, rwkv_w2, rwkv_a0, rwkv_a2, rwkv_g2, rwkv_k_k, rwkv_k_a, rwkv_r_k, rwkv_ln_w, rwkv_ln_b, rwkv_v0, rwkv_v1, rwkv_v2, w_branch_s5, w_branch_attn, w_branch_rwkv, w_out, ffn2_norm, ffn2_up, ffn2_down):
    return _trunk(x, ffn1_norm, ffn1_up, ffn1_down, mix_norm, w_in, s5_a_re, s5_a_im, s5_b_re, s5_b_im, s5_c_re,
                  s5_c_im, s5_log_dt, s5_d, s5_w_glu, attn_q_gain, attn_k_gain, rwkv_shift_mu, rwkv_w0, rwkv_w2,
                  rwkv_a0, rwkv_a2, rwkv_g2, rwkv_k_k, rwkv_k_a, rwkv_r_k, rwkv_ln_w, rwkv_ln_b, rwkv_v0, rwkv_v1,
                  rwkv_v2, w_branch_s5, w_branch_attn, w_branch_rwkv, w_out, ffn2_norm, ffn2_up, ffn2_down)
```

```python
import functools
import math

import numpy as np
import jax
import jax.numpy as jnp
from jax import lax
from jax.experimental import pallas as pl
from jax.experimental.pallas import tpu as pltpu

F32 = jnp.float32
BF16 = jnp.bfloat16

LANES = 128
SUBLANES = 8
VMEM_LIMIT_BYTES = 56 * 1024 * 1024

RMS_EPS = 1e-6
S5_GROUP = 16
S5_STATE = 64
S5_TBL_LEVEL = SUBLANES
S5_TBL_SEG = SUBLANES + 3
S5_N_TBL = SUBLANES + 4
ATTN_HEAD_DIM = 64
ATTN_HEADS = 8
DIL_PATTERNS = ((128, 1), (512, 4), (2048, 16))
ALIBI_MAX_BIAS = 8.0
ATTN_BLOCK = 128
RWKV_HEAD_DIM = 64
RWKV_GN_EPS = 64e-5
WKV_CHUNK = 64
NEG_BIG = -1e30


def _cparams(sem):
    return pltpu.CompilerParams(dimension_semantics=sem, vmem_limit_bytes=VMEM_LIMIT_BYTES)


def _dot(a, b, precision=None):
    return jnp.dot(a, b, preferred_element_type=F32, precision=precision)


def _dot_nt(a, b, precision=None):
    return lax.dot_general(a, b, (((1,), (1,)), ((), ())), preferred_element_type=F32, precision=precision)


def _dot_tn(a, b, precision=None):
    return lax.dot_general(a, b, (((0,), (0,)), ((), ())), preferred_element_type=F32, precision=precision)


def _sigmoid(x):
    return 1.0 / (1.0 + jnp.exp(-x))


def _layer_row(arr, layer):
    n = arr.shape[-1]
    a3 = arr.reshape(arr.shape[0], 1, n)
    return a3, pl.BlockSpec((None, 1, n), lambda *_: (layer, 0, 0))


def _rmsnorm_kernel(x_ref, g_ref, o_ref):
    x = x_ref[...]
    y = x * lax.rsqrt(jnp.mean(x * x, axis=-1, keepdims=True) + RMS_EPS)
    o_ref[...] = (y * g_ref[...]).astype(o_ref.dtype)


def _rmsnorm(x, gain, layer):
    m, d = x.shape
    tm = min(m, 512)
    g3, g_spec = _layer_row(gain, layer)
    return pl.pallas_call(
        _rmsnorm_kernel,
        grid=(m // tm,),
        in_specs=[pl.BlockSpec((tm, d), lambda i: (i, 0)), g_spec],
        out_specs=pl.BlockSpec((tm, d), lambda i: (i, 0)),
        out_shape=jax.ShapeDtypeStruct((m, d), BF16),
        compiler_params=_cparams(("parallel",)),
        name="rmsnorm",
    )(x, g3)


def _mm_kernel(*refs, n_lhs, lhs_needs_cast, rhs_lhs, rhs_parts, n_extra, epilogue, tn, row_chunk):
    it = iter(refs)
    lhs_refs = [next(it) for _ in range(n_lhs)]
    rhs_refs = [[next(it) for _ in range(n_parts)] for n_parts, _ in rhs_parts]
    extra_refs = [next(it) for _ in range(n_extra)]
    out_ref = next(it)
    lhs_scr = {p: next(it) for p in range(n_lhs) if lhs_needs_cast[p]}
    w_scr = [next(it) for _ in range(n_lhs)]

    @pl.when(pl.program_id(1) == 0)
    def _():
        for p, scr in lhs_scr.items():
            scr[...] = lhs_refs[p][...].astype(BF16)

    for p in range(n_lhs):
        mine = [r for r, q in enumerate(rhs_lhs) if q == p]
        for n, r in enumerate(mine):
            transposed = rhs_parts[r][1]
            parts = [t[...] for t in rhs_refs[r]]
            if transposed:
                w = (parts[0] if len(parts) == 1 else jnp.concatenate(parts, axis=0)).T
            else:
                w = parts[0] if len(parts) == 1 else jnp.concatenate(parts, axis=1)
            w_scr[p][:, n * tn:(n + 1) * tn] = w.astype(BF16)

    def chunk(c):
        rows = pl.ds(c * row_chunk, row_chunk)
        accs = [None] * len(rhs_lhs)
        for p in range(n_lhs):
            mine = [r for r, q in enumerate(rhs_lhs) if q == p]
            a = (lhs_scr[p] if lhs_needs_cast[p] else lhs_refs[p])[rows, :]
            acc = _dot(a, w_scr[p][...])
            for n, r in enumerate(mine):
                accs[r] = acc[:, n * tn:(n + 1) * tn]
        out_ref[rows, :] = epilogue(accs, [e[rows, :] for e in extra_refs]).astype(out_ref.dtype)

    for c in range(out_ref.shape[0] // row_chunk):
        chunk(c)


def _mm(name, lhs, rhs, extras, epilogue, n_out, out_dtype, tm, tn, row_chunk=1024):
    m = lhs[0].shape[0]
    tm = min(tm, m)
    row_chunk = min(row_chunk, tm)
    operands, in_specs, lhs_scratch, w_scratch = [], [], [], []
    lhs_needs_cast = []
    for p, a in enumerate(lhs):
        k = a.shape[1]
        operands.append(a)
        in_specs.append(pl.BlockSpec((tm, k), lambda i, j: (i, 0), pipeline_mode=pl.Buffered(1)))
        lhs_needs_cast.append(a.dtype != BF16)
        if a.dtype != BF16:
            lhs_scratch.append(pltpu.VMEM((tm, k), BF16))
        w_scratch.append(pltpu.VMEM((k, tn * sum(1 for r in rhs if r[3] == p)), BF16))
    rhs_parts = []
    for w, layer, off, _ in rhs:
        if isinstance(off, tuple):
            start = off[1]
            part = math.gcd(start, tn)
            assert part % SUBLANES == 0
            n_parts = tn // part
            rhs_parts.append((n_parts, True))
            for t in range(n_parts):
                operands.append(w)
                in_specs.append(pl.BlockSpec(
                    (None, part, w.shape[2]),
                    lambda i, j, b=start // part + t, s=n_parts, layer=layer: (layer, b + s * j, 0)))
        else:
            rhs_parts.append((1, False))
            operands.append(w)
            if layer is None:
                in_specs.append(pl.BlockSpec((w.shape[0], tn), lambda i, j, off=off: (0, off + j)))
            else:
                in_specs.append(pl.BlockSpec((None, w.shape[1], tn),
                                             lambda i, j, off=off, layer=layer: (layer, 0, off + j)))
    for e, off in extras:
        operands.append(e)
        in_specs.append(pl.BlockSpec((tm, tn), lambda i, j, off=off: (i, off + j)))
    kern = functools.partial(
        _mm_kernel, n_lhs=len(lhs), lhs_needs_cast=tuple(lhs_needs_cast), rhs_lhs=tuple(r[3] for r in rhs),
        rhs_parts=tuple(rhs_parts), n_extra=len(extras), epilogue=epilogue, tn=tn, row_chunk=row_chunk)
    return pl.pallas_call(
        kern,
        grid=(m // tm, n_out // tn),
        in_specs=in_specs,
        out_specs=pl.BlockSpec((tm, tn), lambda i, j: (i, j)),
        out_shape=jax.ShapeDtypeStruct((m, n_out), out_dtype),
        scratch_shapes=lhs_scratch + w_scratch,
        compiler_params=_cparams(("parallel", "arbitrary")),
        name=name,
    )(*operands)


def _ep_swiglu(accs, extras):
    a, b = accs
    return a * _sigmoid(a) * b


def _ep_half_residual(accs, extras):
    return extras[0] + 0.5 * accs[0]


def _ep_residual(accs, extras):
    return extras[0] + accs[0]


def _ep_plain(accs, extras):
    return accs[0]


def _ep_glu(accs, extras):
    z = extras[0]
    return z * _sigmoid(accs[0])


def _ep_gated_sum(accs, extras):
    out = _sigmoid(extras[0].astype(F32)) * accs[0]
    for g, a in zip(extras[1:], accs[1:]):
        out = out + _sigmoid(g.astype(F32)) * a
    return out


def _swiglu_ffn(x, norm, w_up, w_down, layer, tm_up, tm_down):
    d_ff = w_down.shape[1]
    tn = LANES
    h = _rmsnorm(x, norm, layer)
    act = _mm("ffn_up", [h], [(w_up, layer, 0, 0), (w_up, layer, d_ff // tn, 0)], [], _ep_swiglu,
              d_ff, BF16, tm_up, tn)
    return _mm("ffn_down", [act], [(w_down, layer, 0, 0)], [(x, 0)], _ep_half_residual,
               x.shape[1], F32, tm_down, 256, row_chunk=512)


def _cmul(ar, ai, br, bi):
    return ar * br - ai * bi, ar * bi + ai * br


def _s5_prep_kernel(are_ref, aim_ref, ldt_ref, bre_ref, bim_ref, bbre_ref, bbim_ref, pw_ref):
    lam_re = jnp.minimum(are_ref[...], -1e-4)
    lam_im = aim_ref[...]
    dt = jnp.exp(ldt_ref[...])
    mag = jnp.exp(lam_re * dt)
    ab_re = mag * jnp.cos(lam_im * dt)
    ab_im = mag * jnp.sin(lam_im * dt)
    den = lam_re * lam_re + lam_im * lam_im
    num_re = ab_re - 1.0
    coef_re = (num_re * lam_re + ab_im * lam_im) / den
    coef_im = (ab_im * lam_re - num_re * lam_im) / den
    for h in range(bre_ref.shape[0]):
        br = bre_ref[h]
        bi = bim_ref[h]
        bbre_ref[h] = coef_re * br - coef_im * bi
        bbim_ref[h] = coef_re * bi + coef_im * br
    powers = [(ab_re, ab_im)]
    for _ in range(SUBLANES - 1):
        powers.append(_cmul(*powers[-1], ab_re, ab_im))
    for t, (pr, pi) in enumerate(powers):
        for r in range(SUBLANES):
            pw_ref[0, t, r] = pr
            pw_ref[1, t, r] = pi
    a8 = powers[-1]
    zero = jnp.zeros_like(ab_re)
    lvl_pow = a8
    for lvl in range(3):
        for r in range(SUBLANES):
            live = r >= (1 << lvl)
            pw_ref[0, S5_TBL_LEVEL + lvl, r] = lvl_pow[0] if live else zero
            pw_ref[1, S5_TBL_LEVEL + lvl, r] = lvl_pow[1] if live else zero
        lvl_pow = _cmul(*lvl_pow, *lvl_pow)
    pr, pi = a8
    for r in range(SUBLANES):
        pw_ref[0, S5_TBL_SEG, r] = pr
        pw_ref[1, S5_TBL_SEG, r] = pi
        pr, pi = _cmul(pr, pi, *a8)


def _s5_scan_kernel(u_ref, bre_ref, bim_ref, cre_ref, cim_ref, pw_ref, d_ref, z_ref, up_s, sre, sim, yp_s,
                    *, seq, lane_chunk):
    S = SUBLANES
    sb_rows = S * S
    n_sb = seq // sb_rows
    n_state = sre.shape[1]

    def transpose_superblocks(read_tile, write_tile):
        def body(sb, _):
            base = pl.multiple_of(sb * sb_rows, sb_rows)
            for t in range(S):
                write_tile(base, t, read_tile(base, t))
            return 0
        lax.fori_loop(0, n_sb, body, 0)

    def put_u(base, t, x):
        up_s[pl.ds(base + t * S, S), :] = x

    transpose_superblocks(lambda base, t: u_ref[pl.ds(base + t, S, stride=S), :], put_u)
    up = up_s[...].astype(BF16)
    sre[...] = _dot(up, bre_ref[...])
    sim[...] = _dot(up, bim_ref[...])

    def scan_superblock(sb, carry):
        base = pl.multiple_of(sb * sb_rows, sb_rows)
        new_carry = []
        for ch in range(n_state // lane_chunk):
            ln = pl.ds(ch * lane_chunk, lane_chunk)
            c_re, c_im = carry[2 * ch], carry[2 * ch + 1]
            ar, ai = pw_ref[0, 0, :, ln], pw_ref[1, 0, :, ln]
            sr, si = sre[pl.ds(base, S), ln], sim[pl.ds(base, S), ln]
            local = [(sr, si)]
            for t in range(1, S):
                rows = pl.ds(base + t * S, S)
                sr, si = sre[rows, ln] + ar * sr - ai * si, sim[rows, ln] + ar * si + ai * sr
                local.append((sr, si))
            er, ei = local[-1]
            for lvl in range(3):
                hr, hi = pw_ref[0, S5_TBL_LEVEL + lvl, :, ln], pw_ref[1, S5_TBL_LEVEL + lvl, :, ln]
                yr, yi = pltpu.roll(er, 1 << lvl, 0), pltpu.roll(ei, 1 << lvl, 0)
                er, ei = er + hr * yr - hi * yi, ei + hr * yi + hi * yr
            qr, qi = pw_ref[0, S5_TBL_SEG, :, ln], pw_ref[1, S5_TBL_SEG, :, ln]
            fr, fi = er + qr * c_re - qi * c_im, ei + qr * c_im + qi * c_re
            first = lax.broadcasted_iota(jnp.int32, fr.shape, 0) == 0
            in_r = jnp.where(first, c_re, pltpu.roll(fr, 1, 0))
            in_i = jnp.where(first, c_im, pltpu.roll(fi, 1, 0))
            for t in range(S):
                rows = pl.ds(base + t * S, S)
                pr, pi = pw_ref[0, t, :, ln], pw_ref[1, t, :, ln]
                sr, si = local[t]
                sre[rows, ln] = sr + pr * in_r - pi * in_i
                sim[rows, ln] = si + pr * in_i + pi * in_r
            new_carry.append(jnp.broadcast_to(fr[S - 1:S, :], fr.shape))
            new_carry.append(jnp.broadcast_to(fi[S - 1:S, :], fi.shape))
        return tuple(new_carry)

    zero = jnp.zeros((S, lane_chunk), F32)
    lax.fori_loop(0, n_sb, scan_superblock, (zero,) * (2 * (n_state // lane_chunk)))
    yp_s[...] = _dot(sre[...].astype(BF16), cre_ref[...]) - _dot(sim[...].astype(BF16), cim_ref[...])
    d = d_ref[...]

    def put_z(base, t, y):
        rows = pl.ds(base + t * S, S)
        z_ref[rows, :] = jax.nn.gelu(y + d * u_ref[rows, :])

    transpose_superblocks(lambda base, t: yp_s[pl.ds(base + t, S, stride=S), :], put_z)


def _s5_mixer_pre_glu(proj3, layer, a_re, a_im, b_re, b_im, c_re, c_im, log_dt, d_skip):
    bsz, seq, _ = proj3.shape
    n_grp, n_st = a_re.shape[1], a_re.shape[2]
    hc = b_re.shape[3]
    gpb = LANES // hc
    n_blk = n_grp // gpb
    nsl = gpb * n_st
    ldt = jnp.broadcast_to(log_dt[layer][:, None], (n_grp, n_st))
    b_re_t = jnp.transpose(b_re[layer], (2, 0, 1))
    b_im_t = jnp.transpose(b_im[layer], (2, 0, 1))
    bb_re, bb_im, pw = pl.pallas_call(
        _s5_prep_kernel,
        out_shape=(jax.ShapeDtypeStruct((hc, n_grp, n_st), F32), jax.ShapeDtypeStruct((hc, n_grp, n_st), F32),
                   jax.ShapeDtypeStruct((2, S5_N_TBL, SUBLANES, n_grp, n_st), F32)),
        name="s5_prep",
    )(a_re[layer], a_im[layer], ldt, b_re_t, b_im_t)
    eye = jnp.eye(gpb, dtype=F32)

    def blockdiag_in(bb):
        x = jnp.transpose(bb, (1, 0, 2)).reshape(n_blk, gpb, hc, n_st)
        x = x[:, :, :, None, :] * eye[None, :, None, :, None]
        return x.reshape(n_blk, gpb * hc, nsl)

    def blockdiag_out(c):
        x = jnp.transpose(c.reshape(n_blk, gpb, hc, n_st), (0, 1, 3, 2))
        x = x[:, :, :, None, :] * eye[None, :, None, :, None]
        return x.reshape(n_blk, nsl, gpb * hc)

    bmat_re, bmat_im = blockdiag_in(bb_re), blockdiag_in(bb_im)
    cmat_re, cmat_im = blockdiag_out(c_re[layer]), blockdiag_out(c_im[layer])
    bmat_re, bmat_im, cmat_re, cmat_im = (x.astype(BF16) for x in (bmat_re, bmat_im, cmat_re, cmat_im))
    pw_b = pw.reshape(2, S5_N_TBL, SUBLANES, n_blk, nsl).transpose(3, 0, 1, 2, 4)
    d3, d_spec = d_skip.reshape(d_skip.shape[0], 1, -1), pl.BlockSpec((None, 1, LANES), lambda b, g: (layer, 0, g))
    return pl.pallas_call(
        functools.partial(_s5_scan_kernel, seq=seq, lane_chunk=2 * LANES),
        grid=(bsz, n_blk),
        in_specs=[
            pl.BlockSpec((None, seq, LANES), lambda b, g: (b, 0, g)),
            pl.BlockSpec((None, LANES, nsl), lambda b, g: (g, 0, 0)),
            pl.BlockSpec((None, LANES, nsl), lambda b, g: (g, 0, 0)),
            pl.BlockSpec((None, nsl, LANES), lambda b, g: (g, 0, 0)),
            pl.BlockSpec((None, nsl, LANES), lambda b, g: (g, 0, 0)),
            pl.BlockSpec((None, 2, S5_N_TBL, SUBLANES, nsl), lambda b, g: (g, 0, 0, 0, 0)),
            d_spec,
        ],
        out_specs=pl.BlockSpec((None, seq, LANES), lambda b, g: (b, 0, g)),
        out_shape=jax.ShapeDtypeStruct((bsz, seq, n_grp * hc), F32),
        scratch_shapes=[pltpu.VMEM((seq, LANES), F32), pltpu.VMEM((seq, nsl), F32), pltpu.VMEM((seq, nsl), F32),
                        pltpu.VMEM((seq, LANES), F32)],
        compiler_params=_cparams(("parallel", "parallel")),
        name="s5_scan",
    )(proj3, bmat_re, bmat_im, cmat_re, cmat_im, pw_b, d3)


def _attn_kernel(slopes_ref, qg_ref, kg_ref, *refs, seq):
    qkv = refs[:9]
    out_ref = refs[9]
    qn_s, kn_s, qm, km, vm, base, s_s, p_s, sc_s, o_seq, l_seq, o_all, l_all = refs[10:]
    pair = pl.program_id(1)
    hd = ATTN_HEAD_DIM
    blk = ATTN_BLOCK
    lane = lax.broadcasted_iota(jnp.int32, (1, LANES), 1)
    head_mask = [(lane < hd).astype(F32), (lane >= hd).astype(F32)]
    qg = qg_ref[...]
    kg = kg_ref[...]

    def qk_norm(x, gain):
        x2 = x * x
        ms = [jnp.sum(x2 * head_mask[h], axis=-1, keepdims=True) * (1.0 / hd) for h in range(2)]
        inv = head_mask[0] * lax.rsqrt(ms[0] + RMS_EPS) + head_mask[1] * lax.rsqrt(ms[1] + RMS_EPS)
        return x * inv * gain

    def block_rows(bi):
        return pl.ds(bi * blk, blk) if isinstance(bi, int) else pl.ds(pl.multiple_of(bi * blk, blk), blk)

    def key_rows(bi, first):
        if first:
            start, size = bi * blk, blk
        else:
            start, size = (bi - 1) * blk, 2 * blk
        return pl.ds(start, size) if isinstance(bi, int) else pl.ds(pl.multiple_of(start, blk), size)

    for g, (window, dil) in enumerate(DIL_PATTERNS):
        q_ref, k_ref, v_ref = qkv[3 * g:3 * g + 3]
        n = seq // dil
        n_blocks = n // blk
        n_back = window // dil
        assert n_back == blk and n % blk == 0
        rows = lax.broadcasted_iota(jnp.int32, (blk, 2 * blk), 0)
        cols = lax.broadcasted_iota(jnp.int32, (blk, 2 * blk), 1)
        delta = rows + blk - cols
        valid = (delta >= 0) & (delta <= n_back)
        for h in range(2):
            slope = slopes_ref[g * ATTN_HEADS + 2 * pair + h]
            base[h] = jnp.where(valid, (-slope * dil) * delta.astype(F32), NEG_BIG)

        if dil == 1:
            qn = qk_norm(q_ref[...], qg)
            km[...] = qk_norm(k_ref[...], kg).astype(BF16)
            v = v_ref[...]
            for h in range(2):
                qm[h] = (qn * head_mask[h]).astype(BF16)
                vm[h] = (v * head_mask[h]).astype(BF16)
        else:
            qn_s[...] = qk_norm(q_ref[...], qg)
            kn_s[...] = qk_norm(k_ref[...], kg)

            def regroup(i, _):
                for j in range(2):
                    r = 2 * i + j
                    dst = pl.ds(pl.multiple_of(r * n, blk), n)
                    q = qn_s[pl.ds(r, n, stride=dil), :]
                    v = v_ref[pl.ds(r, n, stride=dil), :]
                    km[dst, :] = kn_s[pl.ds(r, n, stride=dil), :].astype(BF16)
                    for h in range(2):
                        qm[h, dst, :] = (q * head_mask[h]).astype(BF16)
                        vm[h, dst, :] = (v * head_mask[h]).astype(BF16)
                return 0
            lax.fori_loop(0, dil // 2, regroup, 0)

        def scores(bi, first):
            for h in range(2):
                bias = base[h, :, blk:] if first else base[h]
                s = _dot_nt(qm[h, block_rows(bi), :], km[key_rows(bi, first), :]) * (hd ** -0.5) + bias
                if first:
                    s_s[bi, h, :, pl.ds(0, blk)] = s
                else:
                    s_s[bi, h] = s

        def softmax(bi, first):
            scale = jnp.zeros((blk, LANES), F32)
            lse = jnp.zeros((blk, LANES), F32)
            for h in range(2):
                s = s_s[bi, h, :, pl.ds(0, blk)] if first else s_s[bi, h]
                m = jnp.max(s, axis=-1, keepdims=True)
                p = jnp.exp(s - m)
                l = jnp.sum(p, axis=-1, keepdims=True)
                if first:
                    p_s[bi, h, :, pl.ds(0, blk)] = p.astype(BF16)
                else:
                    p_s[bi, h] = p.astype(BF16)
                scale = scale + head_mask[h] * (1.0 / l)
                lse = lse + head_mask[h] * (m + jnp.log(l))
            sc_s[bi] = scale
            l_seq[block_rows(bi), :] = lse

        def values(bi, first):
            pv = None
            for h in range(2):
                p = p_s[bi, h, :, pl.ds(0, blk)] if first else p_s[bi, h]
                d = _dot(p, vm[h, key_rows(bi, first), :])
                pv = d if pv is None else pv + d
            o_seq[block_rows(bi), :] = pv * sc_s[bi]

        def all_blocks(stage):
            if n_blocks == 1:
                per = 4

                def body(i, _):
                    for j in range(per):
                        stage(i * per + j, True)
                    return 0
                lax.fori_loop(0, dil // per, body, 0)
                return
            for r in range(dil):
                stage(r * n_blocks, True)
            per = 3 if (n_blocks - 1) % 3 == 0 else 1
            groups_per_class = (n_blocks - 1) // per

            def body(i, _):
                cls, grp = i // groups_per_class, i % groups_per_class
                for j in range(per):
                    stage(cls * n_blocks + 1 + grp * per + j, False)
                return 0
            lax.fori_loop(0, dil * groups_per_class, body, 0)

        all_blocks(scores)
        all_blocks(softmax)
        all_blocks(values)

        if dil == 1:
            o_all[g] = o_seq[...]
            l_all[g] = l_seq[...]
        else:
            def scatter(i, _):
                for j in range(2):
                    r = 2 * i + j
                    src = pl.ds(pl.multiple_of(r * n, blk), n)
                    o_all[g, pl.ds(r, n, stride=dil), :] = o_seq[src, :]
                    l_all[g, pl.ds(r, n, stride=dil), :] = l_seq[src, :]
                return 0
            lax.fori_loop(0, dil // 2, scatter, 0)

    l0, l1, l2 = l_all[0], l_all[1], l_all[2]
    mx = jnp.maximum(jnp.maximum(l0, l1), l2)
    w0, w1, w2 = jnp.exp(l0 - mx), jnp.exp(l1 - mx), jnp.exp(l2 - mx)
    o = (w0 * o_all[0] + w1 * o_all[1] + w2 * o_all[2]) / (w0 + w1 + w2)
    out_ref[...] = o.astype(out_ref.dtype)


def _attention(proj3, q_off, layer, q_gain, k_gain):
    bsz, seq, _ = proj3.shape
    n_dil = len(DIL_PATTERNS)
    width = n_dil * ATTN_HEADS * ATTN_HEAD_DIM
    n_pairs = ATTN_HEADS * ATTN_HEAD_DIM // LANES
    n_heads = n_dil * ATTN_HEADS
    n_blk = seq // ATTN_BLOCK
    slopes = jnp.asarray(2.0 ** (-ALIBI_MAX_BIAS * (np.arange(n_heads) + 1) / n_heads), dtype=F32)
    qg = jnp.tile(q_gain[layer], 2)[None]
    kg = jnp.tile(k_gain[layer], 2)[None]
    in_specs = [pl.BlockSpec(memory_space=pltpu.SMEM),
                pl.BlockSpec((1, LANES), lambda b, p: (0, 0)), pl.BlockSpec((1, LANES), lambda b, p: (0, 0))]
    operands = [slopes, qg, kg]
    for g in range(n_dil):
        for which in range(3):
            cb = (q_off + which * width + g * ATTN_HEADS * ATTN_HEAD_DIM) // LANES
            in_specs.append(pl.BlockSpec((None, seq, LANES), lambda b, p, cb=cb: (b, 0, cb + p)))
            operands.append(proj3)
    return pl.pallas_call(
        functools.partial(_attn_kernel, seq=seq),
        grid=(bsz, n_pairs),
        in_specs=in_specs,
        out_specs=pl.BlockSpec((None, seq, LANES), lambda b, p: (b, 0, p)),
        out_shape=jax.ShapeDtypeStruct((bsz, seq, ATTN_HEADS * ATTN_HEAD_DIM), BF16),
        scratch_shapes=[
            pltpu.VMEM((seq, LANES), F32), pltpu.VMEM((seq, LANES), F32),
            pltpu.VMEM((2, seq, LANES), BF16),
            pltpu.VMEM((seq, LANES), BF16), pltpu.VMEM((2, seq, LANES), BF16),
            pltpu.VMEM((2, ATTN_BLOCK, 2 * ATTN_BLOCK), F32),
            pltpu.VMEM((n_blk, 2, ATTN_BLOCK, 2 * ATTN_BLOCK), F32),
            pltpu.VMEM((n_blk, 2, ATTN_BLOCK, 2 * ATTN_BLOCK), BF16),
            pltpu.VMEM((n_blk, ATTN_BLOCK, LANES), F32),
            pltpu.VMEM((seq, LANES), F32), pltpu.VMEM((seq, LANES), F32),
            pltpu.VMEM((n_dil, seq, LANES), F32), pltpu.VMEM((n_dil, seq, LANES), F32),
        ],
        compiler_params=_cparams(("parallel", "parallel")),
        name="dilated_attn",
    )(*operands)


def _split_bf16(x):
    hi = x.astype(BF16)
    return hi, (x - hi.astype(F32)).astype(BF16)


def _dot_x3(a, b):
    ah, al = _split_bf16(a)
    bh, bl = _split_bf16(b)
    return _dot(ah, bh) + (_dot(ah, bl) + _dot(al, bh))


def _head_sums(x, ones_bd):
    parts = []
    for c in range(0, x.shape[1], LANES):
        hi, lo = _split_bf16(x[:, c:c + LANES])
        parts.append(_dot(hi, ones_bd) + _dot(lo, ones_bd))
    return jnp.concatenate(parts, axis=1)


def _rwkv_prep_kernel(*refs, has_vres, tiles_per_seq, width, half):
    it = iter(refs)
    cur = [next(it) for _ in range(7)]
    prev = [next(it) for _ in range(7)]
    mu_ref, ones_ref, w0_ref, w2_ref, a0_ref, a2_ref, g2_ref, kk_ref, ka_ref = [next(it) for _ in range(9)]
    if has_vres:
        vfirst_ref, v0_ref, v1_ref, v2_ref = [next(it) for _ in range(4)]
    r_out, lw_out, k_out, v_out, a_out, b_out, g_out = [next(it) for _ in range(7)]
    vfirst_out = None if has_vres else next(it)

    tm = cur[0].shape[0]
    first_of_seq = (pl.program_id(0) % tiles_per_seq) == 0
    row = lax.broadcasted_iota(jnp.int32, (tm, half), 0)

    def shifted(c):
        x = cur[c][...]
        last_prev = jnp.where(first_of_seq, 0.0, prev[c][SUBLANES - 1:SUBLANES, :])
        x_prev = jnp.where(row == 0, last_prev, pltpu.roll(x, 1, 0))
        mu = mu_ref[:, c * half:(c + 1) * half]
        return x + (x_prev - x) * mu

    sh = [shifted(c) for c in range(7)]
    r = jnp.concatenate(sh[0:2], axis=1)
    k = jnp.concatenate(sh[2:4], axis=1)
    v = jnp.concatenate(sh[4:6], axis=1)
    lora = sh[6]
    w_log = -jax.nn.softplus(-(w0_ref[...] + _dot(jnp.tanh(lora).astype(BF16), w2_ref[...]))) - 0.5
    a = _sigmoid(a0_ref[...] + _dot(lora.astype(BF16), a2_ref[...]))
    g = _dot(_sigmoid(lora).astype(BF16), g2_ref[...])
    if has_vres:
        low = _dot(v.astype(BF16), v1_ref[...])
        mix = _sigmoid(v0_ref[...] + _dot(low.astype(BF16), v2_ref[...]))
        v = v + (vfirst_ref[...] - v) * mix
    ones_bd = ones_ref[...]
    kk = k * kk_ref[...]
    kk = kk * lax.rsqrt(jnp.maximum(_head_sums(kk * kk, ones_bd), 1e-24))
    k = k * (1.0 + (a - 1.0) * ka_ref[...])
    r_out[...] = r.astype(r_out.dtype)
    lw_out[...] = -jnp.exp(w_log)
    k_out[...] = k.astype(k_out.dtype)
    v_out[...] = v.astype(v_out.dtype)
    a_out[...] = (-kk).astype(a_out.dtype)
    b_out[...] = (kk * a).astype(b_out.dtype)
    g_out[...] = g.astype(g_out.dtype)
    if vfirst_out is not None:
        vfirst_out[...] = v


def _wkv_kernel(r_ref, lw_ref, k_ref, v_ref, a_ref, b_ref, g_ref, lnw_ref, lnb_ref, rk_ref, out_ref,
                qp_s, yi_s, m_s, n_s, st_s, pw_s, tinv_s, rb_s, ast_s, vst_s, be_s, ke_s, rst_s, av_s,
                akrk_s, bk_s, wu_s, cum_s, cend_s,
                *, seq, group):
    L = WKV_CHUNK
    N = RWKV_HEAD_DIM
    P = 2 * N
    assert P == LANES and L == N
    n_chunks = seq // L
    ti = lax.broadcasted_iota(jnp.int32, (L, L), 0)
    si = lax.broadcasted_iota(jnp.int32, (L, L), 1)
    tri_incl = jnp.clip((ti - si).astype(F32) + 1.0, 0.0, 1.0).astype(BF16)
    ri = lax.broadcasted_iota(jnp.int32, (P, P), 0)
    ci = lax.broadcasted_iota(jnp.int32, (P, P), 1)
    same_head = (ri // L) == (ci // N)
    strict = same_head & ((ci % L) < (ri % L))
    incl = same_head & ((ci % L) <= (ri % L))
    eye = ri == ci
    lane = lax.broadcasted_iota(jnp.int32, (1, P), 1)
    head_mask = [(lane < N).astype(F32), (lane >= N).astype(F32)]

    def stack(x):
        return jnp.concatenate([x * head_mask[0], x * head_mask[1]], axis=0)

    def dup(x):
        return jnp.concatenate([x, x], axis=0)

    def fold(x):
        return x[:L] + x[L:]

    def for_chunk_groups(fn, per_iter):
        def body(i, _):
            for j in range(per_iter):
                fn(i * per_iter + j)
            return 0
        lax.fori_loop(0, n_chunks // per_iter, body, 0)

    def chunk_rows(c):
        return pl.ds(c * L, L) if isinstance(c, int) else pl.ds(pl.multiple_of(c * L, L), L)

    def stage_cumdecay(c):
        lw_hi, lw_lo = _split_bf16(lw_ref[chunk_rows(c), :])
        cum_s[c] = _dot(tri_incl, lw_hi) + _dot(tri_incl, lw_lo)

    def stage_scale(c):
        rows = chunk_rows(c)
        cum = cum_s[c]
        cum_end = cum[L - 1:L, :]
        ast_s[c] = stack(a_ref[rows, :] * jnp.exp(cum - lw_ref[rows, :])).astype(BF16)
        r_t = r_ref[rows, :] * jnp.exp(cum)
        qp_s[rows, :] = r_t
        rst_s[c] = stack(r_t).astype(BF16)
        g_inv = jnp.exp(-cum)
        bk_s[c] = jnp.concatenate([dup(b_ref[rows, :] * g_inv), dup(k_ref[rows, :] * g_inv)], axis=0).astype(BF16)
        g_end = jnp.exp(cum_end - cum)
        be_s[c] = stack(b_ref[rows, :] * g_end).astype(BF16)
        ke_s[c] = stack(k_ref[rows, :] * g_end).astype(BF16)
        vst_s[c] = stack(v_ref[rows, :]).astype(BF16)
        cend_s[c] = jnp.broadcast_to(cum_end, (SUBLANES, P))

    def stage_scores(c):
        ar_st = jnp.concatenate([ast_s[c], rst_s[c]], axis=0)
        ar_bk = _dot_nt(ar_st, bk_s[c])
        a_ab = jnp.where(strict, ar_bk[:P, :P], 0.0)
        pw_s[c] = a_ab.astype(BF16)
        tinv_s[c] = jnp.where(eye, 1.0, a_ab)
        rb_s[c] = jnp.where(incl, ar_bk[P:, :P], 0.0).astype(BF16)
        akrk_s[c] = jnp.concatenate([jnp.where(strict, ar_bk[:P, P:], 0.0),
                                     jnp.where(incl, ar_bk[P:, P:], 0.0)], axis=0).astype(BF16)

    def stage_inverse_square(c):
        pw = pw_s[c]
        pw_s[c] = _dot(pw, pw).astype(BF16)

    def stage_inverse_level(c):
        pw = pw_s[c]
        tinv = tinv_s[c]
        both = _dot(jnp.concatenate([pw, tinv.astype(BF16)], axis=0), pw)
        pw_s[c] = both[:P].astype(BF16)
        tinv_s[c] = tinv + both[P:]

    def stage_inverse_last(c):
        tinv = tinv_s[c]
        tinv_s[c] = tinv + _dot(tinv.astype(BF16), pw_s[c])

    def stage_av(c):
        both = _dot(akrk_s[c], vst_s[c])
        av_s[c] = both[:P].astype(BF16)
        yi_s[chunk_rows(c), :] = fold(both[P:])

    def stage_wu(c):
        rhs = jnp.concatenate([ast_s[c], av_s[c]], axis=1)
        wu_s[c] = _dot(tinv_s[c].astype(BF16), rhs).astype(BF16)

    def stage_summary(c):
        rows = chunk_rows(c)
        wu = wu_s[c]
        v_st = vst_s[c]
        rb_wu = _dot(rb_s[c], wu)
        be_wu = _dot_tn(be_s[c], wu)
        qp_s[rows, :] = qp_s[rows, :] + fold(rb_wu[:, :P])
        yi_s[rows, :] = yi_s[rows, :] + fold(rb_wu[:, P:])
        d_end = jnp.where(eye, jnp.broadcast_to(jnp.exp(cend_s[c][0:1, :]), (P, P)), 0.0)
        m_s[c] = d_end + be_wu[:, :P]
        n_s[c] = be_wu[:, P:] + _dot_tn(ke_s[c], v_st)

    for_chunk_groups(stage_cumdecay, group)
    for_chunk_groups(stage_scale, min(group, 4))
    for_chunk_groups(stage_scores, group)

    def inverse_levels(_, carry):
        for_chunk_groups(stage_inverse_level, inv_group)
        return carry

    n_powers = (L - 1).bit_length()
    inv_group = min(2 * group, n_chunks)
    for_chunk_groups(stage_inverse_square, inv_group)
    lax.fori_loop(0, n_powers - 2, inverse_levels, 0)
    for_chunk_groups(stage_inverse_last, inv_group)
    for_chunk_groups(stage_av, group)
    for_chunk_groups(stage_wu, group)

    def head_stat(x):
        s0 = jnp.sum(x * head_mask[0], axis=-1, keepdims=True)
        s1 = jnp.sum(x * head_mask[1], axis=-1, keepdims=True)
        return s0 * head_mask[0] + s1 * head_mask[1]

    def emit(c):
        rows = chunk_rows(c)
        y = _dot(qp_s[rows, :].astype(BF16), st_s[c].astype(BF16)) + yi_s[rows, :]
        yc = y - head_stat(y) * (1.0 / N)
        var = head_stat(yc * yc) * (1.0 / N)
        y = yc * lax.rsqrt(var + RWKV_GN_EPS) * lnw_ref[...] + lnb_ref[...]
        bonus = head_stat(r_ref[rows, :].astype(F32) * k_ref[rows, :] * rk_ref[...])
        out_ref[rows, :] = ((y + bonus * v_ref[rows, :]) * g_ref[rows, :]).astype(out_ref.dtype)

    per = max(1, min(4, n_chunks // 2))
    n_groups = n_chunks // per

    def chain(g):
        for j in range(per):
            c = g * per + j
            st_s[c + 1] = _dot_x3(m_s[c], st_s[c]) + n_s[c]

    def emit_group(g):
        for j in range(per):
            emit(g * per + j)

    def summarise_group(g):
        for j in range(per):
            stage_summary(g * per + j)

    def steady(g, _):
        emit_group(g - 1)
        chain(g)
        summarise_group(g + 1)
        return 0

    st_s[0] = jnp.zeros((P, P), F32)
    summarise_group(0)
    chain(0)
    summarise_group(1)
    lax.fori_loop(1, n_groups - 1, steady, 0)
    emit_group(n_groups - 2)
    chain(n_groups - 1)
    emit_group(n_groups - 1)


def _rwkv_mixer(proj, p_off, bsz, seq, layer, v_first, shift_mu, w0, w2, a0, a2, g2, k_k, k_a, r_k, ln_w, ln_b,
                v0, v1, v2, tm):
    m = proj.shape[0]
    width = w0.shape[1]
    half = width // 2
    n_lora = w2.shape[1] + a2.shape[1] + g2.shape[1]
    assert n_lora <= half and p_off % half == 0 and 3 * width % half == 0
    has_vres = layer > 0
    cb0 = p_off // half
    rows8 = tm // SUBLANES
    operands, in_specs = [], []
    for c in range(7):
        operands.append(proj)
        in_specs.append(pl.BlockSpec((tm, half), lambda i, c=c: (i, cb0 + c)))
    for c in range(7):
        operands.append(proj)
        in_specs.append(pl.BlockSpec((SUBLANES, half), lambda i, c=c: (jnp.maximum(i * rows8 - 1, 0), cb0 + c)))

    def full2(x):
        operands.append(x)
        in_specs.append(pl.BlockSpec(x.shape, lambda i: (0, 0)))

    def lrow(x):
        a3, spec = _layer_row(x, layer)
        operands.append(a3)
        in_specs.append(spec)

    mu = jnp.pad(shift_mu[layer], (0, 7 * half - shift_mu.shape[1]))[None]
    full2(mu)
    head_of_lane = np.arange(LANES) // RWKV_HEAD_DIM
    full2(jnp.asarray(head_of_lane[:, None] == head_of_lane[None, :], dtype=F32))
    d_lo, a_lo = w2.shape[1], a2.shape[1]
    w2p = jnp.pad(w2[layer], ((0, half - d_lo), (0, 0)))
    a2p = jnp.pad(a2[layer], ((d_lo, half - d_lo - a_lo), (0, 0)))
    g2p = jnp.pad(g2[layer], ((d_lo + a_lo, half - n_lora), (0, 0)))
    lrow(w0)
    full2(w2p.astype(BF16))
    lrow(a0)
    full2(a2p.astype(BF16))
    full2(g2p.astype(BF16))
    lrow(k_k)
    lrow(k_a)
    if has_vres:
        operands.append(v_first)
        in_specs.append(pl.BlockSpec((tm, width), lambda i: (i, 0)))
        lrow_layer = layer - 1
        for x in (v0,):
            a3 = x.reshape(x.shape[0], 1, width)
            operands.append(a3)
            in_specs.append(pl.BlockSpec((None, 1, width), lambda i: (lrow_layer, 0, 0)))
        full2(v1[layer - 1].astype(BF16))
        full2(v2[layer - 1].astype(BF16))
    tile = pl.BlockSpec((tm, width), lambda i: (i, 0))
    out_dtypes = [BF16, F32, BF16, BF16, BF16, BF16, BF16] + ([] if has_vres else [F32])
    r, lw, k, v, a, b, g, *rest = pl.pallas_call(
        functools.partial(_rwkv_prep_kernel, has_vres=has_vres, tiles_per_seq=seq // tm, width=width, half=half),
        grid=(m // tm,),
        in_specs=in_specs,
        out_specs=[tile] * len(out_dtypes),
        out_shape=[jax.ShapeDtypeStruct((m, width), dt) for dt in out_dtypes],
        compiler_params=_cparams(("parallel",)),
        name="rwkv_prep",
    )(*operands)
    if not has_vres:
        v_first = rest[0]

    n_pairs = width // LANES
    seq_spec = pl.BlockSpec((None, seq, LANES), lambda bi, p: (bi, 0, p))
    row_spec = pl.BlockSpec((None, 1, LANES), lambda bi, p: (layer, 0, p))
    n_chunks = seq // WKV_CHUNK
    N = RWKV_HEAD_DIM
    y = pl.pallas_call(
        functools.partial(_wkv_kernel, seq=seq, group=min(8, n_chunks)),
        grid=(bsz, n_pairs),
        in_specs=[seq_spec] * 7 + [row_spec] * 3,
        out_specs=seq_spec,
        out_shape=jax.ShapeDtypeStruct((bsz, seq, width), BF16),
        scratch_shapes=(
            [pltpu.VMEM((seq, LANES), F32)] * 2
            + [pltpu.VMEM((n_chunks, LANES, LANES), F32)] * 2
            + [pltpu.VMEM((n_chunks + 1, LANES, LANES), F32)]
            + [pltpu.VMEM((n_chunks, LANES, LANES), BF16)]
            + [pltpu.VMEM((n_chunks, LANES, LANES), F32)]
            + [pltpu.VMEM((n_chunks, LANES, LANES), BF16)] * 7
            + [pltpu.VMEM((n_chunks, 2 * LANES, LANES), BF16)] * 2
            + [pltpu.VMEM((n_chunks, LANES, 2 * LANES), BF16)]
            + [pltpu.VMEM((n_chunks, WKV_CHUNK, LANES), F32)]
            + [pltpu.VMEM((n_chunks, SUBLANES, LANES), F32)]
        ),
        compiler_params=_cparams(("parallel", "parallel")),
        name="wkv7",
    )(*[x.reshape(bsz, seq, width) for x in (r, lw, k, v, a, b, g)],
      ln_w.reshape(-1, 1, width), ln_b.reshape(-1, 1, width), r_k.reshape(-1, 1, width))
    return y, v_first


def _row_tiles(m):
    plan = dict(ffn_up=8192, in_proj=4096, ffn_down=2048, narrow=2048, rwkv_prep=256)
    return {k: min(v, m) for k, v in plan.items()}


def _trunk(x, ffn1_norm, ffn1_up, ffn1_down, mix_norm, w_in, s5_a_re, s5_a_im, s5_b_re, s5_b_im, s5_c_re, s5_c_im,
           s5_log_dt, s5_d, s5_w_glu, attn_q_gain, attn_k_gain, rwkv_shift_mu, rwkv_w0, rwkv_w2, rwkv_a0, rwkv_a2,
           rwkv_g2, rwkv_k_k, rwkv_k_a, rwkv_r_k, rwkv_ln_w, rwkv_ln_b, rwkv_v0, rwkv_v1, rwkv_v2,
           w_branch_s5, w_branch_attn, w_branch_rwkv, w_out, ffn2_norm, ffn2_up, ffn2_down):
    bsz, seq, d_model = x.shape
    depth = w_in.shape[0]
    m = bsz * seq
    s5_width = s5_d.shape[1]
    attn_width = len(DIL_PATTERNS) * ATTN_HEADS * ATTN_HEAD_DIM
    rwkv_width = rwkv_w0.shape[1]
    rwkv_part = rwkv_shift_mu.shape[1]
    q_off = s5_width
    p_off = s5_width + 3 * attn_width
    gate_off = p_off + rwkv_part
    tn_in = 512
    tiles = _row_tiles(m)
    w_in_t = jnp.swapaxes(w_in, 1, 2)
    n_front = -(-gate_off // tn_in) * tn_in
    rk3 = rwkv_r_k.reshape(depth, rwkv_width)

    xf = x.reshape(m, d_model)
    v_first = None
    for l in range(depth):
        xf = _swiglu_ffn(xf, ffn1_norm, ffn1_up, ffn1_down, l, tiles["ffn_up"], tiles["ffn_down"])
        h = _rmsnorm(xf, mix_norm, l)
        proj = _mm("in_proj", [h], [(w_in_t, l, ("T", 0), 0)], [], _ep_plain, n_front, F32, tiles["in_proj"], tn_in)
        gates = _mm("in_proj_gates", [h], [(w_in_t, l, ("T", gate_off), 0)], [], _ep_plain,
                    3 * d_model, BF16, tiles["in_proj"], tn_in)
        proj3 = proj.reshape(bsz, seq, n_front)

        z = _s5_mixer_pre_glu(proj3, l, s5_a_re, s5_a_im, s5_b_re, s5_b_im, s5_c_re, s5_c_im, s5_log_dt, s5_d)
        z = z.reshape(m, s5_width)
        y_s5 = _mm("s5_glu", [z], [(s5_w_glu, l, 0, 0)], [(z, 0)], _ep_glu, s5_width, BF16, tiles["narrow"], 256)

        y_attn = _attention(proj3, q_off, l, attn_q_gain, attn_k_gain).reshape(m, -1)

        y_rwkv, v_first = _rwkv_mixer(proj, p_off, bsz, seq, l, v_first, rwkv_shift_mu, rwkv_w0, rwkv_w2, rwkv_a0,
                                      rwkv_a2, rwkv_g2, rwkv_k_k, rwkv_k_a, rk3, rwkv_ln_w, rwkv_ln_b,
                                      rwkv_v0, rwkv_v1, rwkv_v2, tiles["rwkv_prep"])
        y_rwkv = y_rwkv.reshape(m, rwkv_width)

        tn = 256
        gb = d_model // tn
        merged = _mm("branch_merge", [y_s5, y_attn, y_rwkv],
                     [(w_branch_s5, l, 0, 0), (w_branch_attn, l, 0, 1), (w_branch_rwkv, l, 0, 2)],
                     [(gates, 0), (gates, gb), (gates, 2 * gb)], _ep_gated_sum, d_model, BF16, tiles["narrow"], tn)
        xf = _mm("mix_out", [merged], [(w_out, l, 0, 0)], [(xf, 0)], _ep_residual, d_model, F32, tiles["narrow"], tn)
        xf = _swiglu_ffn(xf, ffn2_norm, ffn2_up, ffn2_down, l, tiles["ffn_up"], tiles["ffn_down"])
    return xf.reshape(bsz, seq, d_model)


def kernel(x, ffn1_norm, ffn1_up, ffn1_down, mix_norm, w_in, s5_a_re, s5_a_im, s5_b_re, s5_b_im, s5_c_re, s5_c_im, s5_log_dt, s5_d, s5_w_glu, attn_q_gain, attn_k_gain, rwkv_shift_mu, rwkv_w0, rwkv_w2, rwkv_a0, rwkv_a2, rwkv_g2, rwkv_k_k, rwkv_k_a, rwkv_r_k, rwkv_ln_w, rwkv_ln_b, rwkv_v0, rwkv_v1, rwkv_v2, w_branch_s5, w_branch_attn, w_branch_rwkv, w_out, ffn2_norm, ffn2_up, ffn2_down):
    return _trunk(x, ffn1_norm, ffn1_up, ffn1_down, mix_norm, w_in, s5_a_re, s5_a_im, s5_b_re, s5_b_im, s5_c_re,
                  s5_c_im, s5_log_dt, s5_d, s5_w_glu, attn_q_gain, attn_k_gain, rwkv_shift_mu, rwkv_w0, rwkv_w2,
                  rwkv_a0, rwkv_a2, rwkv_g2, rwkv_k_k, rwkv_k_a, rwkv_r_k, rwkv_ln_w, rwkv_ln_b, rwkv_v0, rwkv_v1,
                  rwkv_v2, w_branch_s5, w_branch_attn, w_branch_rwkv, w_out, ffn2_norm, ffn2_up, ffn2_down)
```

```python
import functools
import math

import numpy as np
import jax
import jax.numpy as jnp
from jax import lax
from jax.experimental import pallas as pl
from jax.experimental.pallas import tpu as pltpu

F32 = jnp.float32
BF16 = jnp.bfloat16

LANES = 128
SUBLANES = 8
VMEM_LIMIT_BYTES = 56 * 1024 * 1024

RMS_EPS = 1e-6
S5_GROUP = 16
S5_STATE = 64
S5_TBL_LEVEL = SUBLANES
S5_TBL_SEG = SUBLANES + 3
S5_N_TBL = SUBLANES + 4
ATTN_HEAD_DIM = 64
ATTN_HEADS = 8
DIL_PATTERNS = ((128, 1), (512, 4), (2048, 16))
ALIBI_MAX_BIAS = 8.0
ATTN_BLOCK = 128
RWKV_HEAD_DIM = 64
RWKV_GN_EPS = 64e-5
WKV_CHUNK = 64
NEG_BIG = -1e30


def _cparams(sem):
    return pltpu.CompilerParams(dimension_semantics=sem, vmem_limit_bytes=VMEM_LIMIT_BYTES)


def _dot(a, b, precision=None):
    return jnp.dot(a, b, preferred_element_type=F32, precision=precision)


def _dot_nt(a, b, precision=None):
    return lax.dot_general(a, b, (((1,), (1,)), ((), ())), preferred_element_type=F32, precision=precision)


def _dot_tn(a, b, precision=None):
    return lax.dot_general(a, b, (((0,), (0,)), ((), ())), preferred_element_type=F32, precision=precision)


def _sigmoid(x):
    return 1.0 / (1.0 + jnp.exp(-x))


def _layer_row(arr, layer):
    n = arr.shape[-1]
    a3 = arr.reshape(arr.shape[0], 1, n)
    return a3, pl.BlockSpec((None, 1, n), lambda *_: (layer, 0, 0))


def _norm_parts_kernel(x_ref, g_ref, xg_ref, rinv_ref):
    x = x_ref[...]
    xg_ref[...] = (x * g_ref[...]).astype(xg_ref.dtype)
    rinv = lax.rsqrt(jnp.mean(x * x, axis=-1, keepdims=True) + RMS_EPS)
    rinv_ref[...] = jnp.broadcast_to(rinv, rinv_ref.shape)


def _norm_parts(x, gain, layer):
    m, d = x.shape
    tm = min(m, 1024)
    g3, g_spec = _layer_row(gain, layer)
    return pl.pallas_call(
        _norm_parts_kernel,
        grid=(m // tm,),
        in_specs=[pl.BlockSpec((tm, d), lambda i: (i, 0)), g_spec],
        out_specs=[pl.BlockSpec((tm, d), lambda i: (i, 0)), pl.BlockSpec((tm, LANES), lambda i: (i, 0))],
        out_shape=[jax.ShapeDtypeStruct((m, d), BF16), jax.ShapeDtypeStruct((m, LANES), F32)],
        compiler_params=_cparams(("parallel",)),
        name="norm_parts",
    )(x, g3)


def _mm_kernel(*refs, n_lhs, lhs_needs_cast, rhs_lhs, rhs_parts, n_extra, epilogue, tn, row_chunk, has_row_scale,
               norm_dim):
    it = iter(refs)
    lhs_refs = [next(it) for _ in range(n_lhs)]
    rhs_refs = [[next(it) for _ in range(n_parts)] for n_parts, _ in rhs_parts]
    extra_refs = [next(it) for _ in range(n_extra)]
    scale_ref = next(it) if has_row_scale else None
    gain_ref = next(it) if norm_dim else None
    out_ref = next(it)
    xg_ref, rinv_ref = (next(it), next(it)) if norm_dim else (None, None)
    lhs_scr = {p: next(it) for p in range(n_lhs) if lhs_needs_cast[p]}
    w_scr = [next(it) for _ in range(n_lhs)]
    ss_scr = next(it) if norm_dim else None
    j = pl.program_id(1)

    @pl.when(j == 0)
    def _():
        for p, scr in lhs_scr.items():
            scr[...] = lhs_refs[p][...].astype(BF16)

    for p in range(n_lhs):
        mine = [r for r, q in enumerate(rhs_lhs) if q == p]
        for n, r in enumerate(mine):
            transposed = rhs_parts[r][1]
            parts = [t[...] for t in rhs_refs[r]]
            if transposed:
                w = (parts[0] if len(parts) == 1 else jnp.concatenate(parts, axis=0)).T
            else:
                w = parts[0] if len(parts) == 1 else jnp.concatenate(parts, axis=1)
            w_scr[p][:, n * tn:(n + 1) * tn] = w.astype(BF16)

    def lane_fold(x):
        acc = x[:, :LANES]
        for c in range(LANES, x.shape[1], LANES):
            acc = acc + x[:, c:c + LANES]
        return acc

    def chunk(c):
        rows = pl.ds(c * row_chunk, row_chunk)
        accs = [None] * len(rhs_lhs)
        for p in range(n_lhs):
            mine = [r for r, q in enumerate(rhs_lhs) if q == p]
            a = (lhs_scr[p] if lhs_needs_cast[p] else lhs_refs[p])[rows, :]
            acc = _dot(a, w_scr[p][...])
            if has_row_scale:
                rs = scale_ref[rows, :]
                acc = acc * jnp.concatenate([rs] * (acc.shape[1] // LANES), axis=1)
            for n, r in enumerate(mine):
                accs[r] = acc[:, n * tn:(n + 1) * tn]
        res = epilogue(accs, [e[rows, :] for e in extra_refs])
        out_ref[rows, :] = res.astype(out_ref.dtype)
        if norm_dim:
            xg_ref[rows, :] = (res * gain_ref[...]).astype(xg_ref.dtype)
            sq = lane_fold(res * res)
            ss_scr[rows, :] = jnp.where(j == 0, sq, ss_scr[rows, :] + sq)

    for c in range(out_ref.shape[0] // row_chunk):
        chunk(c)

    if norm_dim:
        @pl.when(j == pl.num_programs(1) - 1)
        def _():
            tot = jnp.sum(ss_scr[...], axis=-1, keepdims=True)
            rinv_ref[...] = jnp.broadcast_to(lax.rsqrt(tot * (1.0 / norm_dim) + RMS_EPS), rinv_ref.shape)


def _mm(name, lhs, rhs, extras, epilogue, n_out, out_dtype, tm, tn, row_chunk=1024, row_scale=None, norm_out=None):
    m = lhs[0].shape[0]
    tm = min(tm, m)
    row_chunk = min(row_chunk, tm)
    operands, in_specs, lhs_scratch, w_scratch = [], [], [], []
    lhs_needs_cast = []
    for p, a in enumerate(lhs):
        k = a.shape[1]
        operands.append(a)
        in_specs.append(pl.BlockSpec((tm, k), lambda i, j: (i, 0), pipeline_mode=pl.Buffered(1)))
        lhs_needs_cast.append(a.dtype != BF16)
        if a.dtype != BF16:
            lhs_scratch.append(pltpu.VMEM((tm, k), BF16))
        w_scratch.append(pltpu.VMEM((k, tn * sum(1 for r in rhs if r[3] == p)), BF16))
    rhs_parts = []
    for w, layer, off, _ in rhs:
        if isinstance(off, tuple):
            start = off[1]
            part = math.gcd(start, tn)
            assert part % SUBLANES == 0
            n_parts = tn // part
            rhs_parts.append((n_parts, True))
            for t in range(n_parts):
                operands.append(w)
                in_specs.append(pl.BlockSpec(
                    (None, part, w.shape[2]),
                    lambda i, j, b=start // part + t, s=n_parts, layer=layer: (layer, b + s * j, 0)))
        else:
            rhs_parts.append((1, False))
            operands.append(w)
            if layer is None:
                in_specs.append(pl.BlockSpec((w.shape[0], tn), lambda i, j, off=off: (0, off + j)))
            else:
                in_specs.append(pl.BlockSpec((None, w.shape[1], tn),
                                             lambda i, j, off=off, layer=layer: (layer, 0, off + j)))
    for e, off in extras:
        operands.append(e)
        in_specs.append(pl.BlockSpec((tm, tn), lambda i, j, off=off: (i, off + j)))
    if row_scale is not None:
        operands.append(row_scale)
        in_specs.append(pl.BlockSpec((tm, LANES), lambda i, j: (i, 0)))
    out_specs = [pl.BlockSpec((tm, tn), lambda i, j: (i, j))]
    out_shape = [jax.ShapeDtypeStruct((m, n_out), out_dtype)]
    norm_scratch = []
    if norm_out is not None:
        gain, gain_layer = norm_out
        operands.append(gain.reshape(gain.shape[0], 1, n_out))
        in_specs.append(pl.BlockSpec((None, 1, tn), lambda i, j: (gain_layer, 0, j)))
        out_specs += [pl.BlockSpec((tm, tn), lambda i, j: (i, j)), pl.BlockSpec((tm, LANES), lambda i, j: (i, 0))]
        out_shape += [jax.ShapeDtypeStruct((m, n_out), BF16), jax.ShapeDtypeStruct((m, LANES), F32)]
        norm_scratch = [pltpu.VMEM((tm, LANES), F32)]
    kern = functools.partial(
        _mm_kernel, n_lhs=len(lhs), lhs_needs_cast=tuple(lhs_needs_cast), rhs_lhs=tuple(r[3] for r in rhs),
        rhs_parts=tuple(rhs_parts), n_extra=len(extras), epilogue=epilogue, tn=tn, row_chunk=row_chunk,
        has_row_scale=row_scale is not None, norm_dim=n_out if norm_out is not None else 0)
    res = pl.pallas_call(
        kern,
        grid=(m // tm, n_out // tn),
        in_specs=in_specs,
        out_specs=out_specs,
        out_shape=out_shape,
        scratch_shapes=lhs_scratch + w_scratch + norm_scratch,
        compiler_params=_cparams(("parallel", "arbitrary")),
        name=name,
    )(*operands)
    return res if norm_out is not None else res[0]


def _ep_swiglu(accs, extras):
    a, b = accs
    return a * _sigmoid(a) * b


def _ep_half_residual(accs, extras):
    return extras[0] + 0.5 * accs[0]


def _ep_residual(accs, extras):
    return extras[0] + accs[0]


def _ep_plain(accs, extras):
    return accs[0]


def _ep_glu(accs, extras):
    z = extras[0]
    return z * _sigmoid(accs[0])


def _ep_gated_sum(accs, extras):
    out = _sigmoid(extras[0].astype(F32)) * accs[0]
    for g, a in zip(extras[1:], accs[1:]):
        out = out + _sigmoid(g.astype(F32)) * a
    return out


def _swiglu_ffn(x, xg, rinv, w_up, w_down, layer, tm_up, tm_down, next_norm):
    d_ff = w_down.shape[1]
    tn = LANES
    act = _mm("ffn_up", [xg], [(w_up, layer, 0, 0), (w_up, layer, d_ff // tn, 0)], [], _ep_swiglu,
              d_ff, BF16, tm_up, tn, row_scale=rinv)
    return _mm("ffn_down", [act], [(w_down, layer, 0, 0)], [(x, 0)], _ep_half_residual,
               x.shape[1], F32, tm_down, 256, row_chunk=512, norm_out=next_norm)


def _cmul(ar, ai, br, bi):
    return ar * br - ai * bi, ar * bi + ai * br


def _s5_prep_kernel(are_ref, aim_ref, ldt_ref, bre_ref, bim_ref, bbre_ref, bbim_ref, pw_ref):
    lam_re = jnp.minimum(are_ref[...], -1e-4)
    lam_im = aim_ref[...]
    dt = jnp.exp(ldt_ref[...])
    mag = jnp.exp(lam_re * dt)
    ab_re = mag * jnp.cos(lam_im * dt)
    ab_im = mag * jnp.sin(lam_im * dt)
    den = lam_re * lam_re + lam_im * lam_im
    num_re = ab_re - 1.0
    coef_re = (num_re * lam_re + ab_im * lam_im) / den
    coef_im = (ab_im * lam_re - num_re * lam_im) / den
    for h in range(bre_ref.shape[0]):
        br = bre_ref[h]
        bi = bim_ref[h]
        bbre_ref[h] = coef_re * br - coef_im * bi
        bbim_ref[h] = coef_re * bi + coef_im * br
    powers = [(ab_re, ab_im)]
    for _ in range(SUBLANES - 1):
        powers.append(_cmul(*powers[-1], ab_re, ab_im))
    for t, (pr, pi) in enumerate(powers):
        for r in range(SUBLANES):
            pw_ref[0, t, r] = pr
            pw_ref[1, t, r] = pi
    a8 = powers[-1]
    zero = jnp.zeros_like(ab_re)
    lvl_pow = a8
    for lvl in range(3):
        for r in range(SUBLANES):
            live = r >= (1 << lvl)
            pw_ref[0, S5_TBL_LEVEL + lvl, r] = lvl_pow[0] if live else zero
            pw_ref[1, S5_TBL_LEVEL + lvl, r] = lvl_pow[1] if live else zero
        lvl_pow = _cmul(*lvl_pow, *lvl_pow)
    pr, pi = a8
    for r in range(SUBLANES):
        pw_ref[0, S5_TBL_SEG, r] = pr
        pw_ref[1, S5_TBL_SEG, r] = pi
        pr, pi = _cmul(pr, pi, *a8)


def _s5_scan_kernel(u_ref, bre_ref, bim_ref, cre_ref, cim_ref, pw_ref, d_ref, z_ref, up_s, sre, sim, yp_s,
                    *, seq, lane_chunk):
    S = SUBLANES
    sb_rows = S * S
    n_sb = seq // sb_rows
    n_state = sre.shape[1]

    def transpose_superblocks(read_tile, write_tile):
        def body(sb, _):
            base = pl.multiple_of(sb * sb_rows, sb_rows)
            for t in range(S):
                write_tile(base, t, read_tile(base, t))
            return 0
        lax.fori_loop(0, n_sb, body, 0)

    def put_u(base, t, x):
        up_s[pl.ds(base + t * S, S), :] = x

    transpose_superblocks(lambda base, t: u_ref[pl.ds(base + t, S, stride=S), :], put_u)
    up = up_s[...].astype(BF16)
    sre[...] = _dot(up, bre_ref[...])
    sim[...] = _dot(up, bim_ref[...])

    def scan_superblock(sb, carry):
        base = pl.multiple_of(sb * sb_rows, sb_rows)
        new_carry = []
        for ch in range(n_state // lane_chunk):
            ln = pl.ds(ch * lane_chunk, lane_chunk)
            c_re, c_im = carry[2 * ch], carry[2 * ch + 1]
            ar, ai = pw_ref[0, 0, :, ln], pw_ref[1, 0, :, ln]
            sr, si = sre[pl.ds(base, S), ln], sim[pl.ds(base, S), ln]
            local = [(sr, si)]
            for t in range(1, S):
                rows = pl.ds(base + t * S, S)
                sr, si = sre[rows, ln] + ar * sr - ai * si, sim[rows, ln] + ar * si + ai * sr
                local.append((sr, si))
            er, ei = local[-1]
            for lvl in range(3):
                hr, hi = pw_ref[0, S5_TBL_LEVEL + lvl, :, ln], pw_ref[1, S5_TBL_LEVEL + lvl, :, ln]
                yr, yi = pltpu.roll(er, 1 << lvl, 0), pltpu.roll(ei, 1 << lvl, 0)
                er, ei = er + hr * yr - hi * yi, ei + hr * yi + hi * yr
            qr, qi = pw_ref[0, S5_TBL_SEG, :, ln], pw_ref[1, S5_TBL_SEG, :, ln]
            fr, fi = er + qr * c_re - qi * c_im, ei + qr * c_im + qi * c_re
            first = lax.broadcasted_iota(jnp.int32, fr.shape, 0) == 0
            in_r = jnp.where(first, c_re, pltpu.roll(fr, 1, 0))
            in_i = jnp.where(first, c_im, pltpu.roll(fi, 1, 0))
            for t in range(S):
                rows = pl.ds(base + t * S, S)
                pr, pi = pw_ref[0, t, :, ln], pw_ref[1, t, :, ln]
                sr, si = local[t]
                sre[rows, ln] = sr + pr * in_r - pi * in_i
                sim[rows, ln] = si + pr * in_i + pi * in_r
            new_carry.append(jnp.broadcast_to(fr[S - 1:S, :], fr.shape))
            new_carry.append(jnp.broadcast_to(fi[S - 1:S, :], fi.shape))
        return tuple(new_carry)

    zero = jnp.zeros((S, lane_chunk), F32)
    lax.fori_loop(0, n_sb, scan_superblock, (zero,) * (2 * (n_state // lane_chunk)))
    yp_s[...] = _dot(sre[...].astype(BF16), cre_ref[...]) - _dot(sim[...].astype(BF16), cim_ref[...])
    d = d_ref[...]

    def put_z(base, t, y):
        rows = pl.ds(base + t * S, S)
        z_ref[rows, :] = jax.nn.gelu(y + d * u_ref[rows, :])

    transpose_superblocks(lambda base, t: yp_s[pl.ds(base + t, S, stride=S), :], put_z)


def _s5_mixer_pre_glu(proj3, layer, a_re, a_im, b_re, b_im, c_re, c_im, log_dt, d_skip):
    bsz, seq, _ = proj3.shape
    n_grp, n_st = a_re.shape[1], a_re.shape[2]
    hc = b_re.shape[3]
    gpb = LANES // hc
    n_blk = n_grp // gpb
    nsl = gpb * n_st
    ldt = jnp.broadcast_to(log_dt[layer][:, None], (n_grp, n_st))
    b_re_t = jnp.transpose(b_re[layer], (2, 0, 1))
    b_im_t = jnp.transpose(b_im[layer], (2, 0, 1))
    bb_re, bb_im, pw = pl.pallas_call(
        _s5_prep_kernel,
        out_shape=(jax.ShapeDtypeStruct((hc, n_grp, n_st), F32), jax.ShapeDtypeStruct((hc, n_grp, n_st), F32),
                   jax.ShapeDtypeStruct((2, S5_N_TBL, SUBLANES, n_grp, n_st), F32)),
        name="s5_prep",
    )(a_re[layer], a_im[layer], ldt, b_re_t, b_im_t)
    eye = jnp.eye(gpb, dtype=F32)

    def blockdiag_in(bb):
        x = jnp.transpose(bb, (1, 0, 2)).reshape(n_blk, gpb, hc, n_st)
        x = x[:, :, :, None, :] * eye[None, :, None, :, None]
        return x.reshape(n_blk, gpb * hc, nsl)

    def blockdiag_out(c):
        x = jnp.transpose(c.reshape(n_blk, gpb, hc, n_st), (0, 1, 3, 2))
        x = x[:, :, :, None, :] * eye[None, :, None, :, None]
        return x.reshape(n_blk, nsl, gpb * hc)

    bmat_re, bmat_im = blockdiag_in(bb_re), blockdiag_in(bb_im)
    cmat_re, cmat_im = blockdiag_out(c_re[layer]), blockdiag_out(c_im[layer])
    bmat_re, bmat_im, cmat_re, cmat_im = (x.astype(BF16) for x in (bmat_re, bmat_im, cmat_re, cmat_im))
    pw_b = pw.reshape(2, S5_N_TBL, SUBLANES, n_blk, nsl).transpose(3, 0, 1, 2, 4)
    d3, d_spec = d_skip.reshape(d_skip.shape[0], 1, -1), pl.BlockSpec((None, 1, LANES), lambda b, g: (layer, 0, g))
    return pl.pallas_call(
        functools.partial(_s5_scan_kernel, seq=seq, lane_chunk=2 * LANES),
        grid=(bsz, n_blk),
        in_specs=[
            pl.BlockSpec((None, seq, LANES), lambda b, g: (b, 0, g)),
            pl.BlockSpec((None, LANES, nsl), lambda b, g: (g, 0, 0)),
            pl.BlockSpec((None, LANES, nsl), lambda b, g: (g, 0, 0)),
            pl.BlockSpec((None, nsl, LANES), lambda b, g: (g, 0, 0)),
            pl.BlockSpec((None, nsl, LANES), lambda b, g: (g, 0, 0)),
            pl.BlockSpec((None, 2, S5_N_TBL, SUBLANES, nsl), lambda b, g: (g, 0, 0, 0, 0)),
            d_spec,
        ],
        out_specs=pl.BlockSpec((None, seq, LANES), lambda b, g: (b, 0, g)),
        out_shape=jax.ShapeDtypeStruct((bsz, seq, n_grp * hc), F32),
        scratch_shapes=[pltpu.VMEM((seq, LANES), F32), pltpu.VMEM((seq, nsl), F32), pltpu.VMEM((seq, nsl), F32),
                        pltpu.VMEM((seq, LANES), F32)],
        compiler_params=_cparams(("parallel", "parallel")),
        name="s5_scan",
    )(proj3, bmat_re, bmat_im, cmat_re, cmat_im, pw_b, d3)


def _attn_kernel(slopes_ref, qg_ref, kg_ref, *refs, seq):
    qkv = refs[:9]
    out_ref = refs[9]
    qn_s, kn_s, qm, km, vm, base, s_s, p_s, sc_s, o_seq, l_seq, o_all, l_all = refs[10:]
    pair = pl.program_id(1)
    hd = ATTN_HEAD_DIM
    blk = ATTN_BLOCK
    lane = lax.broadcasted_iota(jnp.int32, (1, LANES), 1)
    head_mask = [(lane < hd).astype(F32), (lane >= hd).astype(F32)]
    qg = qg_ref[...]
    kg = kg_ref[...]

    def qk_norm(x, gain):
        x2 = x * x
        ms = [jnp.sum(x2 * head_mask[h], axis=-1, keepdims=True) * (1.0 / hd) for h in range(2)]
        inv = head_mask[0] * lax.rsqrt(ms[0] + RMS_EPS) + head_mask[1] * lax.rsqrt(ms[1] + RMS_EPS)
        return x * inv * gain

    def block_rows(bi):
        return pl.ds(bi * blk, blk) if isinstance(bi, int) else pl.ds(pl.multiple_of(bi * blk, blk), blk)

    def key_rows(bi, first):
        if first:
            start, size = bi * blk, blk
        else:
            start, size = (bi - 1) * blk, 2 * blk
        return pl.ds(start, size) if isinstance(bi, int) else pl.ds(pl.multiple_of(start, blk), size)

    for g, (window, dil) in enumerate(DIL_PATTERNS):
        q_ref, k_ref, v_ref = qkv[3 * g:3 * g + 3]
        n = seq // dil
        n_blocks = n // blk
        n_back = window // dil
        assert n_back == blk and n % blk == 0
        rows = lax.broadcasted_iota(jnp.int32, (blk, 2 * blk), 0)
        cols = lax.broadcasted_iota(jnp.int32, (blk, 2 * blk), 1)
        delta = rows + blk - cols
        valid = (delta >= 0) & (delta <= n_back)
        for h in range(2):
            slope = slopes_ref[g * ATTN_HEADS + 2 * pair + h]
            base[h] = jnp.where(valid, (-slope * dil) * delta.astype(F32), NEG_BIG)

        if dil == 1:
            qn = qk_norm(q_ref[...], qg)
            km[...] = qk_norm(k_ref[...], kg).astype(BF16)
            v = v_ref[...]
            for h in range(2):
                qm[h] = (qn * head_mask[h]).astype(BF16)
                vm[h] = (v * head_mask[h]).astype(BF16)
        else:
            qn_s[...] = qk_norm(q_ref[...], qg)
            kn_s[...] = qk_norm(k_ref[...], kg)

            def regroup(i, _):
                for j in range(2):
                    r = 2 * i + j
                    dst = pl.ds(pl.multiple_of(r * n, blk), n)
                    q = qn_s[pl.ds(r, n, stride=dil), :]
                    v = v_ref[pl.ds(r, n, stride=dil), :]
                    km[dst, :] = kn_s[pl.ds(r, n, stride=dil), :].astype(BF16)
                    for h in range(2):
                        qm[h, dst, :] = (q * head_mask[h]).astype(BF16)
                        vm[h, dst, :] = (v * head_mask[h]).astype(BF16)
                return 0
            lax.fori_loop(0, dil // 2, regroup, 0)

        def scores(bi, first):
            for h in range(2):
                bias = base[h, :, blk:] if first else base[h]
                s = _dot_nt(qm[h, block_rows(bi), :], km[key_rows(bi, first), :]) * (hd ** -0.5) + bias
                if first:
                    s_s[bi, h, :, pl.ds(0, blk)] = s
                else:
                    s_s[bi, h] = s

        def softmax(bi, first):
            scale = jnp.zeros((blk, LANES), F32)
            lse = jnp.zeros((blk, LANES), F32)
            for h in range(2):
                s = s_s[bi, h, :, pl.ds(0, blk)] if first else s_s[bi, h]
                m = jnp.max(s, axis=-1, keepdims=True)
                p = jnp.exp(s - m)
                l = jnp.sum(p, axis=-1, keepdims=True)
                if first:
                    p_s[bi, h, :, pl.ds(0, blk)] = p.astype(BF16)
                else:
                    p_s[bi, h] = p.astype(BF16)
                scale = scale + head_mask[h] * (1.0 / l)
                lse = lse + head_mask[h] * (m + jnp.log(l))
            sc_s[bi] = scale
            l_seq[block_rows(bi), :] = lse

        def values(bi, first):
            pv = None
            for h in range(2):
                p = p_s[bi, h, :, pl.ds(0, blk)] if first else p_s[bi, h]
                d = _dot(p, vm[h, key_rows(bi, first), :])
                pv = d if pv is None else pv + d
            o_seq[block_rows(bi), :] = pv * sc_s[bi]

        def all_blocks(stage):
            if n_blocks == 1:
                per = 4

                def body(i, _):
                    for j in range(per):
                        stage(i * per + j, True)
                    return 0
                lax.fori_loop(0, dil // per, body, 0)
                return
            for r in range(dil):
                stage(r * n_blocks, True)
            per = 3 if (n_blocks - 1) % 3 == 0 else 1
            groups_per_class = (n_blocks - 1) // per

            def body(i, _):
                cls, grp = i // groups_per_class, i % groups_per_class
                for j in range(per):
                    stage(cls * n_blocks + 1 + grp * per + j, False)
                return 0
            lax.fori_loop(0, dil * groups_per_class, body, 0)

        all_blocks(scores)
        all_blocks(softmax)
        all_blocks(values)

        if dil == 1:
            o_all[g] = o_seq[...]
            l_all[g] = l_seq[...]
        else:
            def scatter(i, _):
                for j in range(2):
                    r = 2 * i + j
                    src = pl.ds(pl.multiple_of(r * n, blk), n)
                    o_all[g, pl.ds(r, n, stride=dil), :] = o_seq[src, :]
                    l_all[g, pl.ds(r, n, stride=dil), :] = l_seq[src, :]
                return 0
            lax.fori_loop(0, dil // 2, scatter, 0)

    l0, l1, l2 = l_all[0], l_all[1], l_all[2]
    mx = jnp.maximum(jnp.maximum(l0, l1), l2)
    w0, w1, w2 = jnp.exp(l0 - mx), jnp.exp(l1 - mx), jnp.exp(l2 - mx)
    o = (w0 * o_all[0] + w1 * o_all[1] + w2 * o_all[2]) / (w0 + w1 + w2)
    out_ref[...] = o.astype(out_ref.dtype)


def _attention(proj3, q_off, layer, q_gain, k_gain):
    bsz, seq, _ = proj3.shape
    n_dil = len(DIL_PATTERNS)
    width = n_dil * ATTN_HEADS * ATTN_HEAD_DIM
    n_pairs = ATTN_HEADS * ATTN_HEAD_DIM // LANES
    n_heads = n_dil * ATTN_HEADS
    n_blk = seq // ATTN_BLOCK
    slopes = jnp.asarray(2.0 ** (-ALIBI_MAX_BIAS * (np.arange(n_heads) + 1) / n_heads), dtype=F32)
    qg = jnp.tile(q_gain[layer], 2)[None]
    kg = jnp.tile(k_gain[layer], 2)[None]
    in_specs = [pl.BlockSpec(memory_space=pltpu.SMEM),
                pl.BlockSpec((1, LANES), lambda b, p: (0, 0)), pl.BlockSpec((1, LANES), lambda b, p: (0, 0))]
    operands = [slopes, qg, kg]
    for g in range(n_dil):
        for which in range(3):
            cb = (q_off + which * width + g * ATTN_HEADS * ATTN_HEAD_DIM) // LANES
            in_specs.append(pl.BlockSpec((None, seq, LANES), lambda b, p, cb=cb: (b, 0, cb + p)))
            operands.append(proj3)
    return pl.pallas_call(
        functools.partial(_attn_kernel, seq=seq),
        grid=(bsz, n_pairs),
        in_specs=in_specs,
        out_specs=pl.BlockSpec((None, seq, LANES), lambda b, p: (b, 0, p)),
        out_shape=jax.ShapeDtypeStruct((bsz, seq, ATTN_HEADS * ATTN_HEAD_DIM), BF16),
        scratch_shapes=[
            pltpu.VMEM((seq, LANES), F32), pltpu.VMEM((seq, LANES), F32),
            pltpu.VMEM((2, seq, LANES), BF16),
            pltpu.VMEM((seq, LANES), BF16), pltpu.VMEM((2, seq, LANES), BF16),
            pltpu.VMEM((2, ATTN_BLOCK, 2 * ATTN_BLOCK), F32),
            pltpu.VMEM((n_blk, 2, ATTN_BLOCK, 2 * ATTN_BLOCK), F32),
            pltpu.VMEM((n_blk, 2, ATTN_BLOCK, 2 * ATTN_BLOCK), BF16),
            pltpu.VMEM((n_blk, ATTN_BLOCK, LANES), F32),
            pltpu.VMEM((seq, LANES), F32), pltpu.VMEM((seq, LANES), F32),
            pltpu.VMEM((n_dil, seq, LANES), F32), pltpu.VMEM((n_dil, seq, LANES), F32),
        ],
        compiler_params=_cparams(("parallel", "parallel")),
        name="dilated_attn",
    )(*operands)


def _split_bf16(x):
    hi = x.astype(BF16)
    return hi, (x - hi.astype(F32)).astype(BF16)


def _dot_x3(a, b):
    ah, al = _split_bf16(a)
    bh, bl = _split_bf16(b)
    return _dot(ah, bh) + (_dot(ah, bl) + _dot(al, bh))


def _head_sums(x, ones_bd):
    parts = []
    for c in range(0, x.shape[1], LANES):
        hi, lo = _split_bf16(x[:, c:c + LANES])
        parts.append(_dot(hi, ones_bd) + _dot(lo, ones_bd))
    return jnp.concatenate(parts, axis=1)


def _rwkv_prep_kernel(*refs, has_vres, tiles_per_seq, width, half):
    it = iter(refs)
    cur = [next(it) for _ in range(7)]
    prev = [next(it) for _ in range(7)]
    mu_ref, ones_ref, w0_ref, w2_ref, a0_ref, a2_ref, g2_ref, kk_ref, ka_ref = [next(it) for _ in range(9)]
    if has_vres:
        vfirst_ref, v0_ref, v1_ref, v2_ref = [next(it) for _ in range(4)]
    r_out, lw_out, k_out, v_out, a_out, b_out, g_out = [next(it) for _ in range(7)]
    vfirst_out = None if has_vres else next(it)

    tm = cur[0].shape[0]
    first_of_seq = (pl.program_id(0) % tiles_per_seq) == 0
    row = lax.broadcasted_iota(jnp.int32, (tm, half), 0)

    def shifted(c):
        x = cur[c][...]
        last_prev = jnp.where(first_of_seq, 0.0, prev[c][SUBLANES - 1:SUBLANES, :])
        x_prev = jnp.where(row == 0, last_prev, pltpu.roll(x, 1, 0))
        mu = mu_ref[:, c * half:(c + 1) * half]
        return x + (x_prev - x) * mu

    sh = [shifted(c) for c in range(7)]
    r = jnp.concatenate(sh[0:2], axis=1)
    k = jnp.concatenate(sh[2:4], axis=1)
    v = jnp.concatenate(sh[4:6], axis=1)
    lora = sh[6]
    w_log = -jax.nn.softplus(-(w0_ref[...] + _dot(jnp.tanh(lora).astype(BF16), w2_ref[...]))) - 0.5
    a = _sigmoid(a0_ref[...] + _dot(lora.astype(BF16), a2_ref[...]))
    g = _dot(_sigmoid(lora).astype(BF16), g2_ref[...])
    if has_vres:
        low = _dot(v.astype(BF16), v1_ref[...])
        mix = _sigmoid(v0_ref[...] + _dot(low.astype(BF16), v2_ref[...]))
        v = v + (vfirst_ref[...] - v) * mix
    ones_bd = ones_ref[...]
    kk = k * kk_ref[...]
    kk = kk * lax.rsqrt(jnp.maximum(_head_sums(kk * kk, ones_bd), 1e-24))
    k = k * (1.0 + (a - 1.0) * ka_ref[...])
    r_out[...] = r.astype(r_out.dtype)
    lw_out[...] = -jnp.exp(w_log)
    k_out[...] = k.astype(k_out.dtype)
    v_out[...] = v.astype(v_out.dtype)
    a_out[...] = (-kk).astype(a_out.dtype)
    b_out[...] = (kk * a).astype(b_out.dtype)
    g_out[...] = g.astype(g_out.dtype)
    if vfirst_out is not None:
        vfirst_out[...] = v


def _wkv_kernel(r_ref, lw_ref, k_ref, v_ref, a_ref, b_ref, g_ref, lnw_ref, lnb_ref, rk_ref, out_ref,
                qp_s, yi_s, m_s, n_s, st_s, pw_s, tinv_s, rb_s, ast_s, vst_s, be_s, ke_s, rst_s, av_s,
                akrk_s, bk_s, wu_s, cum_s, cend_s,
                *, seq, group):
    L = WKV_CHUNK
    N = RWKV_HEAD_DIM
    P = 2 * N
    assert P == LANES and L == N
    n_chunks = seq // L
    ti = lax.broadcasted_iota(jnp.int32, (L, L), 0)
    si = lax.broadcasted_iota(jnp.int32, (L, L), 1)
    tri_incl = jnp.clip((ti - si).astype(F32) + 1.0, 0.0, 1.0).astype(BF16)
    ri = lax.broadcasted_iota(jnp.int32, (P, P), 0)
    ci = lax.broadcasted_iota(jnp.int32, (P, P), 1)
    same_head = (ri // L) == (ci // N)
    strict = same_head & ((ci % L) < (ri % L))
    incl = same_head & ((ci % L) <= (ri % L))
    eye = ri == ci
    lane = lax.broadcasted_iota(jnp.int32, (1, P), 1)
    head_mask = [(lane < N).astype(F32), (lane >= N).astype(F32)]

    def stack(x):
        return jnp.concatenate([x * head_mask[0], x * head_mask[1]], axis=0)

    def dup(x):
        return jnp.concatenate([x, x], axis=0)

    def fold(x):
        return x[:L] + x[L:]

    def for_chunk_groups(fn, per_iter):
        def body(i, _):
            for j in range(per_iter):
                fn(i * per_iter + j)
            return 0
        lax.fori_loop(0, n_chunks // per_iter, body, 0)

    def chunk_rows(c):
        return pl.ds(c * L, L) if isinstance(c, int) else pl.ds(pl.multiple_of(c * L, L), L)

    def stage_cumdecay(c):
        lw_hi, lw_lo = _split_bf16(lw_ref[chunk_rows(c), :])
        cum_s[c] = _dot(tri_incl, lw_hi) + _dot(tri_incl, lw_lo)

    def stage_scale(c):
        rows = chunk_rows(c)
        cum = cum_s[c]
        cum_end = cum[L - 1:L, :]
        ast_s[c] = stack(a_ref[rows, :] * jnp.exp(cum - lw_ref[rows, :])).astype(BF16)
        r_t = r_ref[rows, :] * jnp.exp(cum)
        qp_s[rows, :] = r_t
        rst_s[c] = stack(r_t).astype(BF16)
        g_inv = jnp.exp(-cum)
        bk_s[c] = jnp.concatenate([dup(b_ref[rows, :] * g_inv), dup(k_ref[rows, :] * g_inv)], axis=0).astype(BF16)
        g_end = jnp.exp(cum_end - cum)
        be_s[c] = stack(b_ref[rows, :] * g_end).astype(BF16)
        ke_s[c] = stack(k_ref[rows, :] * g_end).astype(BF16)
        vst_s[c] = stack(v_ref[rows, :]).astype(BF16)
        cend_s[c] = jnp.broadcast_to(cum_end, (SUBLANES, P))

    def stage_scores(c):
        ar_st = jnp.concatenate([ast_s[c], rst_s[c]], axis=0)
        ar_bk = _dot_nt(ar_st, bk_s[c])
        a_ab = jnp.where(strict, ar_bk[:P, :P], 0.0)
        pw_s[c] = a_ab.astype(BF16)
        tinv_s[c] = jnp.where(eye, 1.0, a_ab)
        rb_s[c] = jnp.where(incl, ar_bk[P:, :P], 0.0).astype(BF16)
        akrk_s[c] = jnp.concatenate([jnp.where(strict, ar_bk[:P, P:], 0.0),
                                     jnp.where(incl, ar_bk[P:, P:], 0.0)], axis=0).astype(BF16)

    def stage_inverse_square(c):
        pw = pw_s[c]
        pw_s[c] = _dot(pw, pw).astype(BF16)

    def stage_inverse_level(c):
        pw = pw_s[c]
        tinv = tinv_s[c]
        both = _dot(jnp.concatenate([pw, tinv.astype(BF16)], axis=0), pw)
        pw_s[c] = both[:P].astype(BF16)
        tinv_s[c] = tinv + both[P:]

    def stage_inverse_last(c):
        tinv = tinv_s[c]
        tinv_s[c] = tinv + _dot(tinv.astype(BF16), pw_s[c])

    def stage_av(c):
        both = _dot(akrk_s[c], vst_s[c])
        av_s[c] = both[:P].astype(BF16)
        yi_s[chunk_rows(c), :] = fold(both[P:])

    def stage_wu(c):
        rhs = jnp.concatenate([ast_s[c], av_s[c]], axis=1)
        wu_s[c] = _dot(tinv_s[c].astype(BF16), rhs).astype(BF16)

    def stage_summary(c):
        rows = chunk_rows(c)
        wu = wu_s[c]
        v_st = vst_s[c]
        rb_wu = _dot(rb_s[c], wu)
        be_wu = _dot_tn(be_s[c], wu)
        qp_s[rows, :] = qp_s[rows, :] + fold(rb_wu[:, :P])
        yi_s[rows, :] = yi_s[rows, :] + fold(rb_wu[:, P:])
        d_end = jnp.where(eye, jnp.broadcast_to(jnp.exp(cend_s[c][0:1, :]), (P, P)), 0.0)
        m_s[c] = d_end + be_wu[:, :P]
        n_s[c] = be_wu[:, P:] + _dot_tn(ke_s[c], v_st)

    for_chunk_groups(stage_cumdecay, group)
    for_chunk_groups(stage_scale, min(group, 4))
    for_chunk_groups(stage_scores, group)

    def inverse_levels(_, carry):
        for_chunk_groups(stage_inverse_level, inv_group)
        return carry

    n_powers = (L - 1).bit_length()
    inv_group = min(2 * group, n_chunks)
    for_chunk_groups(stage_inverse_square, inv_group)
    lax.fori_loop(0, n_powers - 2, inverse_levels, 0)
    for_chunk_groups(stage_inverse_last, inv_group)
    for_chunk_groups(stage_av, group)
    for_chunk_groups(stage_wu, group)

    def head_stat(x):
        s0 = jnp.sum(x * head_mask[0], axis=-1, keepdims=True)
        s1 = jnp.sum(x * head_mask[1], axis=-1, keepdims=True)
        return s0 * head_mask[0] + s1 * head_mask[1]

    def emit(c):
        rows = chunk_rows(c)
        y = _dot(qp_s[rows, :].astype(BF16), st_s[c].astype(BF16)) + yi_s[rows, :]
        yc = y - head_stat(y) * (1.0 / N)
        var = head_stat(yc * yc) * (1.0 / N)
        y = yc * lax.rsqrt(var + RWKV_GN_EPS) * lnw_ref[...] + lnb_ref[...]
        bonus = head_stat(r_ref[rows, :].astype(F32) * k_ref[rows, :] * rk_ref[...])
        out_ref[rows, :] = ((y + bonus * v_ref[rows, :]) * g_ref[rows, :]).astype(out_ref.dtype)

    per = max(1, min(4, n_chunks // 2))
    n_groups = n_chunks // per

    def chain(g):
        for j in range(per):
            c = g * per + j
            st_s[c + 1] = _dot_x3(m_s[c], st_s[c]) + n_s[c]

    def emit_group(g):
        for j in range(per):
            emit(g * per + j)

    def summarise_group(g):
        for j in range(per):
            stage_summary(g * per + j)

    def steady(g, _):
        emit_group(g - 1)
        chain(g)
        summarise_group(g + 1)
        return 0

    st_s[0] = jnp.zeros((P, P), F32)
    summarise_group(0)
    chain(0)
    summarise_group(1)
    lax.fori_loop(1, n_groups - 1, steady, 0)
    emit_group(n_groups - 2)
    chain(n_groups - 1)
    emit_group(n_groups - 1)


def _rwkv_mixer(proj, p_off, bsz, seq, layer, v_first, shift_mu, w0, w2, a0, a2, g2, k_k, k_a, r_k, ln_w, ln_b,
                v0, v1, v2, tm):
    m = proj.shape[0]
    width = w0.shape[1]
    half = width // 2
    n_lora = w2.shape[1] + a2.shape[1] + g2.shape[1]
    assert n_lora <= half and p_off % half == 0 and 3 * width % half == 0
    has_vres = layer > 0
    cb0 = p_off // half
    rows8 = tm // SUBLANES
    operands, in_specs = [], []
    for c in range(7):
        operands.append(proj)
        in_specs.append(pl.BlockSpec((tm, half), lambda i, c=c: (i, cb0 + c)))
    for c in range(7):
        operands.append(proj)
        in_specs.append(pl.BlockSpec((SUBLANES, half), lambda i, c=c: (jnp.maximum(i * rows8 - 1, 0), cb0 + c)))

    def full2(x):
        operands.append(x)
        in_specs.append(pl.BlockSpec(x.shape, lambda i: (0, 0)))

    def lrow(x):
        a3, spec = _layer_row(x, layer)
        operands.append(a3)
        in_specs.append(spec)

    mu = jnp.pad(shift_mu[layer], (0, 7 * half - shift_mu.shape[1]))[None]
    full2(mu)
    head_of_lane = np.arange(LANES) // RWKV_HEAD_DIM
    full2(jnp.asarray(head_of_lane[:, None] == head_of_lane[None, :], dtype=F32))
    d_lo, a_lo = w2.shape[1], a2.shape[1]
    w2p = jnp.pad(w2[layer], ((0, half - d_lo), (0, 0)))
    a2p = jnp.pad(a2[layer], ((d_lo, half - d_lo - a_lo), (0, 0)))
    g2p = jnp.pad(g2[layer], ((d_lo + a_lo, half - n_lora), (0, 0)))
    lrow(w0)
    full2(w2p.astype(BF16))
    lrow(a0)
    full2(a2p.astype(BF16))
    full2(g2p.astype(BF16))
    lrow(k_k)
    lrow(k_a)
    if has_vres:
        operands.append(v_first)
        in_specs.append(pl.BlockSpec((tm, width), lambda i: (i, 0)))
        lrow_layer = layer - 1
        for x in (v0,):
            a3 = x.reshape(x.shape[0], 1, width)
            operands.append(a3)
            in_specs.append(pl.BlockSpec((None, 1, width), lambda i: (lrow_layer, 0, 0)))
        full2(v1[layer - 1].astype(BF16))
        full2(v2[layer - 1].astype(BF16))
    tile = pl.BlockSpec((tm, width), lambda i: (i, 0))
    out_dtypes = [BF16, F32, BF16, BF16, BF16, BF16, BF16] + ([] if has_vres else [F32])
    r, lw, k, v, a, b, g, *rest = pl.pallas_call(
        functools.partial(_rwkv_prep_kernel, has_vres=has_vres, tiles_per_seq=seq // tm, width=width, half=half),
        grid=(m // tm,),
        in_specs=in_specs,
        out_specs=[tile] * len(out_dtypes),
        out_shape=[jax.ShapeDtypeStruct((m, width), dt) for dt in out_dtypes],
        compiler_params=_cparams(("parallel",)),
        name="rwkv_prep",
    )(*operands)
    if not has_vres:
        v_first = rest[0]

    n_pairs = width // LANES
    seq_spec = pl.BlockSpec((None, seq, LANES), lambda bi, p: (bi, 0, p))
    row_spec = pl.BlockSpec((None, 1, LANES), lambda bi, p: (layer, 0, p))
    n_chunks = seq // WKV_CHUNK
    N = RWKV_HEAD_DIM
    y = pl.pallas_call(
        functools.partial(_wkv_kernel, seq=seq, group=min(8, n_chunks)),
        grid=(bsz, n_pairs),
        in_specs=[seq_spec] * 7 + [row_spec] * 3,
        out_specs=seq_spec,
        out_shape=jax.ShapeDtypeStruct((bsz, seq, width), BF16),
        scratch_shapes=(
            [pltpu.VMEM((seq, LANES), F32)] * 2
            + [pltpu.VMEM((n_chunks, LANES, LANES), F32)] * 2
            + [pltpu.VMEM((n_chunks + 1, LANES, LANES), F32)]
            + [pltpu.VMEM((n_chunks, LANES, LANES), BF16)]
            + [pltpu.VMEM((n_chunks, LANES, LANES), F32)]
            + [pltpu.VMEM((n_chunks, LANES, LANES), BF16)] * 7
            + [pltpu.VMEM((n_chunks, 2 * LANES, LANES), BF16)] * 2
            + [pltpu.VMEM((n_chunks, LANES, 2 * LANES), BF16)]
            + [pltpu.VMEM((n_chunks, WKV_CHUNK, LANES), F32)]
            + [pltpu.VMEM((n_chunks, SUBLANES, LANES), F32)]
        ),
        compiler_params=_cparams(("parallel", "parallel")),
        name="wkv7",
    )(*[x.reshape(bsz, seq, width) for x in (r, lw, k, v, a, b, g)],
      ln_w.reshape(-1, 1, width), ln_b.reshape(-1, 1, width), r_k.reshape(-1, 1, width))
    return y, v_first


def _row_tiles(m):
    plan = dict(ffn_up=8192, in_proj=4096, ffn_down=2048, narrow=2048, rwkv_prep=256)
    return {k: min(v, m) for k, v in plan.items()}


def _trunk(x, ffn1_norm, ffn1_up, ffn1_down, mix_norm, w_in, s5_a_re, s5_a_im, s5_b_re, s5_b_im, s5_c_re, s5_c_im,
           s5_log_dt, s5_d, s5_w_glu, attn_q_gain, attn_k_gain, rwkv_shift_mu, rwkv_w0, rwkv_w2, rwkv_a0, rwkv_a2,
           rwkv_g2, rwkv_k_k, rwkv_k_a, rwkv_r_k, rwkv_ln_w, rwkv_ln_b, rwkv_v0, rwkv_v1, rwkv_v2,
           w_branch_s5, w_branch_attn, w_branch_rwkv, w_out, ffn2_norm, ffn2_up, ffn2_down):
    bsz, seq, d_model = x.shape
    depth = w_in.shape[0]
    m = bsz * seq
    s5_width = s5_d.shape[1]
    attn_width = len(DIL_PATTERNS) * ATTN_HEADS * ATTN_HEAD_DIM
    rwkv_width = rwkv_w0.shape[1]
    rwkv_part = rwkv_shift_mu.shape[1]
    q_off = s5_width
    p_off = s5_width + 3 * attn_width
    gate_off = p_off + rwkv_part
    tn_in = 512
    tiles = _row_tiles(m)
    w_in_t = jnp.swapaxes(w_in, 1, 2)
    n_front = -(-gate_off // tn_in) * tn_in
    rk3 = rwkv_r_k.reshape(depth, rwkv_width)

    xf = x.reshape(m, d_model)
    v_first = None
    xg, rinv = _norm_parts(xf, ffn1_norm, 0)
    for l in range(depth):
        xf, xg, rinv = _swiglu_ffn(xf, xg, rinv, ffn1_up, ffn1_down, l, tiles["ffn_up"], tiles["ffn_down"],
                                   (mix_norm, l))
        proj = _mm("in_proj", [xg], [(w_in_t, l, ("T", 0), 0)], [], _ep_plain, n_front, F32, tiles["in_proj"], tn_in,
                   row_scale=rinv)
        gates = _mm("in_proj_gates", [xg], [(w_in_t, l, ("T", gate_off), 0)], [], _ep_plain,
                    3 * d_model, BF16, tiles["in_proj"], tn_in, row_scale=rinv)
        proj3 = proj.reshape(bsz, seq, n_front)

        z = _s5_mixer_pre_glu(proj3, l, s5_a_re, s5_a_im, s5_b_re, s5_b_im, s5_c_re, s5_c_im, s5_log_dt, s5_d)
        z = z.reshape(m, s5_width)
        y_s5 = _mm("s5_glu", [z], [(s5_w_glu, l, 0, 0)], [(z, 0)], _ep_glu, s5_width, BF16, tiles["narrow"], tn_in)

        y_attn = _attention(proj3, q_off, l, attn_q_gain, attn_k_gain).reshape(m, -1)

        y_rwkv, v_first = _rwkv_mixer(proj, p_off, bsz, seq, l, v_first, rwkv_shift_mu, rwkv_w0, rwkv_w2, rwkv_a0,
                                      rwkv_a2, rwkv_g2, rwkv_k_k, rwkv_k_a, rk3, rwkv_ln_w, rwkv_ln_b,
                                      rwkv_v0, rwkv_v1, rwkv_v2, tiles["rwkv_prep"])
        y_rwkv = y_rwkv.reshape(m, rwkv_width)

        tn = tn_in
        gb = d_model // tn
        merged = _mm("branch_merge", [y_s5, y_attn, y_rwkv],
                     [(w_branch_s5, l, 0, 0), (w_branch_attn, l, 0, 1), (w_branch_rwkv, l, 0, 2)],
                     [(gates, 0), (gates, gb), (gates, 2 * gb)], _ep_gated_sum, d_model, BF16, tiles["narrow"], tn)
        xf, xg, rinv = _mm("mix_out", [merged], [(w_out, l, 0, 0)], [(xf, 0)], _ep_residual, d_model, F32,
                           tiles["narrow"], tn, norm_out=(ffn2_norm, l))
        next_norm = (ffn1_norm, l + 1) if l + 1 < depth else None
        res = _swiglu_ffn(xf, xg, rinv, ffn2_up, ffn2_down, l, tiles["ffn_up"], tiles["ffn_down"], next_norm)
        xf, xg, rinv = res if next_norm is not None else (res, None, None)
    return xf.reshape(bsz, seq, d_model)


def kernel(x, ffn1_norm, ffn1_up, ffn1_down, mix_norm, w_in, s5_a_re, s5_a_im, s5_b_re, s5_b_im, s5_c_re, s5_c_im, s5_log_dt, s5_d, s5_w_glu, attn_q_gain, attn_k_gain, rwkv_shift_mu, rwkv_w0, rwkv_w2, rwkv_a0, rwkv_a2, rwkv_g2, rwkv_k_k, rwkv_k_a, rwkv_r_k, rwkv_ln_w, rwkv_ln_b, rwkv_v0, rwkv_v1, rwkv_v2, w_branch_s5, w_branch_attn, w_branch_rwkv, w_out, ffn2_norm, ffn2_up, ffn2_down):
    return _trunk(x, ffn1_norm, ffn1_up, ffn1_down, mix_norm, w_in, s5_a_re, s5_a_im, s5_b_re, s5_b_im, s5_c_re,
                  s5_c_im, s5_log_dt, s5_d, s5_w_glu, attn_q_gain, attn_k_gain, rwkv_shift_mu, rwkv_w0, rwkv_w2,
                  rwkv_a0, rwkv_a2, rwkv_g2, rwkv_k_k, rwkv_k_a, rwkv_r_k, rwkv_ln_w, rwkv_ln_b, rwkv_v0, rwkv_v1,
                  rwkv_v2, w_branch_s5, w_branch_attn, w_branch_rwkv, w_out, ffn2_norm, ffn2_up, ffn2_down)
```

```python
import functools
import math

import numpy as np
import jax
import jax.numpy as jnp
from jax import lax
from jax.experimental import pallas as pl
from jax.experimental.pallas import tpu as pltpu

F32 = jnp.float32
BF16 = jnp.bfloat16

LANES = 128
SUBLANES = 8
VMEM_LIMIT_BYTES = 56 * 1024 * 1024

RMS_EPS = 1e-6
S5_GROUP = 16
S5_STATE = 64
S5_TBL_LEVEL = SUBLANES
S5_TBL_SEG = SUBLANES + 3
S5_N_TBL = SUBLANES + 4
ATTN_HEAD_DIM = 64
ATTN_HEADS = 8
DIL_PATTERNS = ((128, 1), (512, 4), (2048, 16))
ALIBI_MAX_BIAS = 8.0
ATTN_BLOCK = 128
RWKV_HEAD_DIM = 64
RWKV_GN_EPS = 64e-5
WKV_CHUNK = 64
NEG_BIG = -1e30


def _cparams(sem):
    return pltpu.CompilerParams(dimension_semantics=sem, vmem_limit_bytes=VMEM_LIMIT_BYTES)


def _dot(a, b, precision=None):
    return jnp.dot(a, b, preferred_element_type=F32, precision=precision)


def _dot_nt(a, b, precision=None):
    return lax.dot_general(a, b, (((1,), (1,)), ((), ())), preferred_element_type=F32, precision=precision)


def _dot_tn(a, b, precision=None):
    return lax.dot_general(a, b, (((0,), (0,)), ((), ())), preferred_element_type=F32, precision=precision)


def _sigmoid(x):
    return 1.0 / (1.0 + jnp.exp(-x))


def _layer_row(arr, layer):
    n = arr.shape[-1]
    a3 = arr.reshape(arr.shape[0], 1, n)
    return a3, pl.BlockSpec((None, 1, n), lambda *_: (layer, 0, 0))


def _norm_parts_kernel(x_ref, g_ref, xg_ref, rinv_ref):
    x = x_ref[...]
    xg_ref[...] = (x * g_ref[...]).astype(xg_ref.dtype)
    rinv = lax.rsqrt(jnp.mean(x * x, axis=-1, keepdims=True) + RMS_EPS)
    rinv_ref[...] = jnp.broadcast_to(rinv, rinv_ref.shape)


def _norm_parts(x, gain, layer):
    m, d = x.shape
    tm = min(m, 1024)
    g3, g_spec = _layer_row(gain, layer)
    return pl.pallas_call(
        _norm_parts_kernel,
        grid=(m // tm,),
        in_specs=[pl.BlockSpec((tm, d), lambda i: (i, 0)), g_spec],
        out_specs=[pl.BlockSpec((tm, d), lambda i: (i, 0)), pl.BlockSpec((tm, LANES), lambda i: (i, 0))],
        out_shape=[jax.ShapeDtypeStruct((m, d), BF16), jax.ShapeDtypeStruct((m, LANES), F32)],
        compiler_params=_cparams(("parallel",)),
        name="norm_parts",
    )(x, g3)


def _mm_kernel(*refs, n_lhs, lhs_needs_cast, rhs_lhs, rhs_parts, n_extra, epilogue, tn, row_chunk, has_row_scale,
               norm_dim):
    it = iter(refs)
    lhs_refs = [next(it) for _ in range(n_lhs)]
    rhs_refs = [[next(it) for _ in range(n_parts)] for n_parts, _ in rhs_parts]
    extra_refs = [next(it) for _ in range(n_extra)]
    scale_ref = next(it) if has_row_scale else None
    gain_ref = next(it) if norm_dim else None
    out_ref = next(it)
    xg_ref, rinv_ref = (next(it), next(it)) if norm_dim else (None, None)
    lhs_scr = {p: next(it) for p in range(n_lhs) if lhs_needs_cast[p]}
    w_scr = [next(it) for _ in range(n_lhs)]
    ss_scr = next(it) if norm_dim else None
    j = pl.program_id(1)

    @pl.when(j == 0)
    def _():
        for p, scr in lhs_scr.items():
            scr[...] = lhs_refs[p][...].astype(BF16)

    for p in range(n_lhs):
        mine = [r for r, q in enumerate(rhs_lhs) if q == p]
        for n, r in enumerate(mine):
            transposed = rhs_parts[r][1]
            parts = [t[...] for t in rhs_refs[r]]
            if transposed:
                w = (parts[0] if len(parts) == 1 else jnp.concatenate(parts, axis=0)).T
            else:
                w = parts[0] if len(parts) == 1 else jnp.concatenate(parts, axis=1)
            w_scr[p][:, n * tn:(n + 1) * tn] = w.astype(BF16)

    def lane_fold(x):
        acc = x[:, :LANES]
        for c in range(LANES, x.shape[1], LANES):
            acc = acc + x[:, c:c + LANES]
        return acc

    def chunk(c):
        rows = pl.ds(c * row_chunk, row_chunk)
        accs = [None] * len(rhs_lhs)
        for p in range(n_lhs):
            mine = [r for r, q in enumerate(rhs_lhs) if q == p]
            a = (lhs_scr[p] if lhs_needs_cast[p] else lhs_refs[p])[rows, :]
            acc = _dot(a, w_scr[p][...])
            if has_row_scale:
                rs = scale_ref[rows, :]
                acc = acc * jnp.concatenate([rs] * (acc.shape[1] // LANES), axis=1)
            for n, r in enumerate(mine):
                accs[r] = acc[:, n * tn:(n + 1) * tn]
        res = epilogue(accs, [e[rows, :] for e in extra_refs])
        out_ref[rows, :] = res.astype(out_ref.dtype)
        if norm_dim:
            xg_ref[rows, :] = (res * gain_ref[...]).astype(xg_ref.dtype)
            sq = lane_fold(res * res)
            ss_scr[rows, :] = jnp.where(j == 0, sq, ss_scr[rows, :] + sq)

    for c in range(out_ref.shape[0] // row_chunk):
        chunk(c)

    if norm_dim:
        @pl.when(j == pl.num_programs(1) - 1)
        def _():
            tot = jnp.sum(ss_scr[...], axis=-1, keepdims=True)
            rinv_ref[...] = jnp.broadcast_to(lax.rsqrt(tot * (1.0 / norm_dim) + RMS_EPS), rinv_ref.shape)


def _mm(name, lhs, rhs, extras, epilogue, n_out, out_dtype, tm, tn, row_chunk=1024, row_scale=None, norm_out=None):
    m = lhs[0].shape[0]
    tm = min(tm, m)
    row_chunk = min(row_chunk, tm)
    operands, in_specs, lhs_scratch, w_scratch = [], [], [], []
    lhs_needs_cast = []
    for p, a in enumerate(lhs):
        k = a.shape[1]
        operands.append(a)
        in_specs.append(pl.BlockSpec((tm, k), lambda i, j: (i, 0), pipeline_mode=pl.Buffered(1)))
        lhs_needs_cast.append(a.dtype != BF16)
        if a.dtype != BF16:
            lhs_scratch.append(pltpu.VMEM((tm, k), BF16))
        w_scratch.append(pltpu.VMEM((k, tn * sum(1 for r in rhs if r[3] == p)), BF16))
    rhs_parts = []
    for w, layer, off, _ in rhs:
        if isinstance(off, tuple):
            start = off[1]
            part = math.gcd(start, tn)
            assert part % SUBLANES == 0
            n_parts = tn // part
            rhs_parts.append((n_parts, True))
            for t in range(n_parts):
                operands.append(w)
                in_specs.append(pl.BlockSpec(
                    (None, part, w.shape[2]),
                    lambda i, j, b=start // part + t, s=n_parts, layer=layer: (layer, b + s * j, 0)))
        else:
            rhs_parts.append((1, False))
            operands.append(w)
            if layer is None:
                in_specs.append(pl.BlockSpec((w.shape[0], tn), lambda i, j, off=off: (0, off + j)))
            else:
                in_specs.append(pl.BlockSpec((None, w.shape[1], tn),
                                             lambda i, j, off=off, layer=layer: (layer, 0, off + j)))
    for e, off in extras:
        operands.append(e)
        in_specs.append(pl.BlockSpec((tm, tn), lambda i, j, off=off: (i, off + j)))
    if row_scale is not None:
        operands.append(row_scale)
        in_specs.append(pl.BlockSpec((tm, LANES), lambda i, j: (i, 0)))
    out_specs = [pl.BlockSpec((tm, tn), lambda i, j: (i, j))]
    out_shape = [jax.ShapeDtypeStruct((m, n_out), out_dtype)]
    norm_scratch = []
    if norm_out is not None:
        gain, gain_layer = norm_out
        operands.append(gain.reshape(gain.shape[0], 1, n_out))
        in_specs.append(pl.BlockSpec((None, 1, tn), lambda i, j: (gain_layer, 0, j)))
        out_specs += [pl.BlockSpec((tm, tn), lambda i, j: (i, j)), pl.BlockSpec((tm, LANES), lambda i, j: (i, 0))]
        out_shape += [jax.ShapeDtypeStruct((m, n_out), BF16), jax.ShapeDtypeStruct((m, LANES), F32)]
        norm_scratch = [pltpu.VMEM((tm, LANES), F32)]
    kern = functools.partial(
        _mm_kernel, n_lhs=len(lhs), lhs_needs_cast=tuple(lhs_needs_cast), rhs_lhs=tuple(r[3] for r in rhs),
        rhs_parts=tuple(rhs_parts), n_extra=len(extras), epilogue=epilogue, tn=tn, row_chunk=row_chunk,
        has_row_scale=row_scale is not None, norm_dim=n_out if norm_out is not None else 0)
    res = pl.pallas_call(
        kern,
        grid=(m // tm, n_out // tn),
        in_specs=in_specs,
        out_specs=out_specs,
        out_shape=out_shape,
        scratch_shapes=lhs_scratch + w_scratch + norm_scratch,
        compiler_params=_cparams(("parallel", "arbitrary")),
        name=name,
    )(*operands)
    return res if norm_out is not None else res[0]


def _ep_swiglu(accs, extras):
    a, b = accs
    return a * _sigmoid(a) * b


def _ep_half_residual(accs, extras):
    return extras[0] + 0.5 * accs[0]


def _ep_residual(accs, extras):
    return extras[0] + accs[0]


def _ep_plain(accs, extras):
    return accs[0]


def _ep_glu(accs, extras):
    z = extras[0]
    return z * _sigmoid(accs[0])


def _ep_gated_sum(accs, extras):
    out = _sigmoid(extras[0].astype(F32)) * accs[0]
    for g, a in zip(extras[1:], accs[1:]):
        out = out + _sigmoid(g.astype(F32)) * a
    return out


def _swiglu_ffn(x, xg, rinv, w_up, w_down, layer, tm_up, tm_down, next_norm):
    d_ff = w_down.shape[1]
    tn = LANES
    act = _mm("ffn_up", [xg], [(w_up, layer, 0, 0), (w_up, layer, d_ff // tn, 0)], [], _ep_swiglu,
              d_ff, BF16, tm_up, tn, row_scale=rinv)
    return _mm("ffn_down", [act], [(w_down, layer, 0, 0)], [(x, 0)], _ep_half_residual,
               x.shape[1], F32, tm_down, 256, row_chunk=512, norm_out=next_norm)


def _cmul(ar, ai, br, bi):
    return ar * br - ai * bi, ar * bi + ai * br


def _s5_prep_kernel(are_ref, aim_ref, ldt_ref, bre_ref, bim_ref, bbre_ref, bbim_ref, pw_ref):
    lam_re = jnp.minimum(are_ref[...], -1e-4)
    lam_im = aim_ref[...]
    dt = jnp.exp(ldt_ref[...])
    mag = jnp.exp(lam_re * dt)
    ab_re = mag * jnp.cos(lam_im * dt)
    ab_im = mag * jnp.sin(lam_im * dt)
    den = lam_re * lam_re + lam_im * lam_im
    num_re = ab_re - 1.0
    coef_re = (num_re * lam_re + ab_im * lam_im) / den
    coef_im = (ab_im * lam_re - num_re * lam_im) / den
    for h in range(bre_ref.shape[0]):
        br = bre_ref[h]
        bi = bim_ref[h]
        bbre_ref[h] = coef_re * br - coef_im * bi
        bbim_ref[h] = coef_re * bi + coef_im * br
    powers = [(ab_re, ab_im)]
    for _ in range(SUBLANES - 1):
        powers.append(_cmul(*powers[-1], ab_re, ab_im))
    for t, (pr, pi) in enumerate(powers):
        for r in range(SUBLANES):
            pw_ref[0, t, r] = pr
            pw_ref[1, t, r] = pi
    a8 = powers[-1]
    zero = jnp.zeros_like(ab_re)
    lvl_pow = a8
    for lvl in range(3):
        for r in range(SUBLANES):
            live = r >= (1 << lvl)
            pw_ref[0, S5_TBL_LEVEL + lvl, r] = lvl_pow[0] if live else zero
            pw_ref[1, S5_TBL_LEVEL + lvl, r] = lvl_pow[1] if live else zero
        lvl_pow = _cmul(*lvl_pow, *lvl_pow)
    pr, pi = a8
    for r in range(SUBLANES):
        pw_ref[0, S5_TBL_SEG, r] = pr
        pw_ref[1, S5_TBL_SEG, r] = pi
        pr, pi = _cmul(pr, pi, *a8)


def _s5_scan_kernel(u_ref, bre_ref, bim_ref, cre_ref, cim_ref, pw_ref, d_ref, z_ref, up_s, sre, sim, yp_s,
                    *, seq, lane_chunk):
    S = SUBLANES
    sb_rows = S * S
    n_sb = seq // sb_rows
    n_state = sre.shape[1]

    def transpose_superblocks(read_tile, write_tile):
        def body(sb, _):
            base = pl.multiple_of(sb * sb_rows, sb_rows)
            for t in range(S):
                write_tile(base, t, read_tile(base, t))
            return 0
        lax.fori_loop(0, n_sb, body, 0)

    def put_u(base, t, x):
        up_s[pl.ds(base + t * S, S), :] = x

    transpose_superblocks(lambda base, t: u_ref[pl.ds(base + t, S, stride=S), :], put_u)
    up = up_s[...].astype(BF16)
    sre[...] = _dot(up, bre_ref[...])
    sim[...] = _dot(up, bim_ref[...])

    def scan_superblock(sb, carry):
        base = pl.multiple_of(sb * sb_rows, sb_rows)
        new_carry = []
        for ch in range(n_state // lane_chunk):
            ln = pl.ds(ch * lane_chunk, lane_chunk)
            c_re, c_im = carry[2 * ch], carry[2 * ch + 1]
            ar, ai = pw_ref[0, 0, :, ln], pw_ref[1, 0, :, ln]
            sr, si = sre[pl.ds(base, S), ln], sim[pl.ds(base, S), ln]
            local = [(sr, si)]
            for t in range(1, S):
                rows = pl.ds(base + t * S, S)
                sr, si = sre[rows, ln] + ar * sr - ai * si, sim[rows, ln] + ar * si + ai * sr
                local.append((sr, si))
            er, ei = local[-1]
            for lvl in range(3):
                hr, hi = pw_ref[0, S5_TBL_LEVEL + lvl, :, ln], pw_ref[1, S5_TBL_LEVEL + lvl, :, ln]
                yr, yi = pltpu.roll(er, 1 << lvl, 0), pltpu.roll(ei, 1 << lvl, 0)
                er, ei = er + hr * yr - hi * yi, ei + hr * yi + hi * yr
            qr, qi = pw_ref[0, S5_TBL_SEG, :, ln], pw_ref[1, S5_TBL_SEG, :, ln]
            fr, fi = er + qr * c_re - qi * c_im, ei + qr * c_im + qi * c_re
            first = lax.broadcasted_iota(jnp.int32, fr.shape, 0) == 0
            in_r = jnp.where(first, c_re, pltpu.roll(fr, 1, 0))
            in_i = jnp.where(first, c_im, pltpu.roll(fi, 1, 0))
            for t in range(S):
                rows = pl.ds(base + t * S, S)
                pr, pi = pw_ref[0, t, :, ln], pw_ref[1, t, :, ln]
                sr, si = local[t]
                sre[rows, ln] = sr + pr * in_r - pi * in_i
                sim[rows, ln] = si + pr * in_i + pi * in_r
            new_carry.append(jnp.broadcast_to(fr[S - 1:S, :], fr.shape))
            new_carry.append(jnp.broadcast_to(fi[S - 1:S, :], fi.shape))
        return tuple(new_carry)

    zero = jnp.zeros((S, lane_chunk), F32)
    lax.fori_loop(0, n_sb, scan_superblock, (zero,) * (2 * (n_state // lane_chunk)))
    yp_s[...] = _dot(sre[...].astype(BF16), cre_ref[...]) - _dot(sim[...].astype(BF16), cim_ref[...])
    d = d_ref[...]

    def put_z(base, t, y):
        rows = pl.ds(base + t * S, S)
        z_ref[rows, :] = jax.nn.gelu(y + d * u_ref[rows, :])

    transpose_superblocks(lambda base, t: yp_s[pl.ds(base + t, S, stride=S), :], put_z)


def _s5_mixer_pre_glu(proj3, layer, a_re, a_im, b_re, b_im, c_re, c_im, log_dt, d_skip):
    bsz, seq, _ = proj3.shape
    n_grp, n_st = a_re.shape[1], a_re.shape[2]
    hc = b_re.shape[3]
    gpb = LANES // hc
    n_blk = n_grp // gpb
    nsl = gpb * n_st
    ldt = jnp.broadcast_to(log_dt[layer][:, None], (n_grp, n_st))
    b_re_t = jnp.transpose(b_re[layer], (2, 0, 1))
    b_im_t = jnp.transpose(b_im[layer], (2, 0, 1))
    bb_re, bb_im, pw = pl.pallas_call(
        _s5_prep_kernel,
        out_shape=(jax.ShapeDtypeStruct((hc, n_grp, n_st), F32), jax.ShapeDtypeStruct((hc, n_grp, n_st), F32),
                   jax.ShapeDtypeStruct((2, S5_N_TBL, SUBLANES, n_grp, n_st), F32)),
        name="s5_prep",
    )(a_re[layer], a_im[layer], ldt, b_re_t, b_im_t)
    eye = jnp.eye(gpb, dtype=F32)

    def blockdiag_in(bb):
        x = jnp.transpose(bb, (1, 0, 2)).reshape(n_blk, gpb, hc, n_st)
        x = x[:, :, :, None, :] * eye[None, :, None, :, None]
        return x.reshape(n_blk, gpb * hc, nsl)

    def blockdiag_out(c):
        x = jnp.transpose(c.reshape(n_blk, gpb, hc, n_st), (0, 1, 3, 2))
        x = x[:, :, :, None, :] * eye[None, :, None, :, None]
        return x.reshape(n_blk, nsl, gpb * hc)

    bmat_re, bmat_im = blockdiag_in(bb_re), blockdiag_in(bb_im)
    cmat_re, cmat_im = blockdiag_out(c_re[layer]), blockdiag_out(c_im[layer])
    bmat_re, bmat_im, cmat_re, cmat_im = (x.astype(BF16) for x in (bmat_re, bmat_im, cmat_re, cmat_im))
    pw_b = pw.reshape(2, S5_N_TBL, SUBLANES, n_blk, nsl).transpose(3, 0, 1, 2, 4)
    d3, d_spec = d_skip.reshape(d_skip.shape[0], 1, -1), pl.BlockSpec((None, 1, LANES), lambda b, g: (layer, 0, g))
    return pl.pallas_call(
        functools.partial(_s5_scan_kernel, seq=seq, lane_chunk=2 * LANES),
        grid=(bsz, n_blk),
        in_specs=[
            pl.BlockSpec((None, seq, LANES), lambda b, g: (b, 0, g)),
            pl.BlockSpec((None, LANES, nsl), lambda b, g: (g, 0, 0)),
            pl.BlockSpec((None, LANES, nsl), lambda b, g: (g, 0, 0)),
            pl.BlockSpec((None, nsl, LANES), lambda b, g: (g, 0, 0)),
            pl.BlockSpec((None, nsl, LANES), lambda b, g: (g, 0, 0)),
            pl.BlockSpec((None, 2, S5_N_TBL, SUBLANES, nsl), lambda b, g: (g, 0, 0, 0, 0)),
            d_spec,
        ],
        out_specs=pl.BlockSpec((None, seq, LANES), lambda b, g: (b, 0, g)),
        out_shape=jax.ShapeDtypeStruct((bsz, seq, n_grp * hc), F32),
        scratch_shapes=[pltpu.VMEM((seq, LANES), F32), pltpu.VMEM((seq, nsl), F32), pltpu.VMEM((seq, nsl), F32),
                        pltpu.VMEM((seq, LANES), F32)],
        compiler_params=_cparams(("parallel", "parallel")),
        name="s5_scan",
    )(proj3, bmat_re, bmat_im, cmat_re, cmat_im, pw_b, d3)


def _attn_kernel(slopes_ref, qg_ref, kg_ref, *refs, seq):
    qkv = refs[:9]
    out_ref = refs[9]
    qn_s, kn_s, qm, km, vm, base, s_s, p_s, sc_s, o_seq, l_seq, o_all, l_all = refs[10:]
    pair = pl.program_id(1)
    hd = ATTN_HEAD_DIM
    blk = ATTN_BLOCK
    lane = lax.broadcasted_iota(jnp.int32, (1, LANES), 1)
    head_mask = [(lane < hd).astype(F32), (lane >= hd).astype(F32)]
    qg = qg_ref[...]
    kg = kg_ref[...]

    ri = lax.broadcasted_iota(jnp.int32, (LANES, LANES), 0) // hd
    ci = lax.broadcasted_iota(jnp.int32, (LANES, LANES), 1) // hd
    head_ones = jnp.where(ri == ci, 1.0, 0.0).astype(BF16)

    def qk_norm(x, gain):
        ms = _dot((x * x).astype(BF16), head_ones) * (1.0 / hd)
        return x * lax.rsqrt(ms + RMS_EPS) * gain

    def block_rows(bi):
        return pl.ds(bi * blk, blk) if isinstance(bi, int) else pl.ds(pl.multiple_of(bi * blk, blk), blk)

    def key_rows(bi, first):
        if first:
            start, size = bi * blk, blk
        else:
            start, size = (bi - 1) * blk, 2 * blk
        return pl.ds(start, size) if isinstance(bi, int) else pl.ds(pl.multiple_of(start, blk), size)

    for g, (window, dil) in enumerate(DIL_PATTERNS):
        q_ref, k_ref, v_ref = qkv[3 * g:3 * g + 3]
        n = seq // dil
        n_blocks = n // blk
        n_back = window // dil
        assert n_back == blk and n % blk == 0
        rows = lax.broadcasted_iota(jnp.int32, (blk, 2 * blk), 0)
        cols = lax.broadcasted_iota(jnp.int32, (blk, 2 * blk), 1)
        delta = rows + blk - cols
        valid = (delta >= 0) & (delta <= n_back)
        for h in range(2):
            slope = slopes_ref[g * ATTN_HEADS + 2 * pair + h]
            base[h] = jnp.where(valid, (-slope * dil) * delta.astype(F32), NEG_BIG)

        if dil == 1:
            qn = qk_norm(q_ref[...], qg)
            km[...] = qk_norm(k_ref[...], kg).astype(BF16)
            v = v_ref[...]
            for h in range(2):
                qm[h] = (qn * head_mask[h]).astype(BF16)
                vm[h] = (v * head_mask[h]).astype(BF16)
        else:
            qn_s[...] = qk_norm(q_ref[...], qg)
            kn_s[...] = qk_norm(k_ref[...], kg)

            def regroup(i, _):
                for j in range(2):
                    r = 2 * i + j
                    dst = pl.ds(pl.multiple_of(r * n, blk), n)
                    q = qn_s[pl.ds(r, n, stride=dil), :]
                    v = v_ref[pl.ds(r, n, stride=dil), :]
                    km[dst, :] = kn_s[pl.ds(r, n, stride=dil), :].astype(BF16)
                    for h in range(2):
                        qm[h, dst, :] = (q * head_mask[h]).astype(BF16)
                        vm[h, dst, :] = (v * head_mask[h]).astype(BF16)
                return 0
            lax.fori_loop(0, dil // 2, regroup, 0)

        def scores(bi, first):
            for h in range(2):
                bias = base[h, :, blk:] if first else base[h]
                s = _dot_nt(qm[h, block_rows(bi), :], km[key_rows(bi, first), :]) * (hd ** -0.5) + bias
                if first:
                    s_s[bi, h, :, pl.ds(0, blk)] = s
                else:
                    s_s[bi, h] = s

        def softmax(bi, first):
            row_max = jnp.zeros((blk, LANES), F32)
            for h in range(2):
                s = s_s[bi, h, :, pl.ds(0, blk)] if first else s_s[bi, h]
                m = jnp.max(s, axis=-1, keepdims=True)
                p = jnp.exp(s - m)
                if first:
                    p_s[bi, h, :, pl.ds(0, blk)] = p.astype(BF16)
                else:
                    p_s[bi, h] = p.astype(BF16)
                row_max = row_max + head_mask[h] * m
            sc_s[bi] = row_max

        def values(bi, first):
            o = jnp.zeros((blk, LANES), F32)
            lse = sc_s[bi]
            for h in range(2):
                p = p_s[bi, h, :, pl.ds(0, blk)] if first else p_s[bi, h]
                l = _dot(p, jnp.ones((p.shape[1], LANES), BF16))
                o = o + _dot(p, vm[h, key_rows(bi, first), :]) * (1.0 / l)
                lse = lse + head_mask[h] * jnp.log(l)
            o_seq[block_rows(bi), :] = o
            l_seq[block_rows(bi), :] = lse

        def all_blocks(stage, many):
            if n_blocks == 1:
                per = 8 if many else 4

                def body(i, _):
                    for j in range(per):
                        stage(i * per + j, True)
                    return 0
                lax.fori_loop(0, dil // per, body, 0)
                return
            for r in range(dil):
                stage(r * n_blocks, True)
            per = next(p for p in ((5, 3, 1) if many else (3, 1)) if (n_blocks - 1) % p == 0)
            groups_per_class = (n_blocks - 1) // per

            def body(i, _):
                cls, grp = i // groups_per_class, i % groups_per_class
                for j in range(per):
                    stage(cls * n_blocks + 1 + grp * per + j, False)
                return 0
            lax.fori_loop(0, dil * groups_per_class, body, 0)

        all_blocks(scores, True)
        all_blocks(softmax, False)
        all_blocks(values, True)

        if dil == 1:
            o_all[g] = o_seq[...]
            l_all[g] = l_seq[...]
        else:
            def scatter(i, _):
                for j in range(2):
                    r = 2 * i + j
                    src = pl.ds(pl.multiple_of(r * n, blk), n)
                    o_all[g, pl.ds(r, n, stride=dil), :] = o_seq[src, :]
                    l_all[g, pl.ds(r, n, stride=dil), :] = l_seq[src, :]
                return 0
            lax.fori_loop(0, dil // 2, scatter, 0)

    l0, l1, l2 = l_all[0], l_all[1], l_all[2]
    mx = jnp.maximum(jnp.maximum(l0, l1), l2)
    w0, w1, w2 = jnp.exp(l0 - mx), jnp.exp(l1 - mx), jnp.exp(l2 - mx)
    o = (w0 * o_all[0] + w1 * o_all[1] + w2 * o_all[2]) / (w0 + w1 + w2)
    out_ref[...] = o.astype(out_ref.dtype)


def _attention(proj3, q_off, layer, q_gain, k_gain):
    bsz, seq, _ = proj3.shape
    n_dil = len(DIL_PATTERNS)
    width = n_dil * ATTN_HEADS * ATTN_HEAD_DIM
    n_pairs = ATTN_HEADS * ATTN_HEAD_DIM // LANES
    n_heads = n_dil * ATTN_HEADS
    n_blk = seq // ATTN_BLOCK
    slopes = jnp.asarray(2.0 ** (-ALIBI_MAX_BIAS * (np.arange(n_heads) + 1) / n_heads), dtype=F32)
    qg = jnp.tile(q_gain[layer], 2)[None]
    kg = jnp.tile(k_gain[layer], 2)[None]
    in_specs = [pl.BlockSpec(memory_space=pltpu.SMEM),
                pl.BlockSpec((1, LANES), lambda b, p: (0, 0)), pl.BlockSpec((1, LANES), lambda b, p: (0, 0))]
    operands = [slopes, qg, kg]
    for g in range(n_dil):
        for which in range(3):
            cb = (q_off + which * width + g * ATTN_HEADS * ATTN_HEAD_DIM) // LANES
            in_specs.append(pl.BlockSpec((None, seq, LANES), lambda b, p, cb=cb: (b, 0, cb + p)))
            operands.append(proj3)
    return pl.pallas_call(
        functools.partial(_attn_kernel, seq=seq),
        grid=(bsz, n_pairs),
        in_specs=in_specs,
        out_specs=pl.BlockSpec((None, seq, LANES), lambda b, p: (b, 0, p)),
        out_shape=jax.ShapeDtypeStruct((bsz, seq, ATTN_HEADS * ATTN_HEAD_DIM), BF16),
        scratch_shapes=[
            pltpu.VMEM((seq, LANES), F32), pltpu.VMEM((seq, LANES), F32),
            pltpu.VMEM((2, seq, LANES), BF16),
            pltpu.VMEM((seq, LANES), BF16), pltpu.VMEM((2, seq, LANES), BF16),
            pltpu.VMEM((2, ATTN_BLOCK, 2 * ATTN_BLOCK), F32),
            pltpu.VMEM((n_blk, 2, ATTN_BLOCK, 2 * ATTN_BLOCK), F32),
            pltpu.VMEM((n_blk, 2, ATTN_BLOCK, 2 * ATTN_BLOCK), BF16),
            pltpu.VMEM((n_blk, ATTN_BLOCK, LANES), F32),
            pltpu.VMEM((seq, LANES), F32), pltpu.VMEM((seq, LANES), F32),
            pltpu.VMEM((n_dil, seq, LANES), F32), pltpu.VMEM((n_dil, seq, LANES), F32),
        ],
        compiler_params=_cparams(("parallel", "parallel")),
        name="dilated_attn",
    )(*operands)


def _split_bf16(x):
    hi = x.astype(BF16)
    return hi, (x - hi.astype(F32)).astype(BF16)


def _dot_x3(a, b):
    ah, al = _split_bf16(a)
    bh, bl = _split_bf16(b)
    return _dot(ah, bh) + (_dot(ah, bl) + _dot(al, bh))


def _head_sums(x, ones_bd):
    parts = []
    for c in range(0, x.shape[1], LANES):
        hi, lo = _split_bf16(x[:, c:c + LANES])
        parts.append(_dot(hi, ones_bd) + _dot(lo, ones_bd))
    return jnp.concatenate(parts, axis=1)


def _rwkv_prep_kernel(*refs, has_vres, tiles_per_seq, width, half):
    it = iter(refs)
    cur = [next(it) for _ in range(7)]
    prev = [next(it) for _ in range(7)]
    mu_ref, ones_ref, w0_ref, w2_ref, a0_ref, a2_ref, g2_ref, kk_ref, ka_ref = [next(it) for _ in range(9)]
    if has_vres:
        vfirst_ref, v0_ref, v1_ref, v2_ref = [next(it) for _ in range(4)]
    r_out, lw_out, k_out, v_out, a_out, b_out, g_out = [next(it) for _ in range(7)]
    vfirst_out = None if has_vres else next(it)

    tm = cur[0].shape[0]
    first_of_seq = (pl.program_id(0) % tiles_per_seq) == 0
    row = lax.broadcasted_iota(jnp.int32, (tm, half), 0)

    def shifted(c):
        x = cur[c][...]
        last_prev = jnp.where(first_of_seq, 0.0, prev[c][SUBLANES - 1:SUBLANES, :])
        x_prev = jnp.where(row == 0, last_prev, pltpu.roll(x, 1, 0))
        mu = mu_ref[:, c * half:(c + 1) * half]
        return x + (x_prev - x) * mu

    sh = [shifted(c) for c in range(7)]
    r = jnp.concatenate(sh[0:2], axis=1)
    k = jnp.concatenate(sh[2:4], axis=1)
    v = jnp.concatenate(sh[4:6], axis=1)
    lora = sh[6]
    w_log = -jax.nn.softplus(-(w0_ref[...] + _dot(jnp.tanh(lora).astype(BF16), w2_ref[...]))) - 0.5
    a = _sigmoid(a0_ref[...] + _dot(lora.astype(BF16), a2_ref[...]))
    g = _dot(_sigmoid(lora).astype(BF16), g2_ref[...])
    if has_vres:
        low = _dot(v.astype(BF16), v1_ref[...])
        mix = _sigmoid(v0_ref[...] + _dot(low.astype(BF16), v2_ref[...]))
        v = v + (vfirst_ref[...] - v) * mix
    ones_bd = ones_ref[...]
    kk = k * kk_ref[...]
    kk = kk * lax.rsqrt(jnp.maximum(_head_sums(kk * kk, ones_bd), 1e-24))
    k = k * (1.0 + (a - 1.0) * ka_ref[...])
    r_out[...] = r.astype(r_out.dtype)
    lw_out[...] = -jnp.exp(w_log)
    k_out[...] = k.astype(k_out.dtype)
    v_out[...] = v.astype(v_out.dtype)
    a_out[...] = (-kk).astype(a_out.dtype)
    b_out[...] = (kk * a).astype(b_out.dtype)
    g_out[...] = g.astype(g_out.dtype)
    if vfirst_out is not None:
        vfirst_out[...] = v


def _wkv_kernel(r_ref, lw_ref, k_ref, v_ref, a_ref, b_ref, g_ref, lnw_ref, lnb_ref, rk_ref, out_ref,
                qp_s, yi_s, m_s, n_s, st_s, pw_s, tinv_s, rb_s, ast_s, vst_s, be_s, ke_s, rst_s, av_s,
                akrk_s, bk_s, wu_s, cum_s, cend_s,
                *, seq, group):
    L = WKV_CHUNK
    N = RWKV_HEAD_DIM
    P = 2 * N
    assert P == LANES and L == N
    n_chunks = seq // L
    ti = lax.broadcasted_iota(jnp.int32, (L, L), 0)
    si = lax.broadcasted_iota(jnp.int32, (L, L), 1)
    tri_incl = jnp.clip((ti - si).astype(F32) + 1.0, 0.0, 1.0).astype(BF16)
    ri = lax.broadcasted_iota(jnp.int32, (P, P), 0)
    ci = lax.broadcasted_iota(jnp.int32, (P, P), 1)
    same_head = (ri // L) == (ci // N)
    strict = same_head & ((ci % L) < (ri % L))
    incl = same_head & ((ci % L) <= (ri % L))
    eye = ri == ci
    lane = lax.broadcasted_iota(jnp.int32, (1, P), 1)
    head_mask = [(lane < N).astype(F32), (lane >= N).astype(F32)]

    def stack(x):
        return jnp.concatenate([x * head_mask[0], x * head_mask[1]], axis=0)

    def dup(x):
        return jnp.concatenate([x, x], axis=0)

    def fold(x):
        return x[:L] + x[L:]

    def for_chunk_groups(fn, per_iter):
        def body(i, _):
            for j in range(per_iter):
                fn(i * per_iter + j)
            return 0
        lax.fori_loop(0, n_chunks // per_iter, body, 0)

    def chunk_rows(c):
        return pl.ds(c * L, L) if isinstance(c, int) else pl.ds(pl.multiple_of(c * L, L), L)

    def stage_cumdecay(c):
        lw_hi, lw_lo = _split_bf16(lw_ref[chunk_rows(c), :])
        cum_s[c] = _dot(tri_incl, lw_hi) + _dot(tri_incl, lw_lo)

    def stage_scale(c):
        rows = chunk_rows(c)
        cum = cum_s[c]
        cum_end = cum[L - 1:L, :]
        ast_s[c] = stack(a_ref[rows, :] * jnp.exp(cum - lw_ref[rows, :])).astype(BF16)
        r_t = r_ref[rows, :] * jnp.exp(cum)
        qp_s[rows, :] = r_t
        rst_s[c] = stack(r_t).astype(BF16)
        g_inv = jnp.exp(-cum)
        bk_s[c] = jnp.concatenate([dup(b_ref[rows, :] * g_inv), dup(k_ref[rows, :] * g_inv)], axis=0).astype(BF16)
        g_end = jnp.exp(cum_end - cum)
        be_s[c] = stack(b_ref[rows, :] * g_end).astype(BF16)
        ke_s[c] = stack(k_ref[rows, :] * g_end).astype(BF16)
        vst_s[c] = stack(v_ref[rows, :]).astype(BF16)
        cend_s[c] = jnp.broadcast_to(cum_end, (SUBLANES, P))

    def stage_scores(c):
        ar_st = jnp.concatenate([ast_s[c], rst_s[c]], axis=0)
        ar_bk = _dot_nt(ar_st, bk_s[c])
        a_ab = jnp.where(strict, ar_bk[:P, :P], 0.0)
        pw_s[c] = a_ab.astype(BF16)
        tinv_s[c] = jnp.where(eye, 1.0, a_ab)
        rb_s[c] = jnp.where(incl, ar_bk[P:, :P], 0.0).astype(BF16)
        akrk_s[c] = jnp.concatenate([jnp.where(strict, ar_bk[:P, P:], 0.0),
                                     jnp.where(incl, ar_bk[P:, P:], 0.0)], axis=0).astype(BF16)

    def stage_inverse_square(c):
        pw = pw_s[c]
        pw_s[c] = _dot(pw, pw).astype(BF16)

    def stage_inverse_level(c):
        pw = pw_s[c]
        tinv = tinv_s[c]
        both = _dot(jnp.concatenate([pw, tinv.astype(BF16)], axis=0), pw)
        pw_s[c] = both[:P].astype(BF16)
        tinv_s[c] = tinv + both[P:]

    def stage_inverse_last(c):
        tinv = tinv_s[c]
        tinv_s[c] = tinv + _dot(tinv.astype(BF16), pw_s[c])

    def stage_av(c):
        both = _dot(akrk_s[c], vst_s[c])
        av_s[c] = both[:P].astype(BF16)
        yi_s[chunk_rows(c), :] = fold(both[P:])

    def stage_wu(c):
        rhs = jnp.concatenate([ast_s[c], av_s[c]], axis=1)
        wu_s[c] = _dot(tinv_s[c].astype(BF16), rhs).astype(BF16)

    def stage_summary(c):
        rows = chunk_rows(c)
        wu = wu_s[c]
        v_st = vst_s[c]
        rb_wu = _dot(rb_s[c], wu)
        be_wu = _dot_tn(be_s[c], wu)
        qp_s[rows, :] = qp_s[rows, :] + fold(rb_wu[:, :P])
        yi_s[rows, :] = yi_s[rows, :] + fold(rb_wu[:, P:])
        d_end = jnp.where(eye, jnp.broadcast_to(jnp.exp(cend_s[c][0:1, :]), (P, P)), 0.0)
        m_s[c] = d_end + be_wu[:, :P]
        n_s[c] = be_wu[:, P:] + _dot_tn(ke_s[c], v_st)

    for_chunk_groups(stage_cumdecay, group)
    for_chunk_groups(stage_scale, min(group, 4))
    for_chunk_groups(stage_scores, group)

    def inverse_levels(_, carry):
        for_chunk_groups(stage_inverse_level, inv_group)
        return carry

    n_powers = (L - 1).bit_length()
    inv_group = min(2 * group, n_chunks)
    for_chunk_groups(stage_inverse_square, inv_group)
    lax.fori_loop(0, n_powers - 2, inverse_levels, 0)
    for_chunk_groups(stage_inverse_last, inv_group)
    for_chunk_groups(stage_av, group)
    for_chunk_groups(stage_wu, group)

    def head_stat(x):
        s0 = jnp.sum(x * head_mask[0], axis=-1, keepdims=True)
        s1 = jnp.sum(x * head_mask[1], axis=-1, keepdims=True)
        return s0 * head_mask[0] + s1 * head_mask[1]

    def emit(c):
        rows = chunk_rows(c)
        y = _dot(qp_s[rows, :].astype(BF16), st_s[c].astype(BF16)) + yi_s[rows, :]
        yc = y - head_stat(y) * (1.0 / N)
        var = head_stat(yc * yc) * (1.0 / N)
        y = yc * lax.rsqrt(var + RWKV_GN_EPS) * lnw_ref[...] + lnb_ref[...]
        bonus = head_stat(r_ref[rows, :].astype(F32) * k_ref[rows, :] * rk_ref[...])
        out_ref[rows, :] = ((y + bonus * v_ref[rows, :]) * g_ref[rows, :]).astype(out_ref.dtype)

    per = max(1, min(4, n_chunks // 2))
    n_groups = n_chunks // per

    def chain(g):
        for j in range(per):
            c = g * per + j
            st_s[c + 1] = _dot_x3(m_s[c], st_s[c]) + n_s[c]

    def emit_group(g):
        for j in range(per):
            emit(g * per + j)

    def summarise_group(g):
        for j in range(per):
            stage_summary(g * per + j)

    def steady(g, _):
        emit_group(g - 1)
        chain(g)
        summarise_group(g + 1)
        return 0

    st_s[0] = jnp.zeros((P, P), F32)
    summarise_group(0)
    chain(0)
    summarise_group(1)
    lax.fori_loop(1, n_groups - 1, steady, 0)
    emit_group(n_groups - 2)
    chain(n_groups - 1)
    emit_group(n_groups - 1)


def _rwkv_mixer(proj, p_off, bsz, seq, layer, v_first, shift_mu, w0, w2, a0, a2, g2, k_k, k_a, r_k, ln_w, ln_b,
                v0, v1, v2, tm):
    m = proj.shape[0]
    width = w0.shape[1]
    half = width // 2
    n_lora = w2.shape[1] + a2.shape[1] + g2.shape[1]
    assert n_lora <= half and p_off % half == 0 and 3 * width % half == 0
    has_vres = layer > 0
    cb0 = p_off // half
    rows8 = tm // SUBLANES
    operands, in_specs = [], []
    for c in range(7):
        operands.append(proj)
        in_specs.append(pl.BlockSpec((tm, half), lambda i, c=c: (i, cb0 + c)))
    for c in range(7):
        operands.append(proj)
        in_specs.append(pl.BlockSpec((SUBLANES, half), lambda i, c=c: (jnp.maximum(i * rows8 - 1, 0), cb0 + c)))

    def full2(x):
        operands.append(x)
        in_specs.append(pl.BlockSpec(x.shape, lambda i: (0, 0)))

    def lrow(x):
        a3, spec = _layer_row(x, layer)
        operands.append(a3)
        in_specs.append(spec)

    mu = jnp.pad(shift_mu[layer], (0, 7 * half - shift_mu.shape[1]))[None]
    full2(mu)
    head_of_lane = np.arange(LANES) // RWKV_HEAD_DIM
    full2(jnp.asarray(head_of_lane[:, None] == head_of_lane[None, :], dtype=F32))
    d_lo, a_lo = w2.shape[1], a2.shape[1]
    w2p = jnp.pad(w2[layer], ((0, half - d_lo), (0, 0)))
    a2p = jnp.pad(a2[layer], ((d_lo, half - d_lo - a_lo), (0, 0)))
    g2p = jnp.pad(g2[layer], ((d_lo + a_lo, half - n_lora), (0, 0)))
    lrow(w0)
    full2(w2p.astype(BF16))
    lrow(a0)
    full2(a2p.astype(BF16))
    full2(g2p.astype(BF16))
    lrow(k_k)
    lrow(k_a)
    if has_vres:
        operands.append(v_first)
        in_specs.append(pl.BlockSpec((tm, width), lambda i: (i, 0)))
        lrow_layer = layer - 1
        for x in (v0,):
            a3 = x.reshape(x.shape[0], 1, width)
            operands.append(a3)
            in_specs.append(pl.BlockSpec((None, 1, width), lambda i: (lrow_layer, 0, 0)))
        full2(v1[layer - 1].astype(BF16))
        full2(v2[layer - 1].astype(BF16))
    tile = pl.BlockSpec((tm, width), lambda i: (i, 0))
    out_dtypes = [BF16, F32, BF16, BF16, BF16, BF16, BF16] + ([] if has_vres else [F32])
    r, lw, k, v, a, b, g, *rest = pl.pallas_call(
        functools.partial(_rwkv_prep_kernel, has_vres=has_vres, tiles_per_seq=seq // tm, width=width, half=half),
        grid=(m // tm,),
        in_specs=in_specs,
        out_specs=[tile] * len(out_dtypes),
        out_shape=[jax.ShapeDtypeStruct((m, width), dt) for dt in out_dtypes],
        compiler_params=_cparams(("parallel",)),
        name="rwkv_prep",
    )(*operands)
    if not has_vres:
        v_first = rest[0]

    n_pairs = width // LANES
    seq_spec = pl.BlockSpec((None, seq, LANES), lambda bi, p: (bi, 0, p))
    row_spec = pl.BlockSpec((None, 1, LANES), lambda bi, p: (layer, 0, p))
    n_chunks = seq // WKV_CHUNK
    N = RWKV_HEAD_DIM
    y = pl.pallas_call(
        functools.partial(_wkv_kernel, seq=seq, group=min(16, n_chunks)),
        grid=(bsz, n_pairs),
        in_specs=[seq_spec] * 7 + [row_spec] * 3,
        out_specs=seq_spec,
        out_shape=jax.ShapeDtypeStruct((bsz, seq, width), BF16),
        scratch_shapes=(
            [pltpu.VMEM((seq, LANES), F32)] * 2
            + [pltpu.VMEM((n_chunks, LANES, LANES), F32)] * 2
            + [pltpu.VMEM((n_chunks + 1, LANES, LANES), F32)]
            + [pltpu.VMEM((n_chunks, LANES, LANES), BF16)]
            + [pltpu.VMEM((n_chunks, LANES, LANES), F32)]
            + [pltpu.VMEM((n_chunks, LANES, LANES), BF16)] * 7
            + [pltpu.VMEM((n_chunks, 2 * LANES, LANES), BF16)] * 2
            + [pltpu.VMEM((n_chunks, LANES, 2 * LANES), BF16)]
            + [pltpu.VMEM((n_chunks, WKV_CHUNK, LANES), F32)]
            + [pltpu.VMEM((n_chunks, SUBLANES, LANES), F32)]
        ),
        compiler_params=_cparams(("parallel", "parallel")),
        name="wkv7",
    )(*[x.reshape(bsz, seq, width) for x in (r, lw, k, v, a, b, g)],
      ln_w.reshape(-1, 1, width), ln_b.reshape(-1, 1, width), r_k.reshape(-1, 1, width))
    return y, v_first


def _row_tiles(m):
    plan = dict(ffn_up=8192, in_proj=4096, ffn_down=2048, narrow=2048, rwkv_prep=256)
    return {k: min(v, m) for k, v in plan.items()}


def _trunk(x, ffn1_norm, ffn1_up, ffn1_down, mix_norm, w_in, s5_a_re, s5_a_im, s5_b_re, s5_b_im, s5_c_re, s5_c_im,
           s5_log_dt, s5_d, s5_w_glu, attn_q_gain, attn_k_gain, rwkv_shift_mu, rwkv_w0, rwkv_w2, rwkv_a0, rwkv_a2,
           rwkv_g2, rwkv_k_k, rwkv_k_a, rwkv_r_k, rwkv_ln_w, rwkv_ln_b, rwkv_v0, rwkv_v1, rwkv_v2,
           w_branch_s5, w_branch_attn, w_branch_rwkv, w_out, ffn2_norm, ffn2_up, ffn2_down):
    bsz, seq, d_model = x.shape
    depth = w_in.shape[0]
    m = bsz * seq
    s5_width = s5_d.shape[1]
    attn_width = len(DIL_PATTERNS) * ATTN_HEADS * ATTN_HEAD_DIM
    rwkv_width = rwkv_w0.shape[1]
    rwkv_part = rwkv_shift_mu.shape[1]
    q_off = s5_width
    p_off = s5_width + 3 * attn_width
    gate_off = p_off + rwkv_part
    tn_in = 512
    tiles = _row_tiles(m)
    w_in_t = jnp.swapaxes(w_in, 1, 2)
    n_front = -(-gate_off // tn_in) * tn_in
    rk3 = rwkv_r_k.reshape(depth, rwkv_width)

    xf = x.reshape(m, d_model)
    v_first = None
    xg, rinv = _norm_parts(xf, ffn1_norm, 0)
    for l in range(depth):
        xf, xg, rinv = _swiglu_ffn(xf, xg, rinv, ffn1_up, ffn1_down, l, tiles["ffn_up"], tiles["ffn_down"],
                                   (mix_norm, l))
        proj = _mm("in_proj", [xg], [(w_in_t, l, ("T", 0), 0)], [], _ep_plain, n_front, F32, tiles["in_proj"], tn_in,
                   row_scale=rinv)
        gates = _mm("in_proj_gates", [xg], [(w_in_t, l, ("T", gate_off), 0)], [], _ep_plain,
                    3 * d_model, BF16, tiles["in_proj"], tn_in, row_scale=rinv)
        proj3 = proj.reshape(bsz, seq, n_front)

        z = _s5_mixer_pre_glu(proj3, l, s5_a_re, s5_a_im, s5_b_re, s5_b_im, s5_c_re, s5_c_im, s5_log_dt, s5_d)
        z = z.reshape(m, s5_width)
        y_s5 = _mm("s5_glu", [z], [(s5_w_glu, l, 0, 0)], [(z, 0)], _ep_glu, s5_width, BF16, tiles["narrow"], tn_in)

        y_attn = _attention(proj3, q_off, l, attn_q_gain, attn_k_gain).reshape(m, -1)

        y_rwkv, v_first = _rwkv_mixer(proj, p_off, bsz, seq, l, v_first, rwkv_shift_mu, rwkv_w0, rwkv_w2, rwkv_a0,
                                      rwkv_a2, rwkv_g2, rwkv_k_k, rwkv_k_a, rk3, rwkv_ln_w, rwkv_ln_b,
                                      rwkv_v0, rwkv_v1, rwkv_v2, tiles["rwkv_prep"])
        y_rwkv = y_rwkv.reshape(m, rwkv_width)

        tn = 256
        gb = d_model // tn
        merged = _mm("branch_merge", [y_s5, y_attn, y_rwkv],
                     [(w_branch_s5, l, 0, 0), (w_branch_attn, l, 0, 1), (w_branch_rwkv, l, 0, 2)],
                     [(gates, 0), (gates, gb), (gates, 2 * gb)], _ep_gated_sum, d_model, BF16, tiles["narrow"], tn)
        xf, xg, rinv = _mm("mix_out", [merged], [(w_out, l, 0, 0)], [(xf, 0)], _ep_residual, d_model, F32,
                           tiles["narrow"], tn, norm_out=(ffn2_norm, l))
        next_norm = (ffn1_norm, l + 1) if l + 1 < depth else None
        res = _swiglu_ffn(xf, xg, rinv, ffn2_up, ffn2_down, l, tiles["ffn_up"], tiles["ffn_down"], next_norm)
        xf, xg, rinv = res if next_norm is not None else (res, None, None)
    return xf.reshape(bsz, seq, d_model)


def kernel(x, ffn1_norm, ffn1_up, ffn1_down, mix_norm, w_in, s5_a_re, s5_a_im, s5_b_re, s5_b_im, s5_c_re, s5_c_im, s5_log_dt, s5_d, s5_w_glu, attn_q_gain, attn_k_gain, rwkv_shift_mu, rwkv_w0, rwkv_w2, rwkv_a0, rwkv_a2, rwkv_g2, rwkv_k_k, rwkv_k_a, rwkv_r_k, rwkv_ln_w, rwkv_ln_b, rwkv_v0, rwkv_v1, rwkv_v2, w_branch_s5, w_branch_attn, w_branch_rwkv, w_out, ffn2_norm, ffn2_up, ffn2_down):
    return _trunk(x, ffn1_norm, ffn1_up, ffn1_down, mix_norm, w_in, s5_a_re, s5_a_im, s5_b_re, s5_b_im, s5_c_re,
                  s5_c_im, s5_log_dt, s5_d, s5_w_glu, attn_q_gain, attn_k_gain, rwkv_shift_mu, rwkv_w0, rwkv_w2,
                  rwkv_a0, rwkv_a2, rwkv_g2, rwkv_k_k, rwkv_k_a, rwkv_r_k, rwkv_ln_w, rwkv_ln_b, rwkv_v0, rwkv_v1,
                  rwkv_v2, w_branch_s5, w_branch_attn, w_branch_rwkv, w_out, ffn2_norm, ffn2_up, ffn2_down)
```

```python
import functools
import math

import numpy as np
import jax
import jax.numpy as jnp
from jax import lax
from jax.experimental import pallas as pl
from jax.experimental.pallas import tpu as pltpu

F32 = jnp.float32
BF16 = jnp.bfloat16

LANES = 128
SUBLANES = 8
VMEM_LIMIT_BYTES = 56 * 1024 * 1024

RMS_EPS = 1e-6
S5_GROUP = 16
S5_STATE = 64
S5_TBL_LEVEL = SUBLANES
S5_TBL_SEG = SUBLANES + 3
S5_N_TBL = SUBLANES + 4
ATTN_HEAD_DIM = 64
ATTN_HEADS = 8
DIL_PATTERNS = ((128, 1), (512, 4), (2048, 16))
ALIBI_MAX_BIAS = 8.0
ATTN_BLOCK = 128
RWKV_HEAD_DIM = 64
RWKV_GN_EPS = 64e-5
WKV_CHUNK = 64
NEG_BIG = -1e30


def _cparams(sem):
    return pltpu.CompilerParams(dimension_semantics=sem, vmem_limit_bytes=VMEM_LIMIT_BYTES)


def _dot(a, b, precision=None):
    return jnp.dot(a, b, preferred_element_type=F32, precision=precision)


def _dot_nt(a, b, precision=None):
    return lax.dot_general(a, b, (((1,), (1,)), ((), ())), preferred_element_type=F32, precision=precision)


def _dot_tn(a, b, precision=None):
    return lax.dot_general(a, b, (((0,), (0,)), ((), ())), preferred_element_type=F32, precision=precision)


def _sigmoid(x):
    return 1.0 / (1.0 + jnp.exp(-x))


def _layer_row(arr, layer):
    n = arr.shape[-1]
    a3 = arr.reshape(arr.shape[0], 1, n)
    return a3, pl.BlockSpec((None, 1, n), lambda *_: (layer, 0, 0))


def _norm_parts_kernel(x_ref, g_ref, xg_ref, rinv_ref):
    x = x_ref[...]
    xg_ref[...] = (x * g_ref[...]).astype(xg_ref.dtype)
    rinv = lax.rsqrt(jnp.mean(x * x, axis=-1, keepdims=True) + RMS_EPS)
    rinv_ref[...] = jnp.broadcast_to(rinv, rinv_ref.shape)


def _norm_parts(x, gain, layer):
    m, d = x.shape
    tm = min(m, 1024)
    g3, g_spec = _layer_row(gain, layer)
    return pl.pallas_call(
        _norm_parts_kernel,
        grid=(m // tm,),
        in_specs=[pl.BlockSpec((tm, d), lambda i: (i, 0)), g_spec],
        out_specs=[pl.BlockSpec((tm, d), lambda i: (i, 0)), pl.BlockSpec((tm, LANES), lambda i: (i, 0))],
        out_shape=[jax.ShapeDtypeStruct((m, d), BF16), jax.ShapeDtypeStruct((m, LANES), F32)],
        compiler_params=_cparams(("parallel",)),
        name="norm_parts",
    )(x, g3)


def _mm_kernel(*refs, n_lhs, lhs_needs_cast, rhs_lhs, rhs_parts, n_extra, epilogue, tn, row_chunk, has_row_scale,
               norm_dim):
    it = iter(refs)
    lhs_refs = [next(it) for _ in range(n_lhs)]
    rhs_refs = [[next(it) for _ in range(n_parts)] for n_parts, _ in rhs_parts]
    extra_refs = [next(it) for _ in range(n_extra)]
    scale_ref = next(it) if has_row_scale else None
    gain_ref = next(it) if norm_dim else None
    out_ref = next(it)
    xg_ref, rinv_ref = (next(it), next(it)) if norm_dim else (None, None)
    lhs_scr = {p: next(it) for p in range(n_lhs) if lhs_needs_cast[p]}
    w_scr = [next(it) for _ in range(n_lhs)]
    ss_scr = next(it) if norm_dim else None
    j = pl.program_id(1)

    @pl.when(j == 0)
    def _():
        for p, scr in lhs_scr.items():
            scr[...] = lhs_refs[p][...].astype(BF16)

    for p in range(n_lhs):
        mine = [r for r, q in enumerate(rhs_lhs) if q == p]
        for n, r in enumerate(mine):
            transposed = rhs_parts[r][1]
            parts = [t[...] for t in rhs_refs[r]]
            if transposed:
                w = (parts[0] if len(parts) == 1 else jnp.concatenate(parts, axis=0)).T
            else:
                w = parts[0] if len(parts) == 1 else jnp.concatenate(parts, axis=1)
            w_scr[p][:, n * tn:(n + 1) * tn] = w.astype(BF16)

    def lane_fold(x):
        acc = x[:, :LANES]
        for c in range(LANES, x.shape[1], LANES):
            acc = acc + x[:, c:c + LANES]
        return acc

    def chunk(c):
        rows = pl.ds(c * row_chunk, row_chunk)
        accs = [None] * len(rhs_lhs)
        for p in range(n_lhs):
            mine = [r for r, q in enumerate(rhs_lhs) if q == p]
            a = (lhs_scr[p] if lhs_needs_cast[p] else lhs_refs[p])[rows, :]
            acc = _dot(a, w_scr[p][...])
            if has_row_scale:
                rs = scale_ref[rows, :]
                acc = acc * jnp.concatenate([rs] * (acc.shape[1] // LANES), axis=1)
            for n, r in enumerate(mine):
                accs[r] = acc[:, n * tn:(n + 1) * tn]
        res = epilogue(accs, [e[rows, :] for e in extra_refs])
        out_ref[rows, :] = res.astype(out_ref.dtype)
        if norm_dim:
            xg_ref[rows, :] = (res * gain_ref[...]).astype(xg_ref.dtype)
            sq = lane_fold(res * res)
            ss_scr[rows, :] = jnp.where(j == 0, sq, ss_scr[rows, :] + sq)

    for c in range(out_ref.shape[0] // row_chunk):
        chunk(c)

    if norm_dim:
        @pl.when(j == pl.num_programs(1) - 1)
        def _():
            tot = jnp.sum(ss_scr[...], axis=-1, keepdims=True)
            rinv_ref[...] = jnp.broadcast_to(lax.rsqrt(tot * (1.0 / norm_dim) + RMS_EPS), rinv_ref.shape)


def _mm(name, lhs, rhs, extras, epilogue, n_out, out_dtype, tm, tn, row_chunk=1024, row_scale=None, norm_out=None):
    m = lhs[0].shape[0]
    tm = min(tm, m)
    row_chunk = min(row_chunk, tm)
    operands, in_specs, lhs_scratch, w_scratch = [], [], [], []
    lhs_needs_cast = []
    for p, a in enumerate(lhs):
        k = a.shape[1]
        operands.append(a)
        in_specs.append(pl.BlockSpec((tm, k), lambda i, j: (i, 0), pipeline_mode=pl.Buffered(1)))
        lhs_needs_cast.append(a.dtype != BF16)
        if a.dtype != BF16:
            lhs_scratch.append(pltpu.VMEM((tm, k), BF16))
        w_scratch.append(pltpu.VMEM((k, tn * sum(1 for r in rhs if r[3] == p)), BF16))
    rhs_parts = []
    for w, layer, off, _ in rhs:
        if isinstance(off, tuple):
            start = off[1]
            part = math.gcd(start, tn)
            assert part % SUBLANES == 0
            n_parts = tn // part
            rhs_parts.append((n_parts, True))
            for t in range(n_parts):
                operands.append(w)
                in_specs.append(pl.BlockSpec(
                    (None, part, w.shape[2]),
                    lambda i, j, b=start // part + t, s=n_parts, layer=layer: (layer, b + s * j, 0)))
        else:
            rhs_parts.append((1, False))
            operands.append(w)
            if layer is None:
                in_specs.append(pl.BlockSpec((w.shape[0], tn), lambda i, j, off=off: (0, off + j)))
            else:
                in_specs.append(pl.BlockSpec((None, w.shape[1], tn),
                                             lambda i, j, off=off, layer=layer: (layer, 0, off + j)))
    for e, off in extras:
        operands.append(e)
        in_specs.append(pl.BlockSpec((tm, tn), lambda i, j, off=off: (i, off + j)))
    if row_scale is not None:
        operands.append(row_scale)
        in_specs.append(pl.BlockSpec((tm, LANES), lambda i, j: (i, 0)))
    out_specs = [pl.BlockSpec((tm, tn), lambda i, j: (i, j))]
    out_shape = [jax.ShapeDtypeStruct((m, n_out), out_dtype)]
    norm_scratch = []
    if norm_out is not None:
        gain, gain_layer = norm_out
        operands.append(gain.reshape(gain.shape[0], 1, n_out))
        in_specs.append(pl.BlockSpec((None, 1, tn), lambda i, j: (gain_layer, 0, j)))
        out_specs += [pl.BlockSpec((tm, tn), lambda i, j: (i, j)), pl.BlockSpec((tm, LANES), lambda i, j: (i, 0))]
        out_shape += [jax.ShapeDtypeStruct((m, n_out), BF16), jax.ShapeDtypeStruct((m, LANES), F32)]
        norm_scratch = [pltpu.VMEM((tm, LANES), F32)]
    kern = functools.partial(
        _mm_kernel, n_lhs=len(lhs), lhs_needs_cast=tuple(lhs_needs_cast), rhs_lhs=tuple(r[3] for r in rhs),
        rhs_parts=tuple(rhs_parts), n_extra=len(extras), epilogue=epilogue, tn=tn, row_chunk=row_chunk,
        has_row_scale=row_scale is not None, norm_dim=n_out if norm_out is not None else 0)
    res = pl.pallas_call(
        kern,
        grid=(m // tm, n_out // tn),
        in_specs=in_specs,
        out_specs=out_specs,
        out_shape=out_shape,
        scratch_shapes=lhs_scratch + w_scratch + norm_scratch,
        compiler_params=_cparams(("parallel", "arbitrary")),
        name=name,
    )(*operands)
    return res if norm_out is not None else res[0]


def _ep_swiglu(accs, extras):
    a, b = accs
    return a * _sigmoid(a) * b


def _ep_half_residual(accs, extras):
    return extras[0] + 0.5 * accs[0]


def _ep_residual(accs, extras):
    return extras[0] + accs[0]


def _ep_plain(accs, extras):
    return accs[0]


def _ep_glu(accs, extras):
    z = extras[0]
    return z * _sigmoid(accs[0])


def _ep_gated_sum(accs, extras):
    out = _sigmoid(extras[0].astype(F32)) * accs[0]
    for g, a in zip(extras[1:], accs[1:]):
        out = out + _sigmoid(g.astype(F32)) * a
    return out


def _swiglu_ffn(x, xg, rinv, w_up, w_down, layer, tm_up, tm_down, next_norm):
    d_ff = w_down.shape[1]
    tn = LANES
    act = _mm("ffn_up", [xg], [(w_up, layer, 0, 0), (w_up, layer, d_ff // tn, 0)], [], _ep_swiglu,
              d_ff, BF16, tm_up, tn, row_scale=rinv)
    return _mm("ffn_down", [act], [(w_down, layer, 0, 0)], [(x, 0)], _ep_half_residual,
               x.shape[1], F32, tm_down, 256, row_chunk=512, norm_out=next_norm)


def _cmul(ar, ai, br, bi):
    return ar * br - ai * bi, ar * bi + ai * br


def _s5_prep_kernel(are_ref, aim_ref, ldt_ref, bre_ref, bim_ref, bbre_ref, bbim_ref, pw_ref):
    lam_re = jnp.minimum(are_ref[...], -1e-4)
    lam_im = aim_ref[...]
    dt = jnp.exp(ldt_ref[...])
    mag = jnp.exp(lam_re * dt)
    ab_re = mag * jnp.cos(lam_im * dt)
    ab_im = mag * jnp.sin(lam_im * dt)
    den = lam_re * lam_re + lam_im * lam_im
    num_re = ab_re - 1.0
    coef_re = (num_re * lam_re + ab_im * lam_im) / den
    coef_im = (ab_im * lam_re - num_re * lam_im) / den
    for h in range(bre_ref.shape[0]):
        br = bre_ref[h]
        bi = bim_ref[h]
        bbre_ref[h] = coef_re * br - coef_im * bi
        bbim_ref[h] = coef_re * bi + coef_im * br
    powers = [(ab_re, ab_im)]
    for _ in range(SUBLANES - 1):
        powers.append(_cmul(*powers[-1], ab_re, ab_im))
    for t, (pr, pi) in enumerate(powers):
        for r in range(SUBLANES):
            pw_ref[0, t, r] = pr
            pw_ref[1, t, r] = pi
    a8 = powers[-1]
    zero = jnp.zeros_like(ab_re)
    lvl_pow = a8
    for lvl in range(3):
        for r in range(SUBLANES):
            live = r >= (1 << lvl)
            pw_ref[0, S5_TBL_LEVEL + lvl, r] = lvl_pow[0] if live else zero
            pw_ref[1, S5_TBL_LEVEL + lvl, r] = lvl_pow[1] if live else zero
        lvl_pow = _cmul(*lvl_pow, *lvl_pow)
    pr, pi = a8
    for r in range(SUBLANES):
        pw_ref[0, S5_TBL_SEG, r] = pr
        pw_ref[1, S5_TBL_SEG, r] = pi
        pr, pi = _cmul(pr, pi, *a8)


def _s5_scan_kernel(u_ref, bre_ref, bim_ref, cre_ref, cim_ref, pw_ref, d_ref, z_ref, up_s, sre, sim, yp_s,
                    *, seq, lane_chunk):
    S = SUBLANES
    sb_rows = S * S
    n_sb = seq // sb_rows
    n_state = sre.shape[1]

    def transpose_superblocks(read_tile, write_tile):
        def body(sb, _):
            base = pl.multiple_of(sb * sb_rows, sb_rows)
            for t in range(S):
                write_tile(base, t, read_tile(base, t))
            return 0
        lax.fori_loop(0, n_sb, body, 0)

    def put_u(base, t, x):
        up_s[pl.ds(base + t * S, S), :] = x

    transpose_superblocks(lambda base, t: u_ref[pl.ds(base + t, S, stride=S), :], put_u)
    up = up_s[...].astype(BF16)
    sre[...] = _dot(up, bre_ref[...])
    sim[...] = _dot(up, bim_ref[...])

    def scan_superblock(sb, carry):
        base = pl.multiple_of(sb * sb_rows, sb_rows)
        new_carry = []
        for ch in range(n_state // lane_chunk):
            ln = pl.ds(ch * lane_chunk, lane_chunk)
            c_re, c_im = carry[2 * ch], carry[2 * ch + 1]
            ar, ai = pw_ref[0, 0, :, ln], pw_ref[1, 0, :, ln]
            sr, si = sre[pl.ds(base, S), ln], sim[pl.ds(base, S), ln]
            local = [(sr, si)]
            for t in range(1, S):
                rows = pl.ds(base + t * S, S)
                sr, si = sre[rows, ln] + ar * sr - ai * si, sim[rows, ln] + ar * si + ai * sr
                local.append((sr, si))
            er, ei = local[-1]
            for lvl in range(3):
                hr, hi = pw_ref[0, S5_TBL_LEVEL + lvl, :, ln], pw_ref[1, S5_TBL_LEVEL + lvl, :, ln]
                yr, yi = pltpu.roll(er, 1 << lvl, 0), pltpu.roll(ei, 1 << lvl, 0)
                er, ei = er + hr * yr - hi * yi, ei + hr * yi + hi * yr
            qr, qi = pw_ref[0, S5_TBL_SEG, :, ln], pw_ref[1, S5_TBL_SEG, :, ln]
            fr, fi = er + qr * c_re - qi * c_im, ei + qr * c_im + qi * c_re
            first = lax.broadcasted_iota(jnp.int32, fr.shape, 0) == 0
            in_r = jnp.where(first, c_re, pltpu.roll(fr, 1, 0))
            in_i = jnp.where(first, c_im, pltpu.roll(fi, 1, 0))
            for t in range(S):
                rows = pl.ds(base + t * S, S)
                pr, pi = pw_ref[0, t, :, ln], pw_ref[1, t, :, ln]
                sr, si = local[t]
                sre[rows, ln] = sr + pr * in_r - pi * in_i
                sim[rows, ln] = si + pr * in_i + pi * in_r
            new_carry.append(jnp.broadcast_to(fr[S - 1:S, :], fr.shape))
            new_carry.append(jnp.broadcast_to(fi[S - 1:S, :], fi.shape))
        return tuple(new_carry)

    zero = jnp.zeros((S, lane_chunk), F32)
    lax.fori_loop(0, n_sb, scan_superblock, (zero,) * (2 * (n_state // lane_chunk)))
    yp_s[...] = _dot(sre[...].astype(BF16), cre_ref[...]) - _dot(sim[...].astype(BF16), cim_ref[...])
    d = d_ref[...]

    def put_z(base, t, y):
        rows = pl.ds(base + t * S, S)
        z_ref[rows, :] = jax.nn.gelu(y + d * u_ref[rows, :])

    transpose_superblocks(lambda base, t: yp_s[pl.ds(base + t, S, stride=S), :], put_z)


def _s5_mixer_pre_glu(proj3, layer, a_re, a_im, b_re, b_im, c_re, c_im, log_dt, d_skip):
    bsz, seq, _ = proj3.shape
    n_grp, n_st = a_re.shape[1], a_re.shape[2]
    hc = b_re.shape[3]
    gpb = LANES // hc
    n_blk = n_grp // gpb
    nsl = gpb * n_st
    ldt = jnp.broadcast_to(log_dt[layer][:, None], (n_grp, n_st))
    b_re_t = jnp.transpose(b_re[layer], (2, 0, 1))
    b_im_t = jnp.transpose(b_im[layer], (2, 0, 1))
    bb_re, bb_im, pw = pl.pallas_call(
        _s5_prep_kernel,
        out_shape=(jax.ShapeDtypeStruct((hc, n_grp, n_st), F32), jax.ShapeDtypeStruct((hc, n_grp, n_st), F32),
                   jax.ShapeDtypeStruct((2, S5_N_TBL, SUBLANES, n_grp, n_st), F32)),
        name="s5_prep",
    )(a_re[layer], a_im[layer], ldt, b_re_t, b_im_t)
    eye = jnp.eye(gpb, dtype=F32)

    def blockdiag_in(bb):
        x = jnp.transpose(bb, (1, 0, 2)).reshape(n_blk, gpb, hc, n_st)
        x = x[:, :, :, None, :] * eye[None, :, None, :, None]
        return x.reshape(n_blk, gpb * hc, nsl)

    def blockdiag_out(c):
        x = jnp.transpose(c.reshape(n_blk, gpb, hc, n_st), (0, 1, 3, 2))
        x = x[:, :, :, None, :] * eye[None, :, None, :, None]
        return x.reshape(n_blk, nsl, gpb * hc)

    bmat_re, bmat_im = blockdiag_in(bb_re), blockdiag_in(bb_im)
    cmat_re, cmat_im = blockdiag_out(c_re[layer]), blockdiag_out(c_im[layer])
    bmat_re, bmat_im, cmat_re, cmat_im = (x.astype(BF16) for x in (bmat_re, bmat_im, cmat_re, cmat_im))
    pw_b = pw.reshape(2, S5_N_TBL, SUBLANES, n_blk, nsl).transpose(3, 0, 1, 2, 4)
    d3, d_spec = d_skip.reshape(d_skip.shape[0], 1, -1), pl.BlockSpec((None, 1, LANES), lambda b, g: (layer, 0, g))
    return pl.pallas_call(
        functools.partial(_s5_scan_kernel, seq=seq, lane_chunk=nsl),
        grid=(bsz, n_blk),
        in_specs=[
            pl.BlockSpec((None, seq, LANES), lambda b, g: (b, 0, g)),
            pl.BlockSpec((None, LANES, nsl), lambda b, g: (g, 0, 0)),
            pl.BlockSpec((None, LANES, nsl), lambda b, g: (g, 0, 0)),
            pl.BlockSpec((None, nsl, LANES), lambda b, g: (g, 0, 0)),
            pl.BlockSpec((None, nsl, LANES), lambda b, g: (g, 0, 0)),
            pl.BlockSpec((None, 2, S5_N_TBL, SUBLANES, nsl), lambda b, g: (g, 0, 0, 0, 0)),
            d_spec,
        ],
        out_specs=pl.BlockSpec((None, seq, LANES), lambda b, g: (b, 0, g)),
        out_shape=jax.ShapeDtypeStruct((bsz, seq, n_grp * hc), F32),
        scratch_shapes=[pltpu.VMEM((seq, LANES), F32), pltpu.VMEM((seq, nsl), F32), pltpu.VMEM((seq, nsl), F32),
                        pltpu.VMEM((seq, LANES), F32)],
        compiler_params=_cparams(("parallel", "parallel")),
        name="s5_scan",
    )(proj3, bmat_re, bmat_im, cmat_re, cmat_im, pw_b, d3)


def _attn_kernel(slopes_ref, qg_ref, kg_ref, *refs, seq):
    qkv = refs[:9]
    out_ref = refs[9]
    qn_s, kn_s, qm, km, vm, base, s_s, p_s, sc_s, o_seq, l_seq, o_all, l_all = refs[10:]
    pair = pl.program_id(1)
    hd = ATTN_HEAD_DIM
    blk = ATTN_BLOCK
    lane = lax.broadcasted_iota(jnp.int32, (1, LANES), 1)
    head_mask = [(lane < hd).astype(F32), (lane >= hd).astype(F32)]
    qg = qg_ref[...]
    kg = kg_ref[...]

    ri = lax.broadcasted_iota(jnp.int32, (LANES, LANES), 0) // hd
    ci = lax.broadcasted_iota(jnp.int32, (LANES, LANES), 1) // hd
    head_ones = jnp.where(ri == ci, 1.0, 0.0).astype(BF16)

    def qk_norm(x, gain):
        ms = _dot((x * x).astype(BF16), head_ones) * (1.0 / hd)
        return x * lax.rsqrt(ms + RMS_EPS) * gain

    def block_rows(bi):
        return pl.ds(bi * blk, blk) if isinstance(bi, int) else pl.ds(pl.multiple_of(bi * blk, blk), blk)

    def key_rows(bi, first):
        if first:
            start, size = bi * blk, blk
        else:
            start, size = (bi - 1) * blk, 2 * blk
        return pl.ds(start, size) if isinstance(bi, int) else pl.ds(pl.multiple_of(start, blk), size)

    for g, (window, dil) in enumerate(DIL_PATTERNS):
        q_ref, k_ref, v_ref = qkv[3 * g:3 * g + 3]
        n = seq // dil
        n_blocks = n // blk
        n_back = window // dil
        assert n_back == blk and n % blk == 0
        rows = lax.broadcasted_iota(jnp.int32, (blk, 2 * blk), 0)
        cols = lax.broadcasted_iota(jnp.int32, (blk, 2 * blk), 1)
        delta = rows + blk - cols
        valid = (delta >= 0) & (delta <= n_back)
        for h in range(2):
            slope = slopes_ref[g * ATTN_HEADS + 2 * pair + h]
            base[h] = jnp.where(valid, (-slope * dil) * delta.astype(F32), NEG_BIG)

        if dil == 1:
            qn = qk_norm(q_ref[...], qg)
            km[...] = qk_norm(k_ref[...], kg).astype(BF16)
            v = v_ref[...]
            for h in range(2):
                qm[h] = (qn * head_mask[h]).astype(BF16)
                vm[h] = (v * head_mask[h]).astype(BF16)
        else:
            qn_s[...] = qk_norm(q_ref[...], qg)
            kn_s[...] = qk_norm(k_ref[...], kg)

            def regroup(i, _):
                for j in range(2):
                    r = 2 * i + j
                    dst = pl.ds(pl.multiple_of(r * n, blk), n)
                    q = qn_s[pl.ds(r, n, stride=dil), :]
                    v = v_ref[pl.ds(r, n, stride=dil), :]
                    km[dst, :] = kn_s[pl.ds(r, n, stride=dil), :].astype(BF16)
                    for h in range(2):
                        qm[h, dst, :] = (q * head_mask[h]).astype(BF16)
                        vm[h, dst, :] = (v * head_mask[h]).astype(BF16)
                return 0
            lax.fori_loop(0, dil // 2, regroup, 0)

        def scores(bi, first):
            for h in range(2):
                bias = base[h, :, blk:] if first else base[h]
                s = _dot_nt(qm[h, block_rows(bi), :], km[key_rows(bi, first), :]) * (hd ** -0.5) + bias
                if first:
                    s_s[bi, h, :, pl.ds(0, blk)] = s
                else:
                    s_s[bi, h] = s

        def softmax(bi, first):
            row_max = jnp.zeros((blk, LANES), F32)
            for h in range(2):
                s = s_s[bi, h, :, pl.ds(0, blk)] if first else s_s[bi, h]
                m = jnp.max(s, axis=-1, keepdims=True)
                p = jnp.exp(s - m)
                if first:
                    p_s[bi, h, :, pl.ds(0, blk)] = p.astype(BF16)
                else:
                    p_s[bi, h] = p.astype(BF16)
                row_max = row_max + head_mask[h] * m
            sc_s[bi] = row_max

        o_dst, l_dst = (o_all.at[g], l_all.at[g]) if dil == 1 else (o_seq, l_seq)

        def values(bi, first):
            o = jnp.zeros((blk, LANES), F32)
            lse = sc_s[bi]
            for h in range(2):
                p = p_s[bi, h, :, pl.ds(0, blk)] if first else p_s[bi, h]
                l = _dot(p, jnp.ones((p.shape[1], LANES), BF16))
                o = o + _dot(p, vm[h, key_rows(bi, first), :]) * (1.0 / l)
                lse = lse + head_mask[h] * jnp.log(l)
            o_dst[block_rows(bi), :] = o
            l_dst[block_rows(bi), :] = lse

        def all_blocks(stage, many):
            if n_blocks == 1:
                per = 8 if many else 4

                def body(i, _):
                    for j in range(per):
                        stage(i * per + j, True)
                    return 0
                lax.fori_loop(0, dil // per, body, 0)
                return
            for r in range(dil):
                stage(r * n_blocks, True)
            per = next(p for p in ((5, 3, 1) if many else (3, 1)) if (n_blocks - 1) % p == 0)
            groups_per_class = (n_blocks - 1) // per

            def body(i, _):
                cls, grp = i // groups_per_class, i % groups_per_class
                for j in range(per):
                    stage(cls * n_blocks + 1 + grp * per + j, False)
                return 0
            lax.fori_loop(0, dil * groups_per_class, body, 0)

        all_blocks(scores, True)
        all_blocks(softmax, False)
        all_blocks(values, True)

        if dil > 1:
            def scatter(i, _):
                for j in range(2):
                    r = 2 * i + j
                    src = pl.ds(pl.multiple_of(r * n, blk), n)
                    o_all[g, pl.ds(r, n, stride=dil), :] = o_seq[src, :]
                    l_all[g, pl.ds(r, n, stride=dil), :] = l_seq[src, :]
                return 0
            lax.fori_loop(0, dil // 2, scatter, 0)

    l0, l1, l2 = l_all[0], l_all[1], l_all[2]
    mx = jnp.maximum(jnp.maximum(l0, l1), l2)
    w0, w1, w2 = jnp.exp(l0 - mx), jnp.exp(l1 - mx), jnp.exp(l2 - mx)
    o = (w0 * o_all[0] + w1 * o_all[1] + w2 * o_all[2]) / (w0 + w1 + w2)
    out_ref[...] = o.astype(out_ref.dtype)


def _attention(proj3, q_off, layer, q_gain, k_gain):
    bsz, seq, _ = proj3.shape
    n_dil = len(DIL_PATTERNS)
    width = n_dil * ATTN_HEADS * ATTN_HEAD_DIM
    n_pairs = ATTN_HEADS * ATTN_HEAD_DIM // LANES
    n_heads = n_dil * ATTN_HEADS
    n_blk = seq // ATTN_BLOCK
    slopes = jnp.asarray(2.0 ** (-ALIBI_MAX_BIAS * (np.arange(n_heads) + 1) / n_heads), dtype=F32)
    qg = jnp.tile(q_gain[layer], 2)[None]
    kg = jnp.tile(k_gain[layer], 2)[None]
    in_specs = [pl.BlockSpec(memory_space=pltpu.SMEM),
                pl.BlockSpec((1, LANES), lambda b, p: (0, 0)), pl.BlockSpec((1, LANES), lambda b, p: (0, 0))]
    operands = [slopes, qg, kg]
    for g in range(n_dil):
        for which in range(3):
            cb = (q_off + which * width + g * ATTN_HEADS * ATTN_HEAD_DIM) // LANES
            in_specs.append(pl.BlockSpec((None, seq, LANES), lambda b, p, cb=cb: (b, 0, cb + p)))
            operands.append(proj3)
    return pl.pallas_call(
        functools.partial(_attn_kernel, seq=seq),
        grid=(bsz, n_pairs),
        in_specs=in_specs,
        out_specs=pl.BlockSpec((None, seq, LANES), lambda b, p: (b, 0, p)),
        out_shape=jax.ShapeDtypeStruct((bsz, seq, ATTN_HEADS * ATTN_HEAD_DIM), BF16),
        scratch_shapes=[
            pltpu.VMEM((seq, LANES), F32), pltpu.VMEM((seq, LANES), F32),
            pltpu.VMEM((2, seq, LANES), BF16),
            pltpu.VMEM((seq, LANES), BF16), pltpu.VMEM((2, seq, LANES), BF16),
            pltpu.VMEM((2, ATTN_BLOCK, 2 * ATTN_BLOCK), F32),
            pltpu.VMEM((n_blk, 2, ATTN_BLOCK, 2 * ATTN_BLOCK), F32),
            pltpu.VMEM((n_blk, 2, ATTN_BLOCK, 2 * ATTN_BLOCK), BF16),
            pltpu.VMEM((n_blk, ATTN_BLOCK, LANES), F32),
            pltpu.VMEM((seq, LANES), F32), pltpu.VMEM((seq, LANES), F32),
            pltpu.VMEM((n_dil, seq, LANES), F32), pltpu.VMEM((n_dil, seq, LANES), F32),
        ],
        compiler_params=_cparams(("parallel", "parallel")),
        name="dilated_attn",
    )(*operands)


def _split_bf16(x):
    hi = x.astype(BF16)
    return hi, (x - hi.astype(F32)).astype(BF16)


def _dot_x3(a, b):
    ah, al = _split_bf16(a)
    bh, bl = _split_bf16(b)
    return _dot(ah, bh) + (_dot(ah, bl) + _dot(al, bh))


def _head_sums(x, ones_bd):
    parts = []
    for c in range(0, x.shape[1], LANES):
        hi, lo = _split_bf16(x[:, c:c + LANES])
        parts.append(_dot(hi, ones_bd) + _dot(lo, ones_bd))
    return jnp.concatenate(parts, axis=1)


def _rwkv_prep_kernel(*refs, has_vres, tiles_per_seq, width, half):
    it = iter(refs)
    cur = [next(it) for _ in range(7)]
    prev = [next(it) for _ in range(7)]
    mu_ref, ones_ref, w0_ref, w2_ref, a0_ref, a2_ref, g2_ref, kk_ref, ka_ref = [next(it) for _ in range(9)]
    if has_vres:
        vfirst_ref, v0_ref, v1_ref, v2_ref = [next(it) for _ in range(4)]
    r_out, lw_out, k_out, v_out, a_out, b_out, g_out = [next(it) for _ in range(7)]
    vfirst_out = None if has_vres else next(it)

    tm = cur[0].shape[0]
    first_of_seq = (pl.program_id(0) % tiles_per_seq) == 0
    row = lax.broadcasted_iota(jnp.int32, (tm, half), 0)

    def shifted(c):
        x = cur[c][...]
        last_prev = jnp.where(first_of_seq, 0.0, prev[c][SUBLANES - 1:SUBLANES, :])
        x_prev = jnp.where(row == 0, last_prev, pltpu.roll(x, 1, 0))
        mu = mu_ref[:, c * half:(c + 1) * half]
        return x + (x_prev - x) * mu

    sh = [shifted(c) for c in range(7)]
    r = jnp.concatenate(sh[0:2], axis=1)
    k = jnp.concatenate(sh[2:4], axis=1)
    v = jnp.concatenate(sh[4:6], axis=1)
    lora = sh[6]
    w_log = -jax.nn.softplus(-(w0_ref[...] + _dot(jnp.tanh(lora).astype(BF16), w2_ref[...]))) - 0.5
    a = _sigmoid(a0_ref[...] + _dot(lora.astype(BF16), a2_ref[...]))
    g = _dot(_sigmoid(lora).astype(BF16), g2_ref[...])
    if has_vres:
        low = _dot(v.astype(BF16), v1_ref[...])
        mix = _sigmoid(v0_ref[...] + _dot(low.astype(BF16), v2_ref[...]))
        v = v + (vfirst_ref[...] - v) * mix
    ones_bd = ones_ref[...]
    kk = k * kk_ref[...]
    kk = kk * lax.rsqrt(jnp.maximum(_head_sums(kk * kk, ones_bd), 1e-24))
    k = k * (1.0 + (a - 1.0) * ka_ref[...])
    r_out[...] = r.astype(r_out.dtype)
    lw_out[...] = -jnp.exp(w_log)
    k_out[...] = k.astype(k_out.dtype)
    v_out[...] = v.astype(v_out.dtype)
    a_out[...] = (-kk).astype(a_out.dtype)
    b_out[...] = (kk * a).astype(b_out.dtype)
    g_out[...] = g.astype(g_out.dtype)
    if vfirst_out is not None:
        vfirst_out[...] = v


def _wkv_kernel(r_ref, lw_ref, k_ref, v_ref, a_ref, b_ref, g_ref, lnw_ref, lnb_ref, rk_ref, out_ref,
                qp_s, yi_s, m_s, n_s, st_s, pw_s, tinv_s, rb_s, ast_s, vst_s, be_s, ke_s, rst_s, av_s,
                akrk_s, bk_s, wu_s, cum_s, cend_s,
                *, seq, group):
    L = WKV_CHUNK
    N = RWKV_HEAD_DIM
    P = 2 * N
    assert P == LANES and L == N
    n_chunks = seq // L
    ti = lax.broadcasted_iota(jnp.int32, (L, L), 0)
    si = lax.broadcasted_iota(jnp.int32, (L, L), 1)
    tri_incl = jnp.clip((ti - si).astype(F32) + 1.0, 0.0, 1.0).astype(BF16)
    ri = lax.broadcasted_iota(jnp.int32, (P, P), 0)
    ci = lax.broadcasted_iota(jnp.int32, (P, P), 1)
    same_head = (ri // L) == (ci // N)
    strict = same_head & ((ci % L) < (ri % L))
    incl = same_head & ((ci % L) <= (ri % L))
    eye = ri == ci
    lane = lax.broadcasted_iota(jnp.int32, (1, P), 1)
    head_mask = [(lane < N).astype(F32), (lane >= N).astype(F32)]

    def stack(x):
        return jnp.concatenate([x * head_mask[0], x * head_mask[1]], axis=0)

    def dup(x):
        return jnp.concatenate([x, x], axis=0)

    def fold(x):
        return x[:L] + x[L:]

    def for_chunk_groups(fn, per_iter):
        def body(i, _):
            for j in range(per_iter):
                fn(i * per_iter + j)
            return 0
        lax.fori_loop(0, n_chunks // per_iter, body, 0)

    def chunk_rows(c):
        return pl.ds(c * L, L) if isinstance(c, int) else pl.ds(pl.multiple_of(c * L, L), L)

    def stage_cumdecay(c):
        lw_hi, lw_lo = _split_bf16(lw_ref[chunk_rows(c), :])
        cum_s[c] = _dot(tri_incl, lw_hi) + _dot(tri_incl, lw_lo)

    def stage_scale(c):
        rows = chunk_rows(c)
        cum = cum_s[c]
        cum_end = cum[L - 1:L, :]
        ast_s[c] = stack(a_ref[rows, :] * jnp.exp(cum - lw_ref[rows, :])).astype(BF16)
        r_t = r_ref[rows, :] * jnp.exp(cum)
        qp_s[rows, :] = r_t
        rst_s[c] = stack(r_t).astype(BF16)
        g_inv = jnp.exp(-cum)
        bk_s[c] = jnp.concatenate([dup(b_ref[rows, :] * g_inv), dup(k_ref[rows, :] * g_inv)], axis=0).astype(BF16)
        g_end = jnp.exp(cum_end - cum)
        be_s[c] = stack(b_ref[rows, :] * g_end).astype(BF16)
        ke_s[c] = stack(k_ref[rows, :] * g_end).astype(BF16)
        vst_s[c] = stack(v_ref[rows, :]).astype(BF16)
        cend_s[c] = jnp.broadcast_to(cum_end, (SUBLANES, P))

    def stage_scores(c):
        ar_st = jnp.concatenate([ast_s[c], rst_s[c]], axis=0)
        ar_bk = _dot_nt(ar_st, bk_s[c])
        a_ab = jnp.where(strict, ar_bk[:P, :P], 0.0)
        pw_s[c] = a_ab.astype(BF16)
        tinv_s[c] = jnp.where(eye, 1.0, a_ab)
        rb_s[c] = jnp.where(incl, ar_bk[P:, :P], 0.0).astype(BF16)
        akrk_s[c] = jnp.concatenate([jnp.where(strict, ar_bk[:P, P:], 0.0),
                                     jnp.where(incl, ar_bk[P:, P:], 0.0)], axis=0).astype(BF16)

    def stage_inverse_square(c):
        pw = pw_s[c]
        pw_s[c] = _dot(pw, pw).astype(BF16)

    def stage_inverse_level(c):
        pw = pw_s[c]
        tinv = tinv_s[c]
        both = _dot(jnp.concatenate([pw, tinv.astype(BF16)], axis=0), pw)
        pw_s[c] = both[:P].astype(BF16)
        tinv_s[c] = tinv + both[P:]

    def stage_inverse_last(c):
        tinv = tinv_s[c]
        tinv_s[c] = tinv + _dot(tinv.astype(BF16), pw_s[c])

    def stage_av(c):
        both = _dot(akrk_s[c], vst_s[c])
        av_s[c] = both[:P].astype(BF16)
        yi_s[chunk_rows(c), :] = fold(both[P:])

    def stage_wu(c):
        rhs = jnp.concatenate([ast_s[c], av_s[c]], axis=1)
        wu_s[c] = _dot(tinv_s[c].astype(BF16), rhs).astype(BF16)

    def stage_summary(c):
        rows = chunk_rows(c)
        wu = wu_s[c]
        v_st = vst_s[c]
        rb_wu = _dot(rb_s[c], wu)
        be_wu = _dot_tn(be_s[c], wu)
        qp_s[rows, :] = qp_s[rows, :] + fold(rb_wu[:, :P])
        yi_s[rows, :] = yi_s[rows, :] + fold(rb_wu[:, P:])
        d_end = jnp.where(eye, jnp.broadcast_to(jnp.exp(cend_s[c][0:1, :]), (P, P)), 0.0)
        m_s[c] = d_end + be_wu[:, :P]
        n_s[c] = be_wu[:, P:] + _dot_tn(ke_s[c], v_st)

    for_chunk_groups(stage_cumdecay, group)
    for_chunk_groups(stage_scale, min(group, 4))
    for_chunk_groups(stage_scores, group)

    def inverse_levels(_, carry):
        for_chunk_groups(stage_inverse_level, inv_group)
        return carry

    n_powers = (L - 1).bit_length()
    inv_group = min(2 * group, n_chunks)
    for_chunk_groups(stage_inverse_square, inv_group)
    lax.fori_loop(0, n_powers - 2, inverse_levels, 0)
    for_chunk_groups(stage_inverse_last, inv_group)
    for_chunk_groups(stage_av, group)
    for_chunk_groups(stage_wu, group)

    def head_stat(x):
        s0 = jnp.sum(x * head_mask[0], axis=-1, keepdims=True)
        s1 = jnp.sum(x * head_mask[1], axis=-1, keepdims=True)
        return s0 * head_mask[0] + s1 * head_mask[1]

    def emit(c):
        rows = chunk_rows(c)
        y = _dot(qp_s[rows, :].astype(BF16), st_s[c].astype(BF16)) + yi_s[rows, :]
        yc = y - head_stat(y) * (1.0 / N)
        var = head_stat(yc * yc) * (1.0 / N)
        y = yc * lax.rsqrt(var + RWKV_GN_EPS) * lnw_ref[...] + lnb_ref[...]
        bonus = head_stat(r_ref[rows, :].astype(F32) * k_ref[rows, :] * rk_ref[...])
        out_ref[rows, :] = ((y + bonus * v_ref[rows, :]) * g_ref[rows, :]).astype(out_ref.dtype)

    per = max(1, min(4, n_chunks // 2))
    n_groups = n_chunks // per

    def chain(g):
        for j in range(per):
            c = g * per + j
            st_s[c + 1] = _dot_x3(m_s[c], st_s[c]) + n_s[c]

    def emit_group(g):
        for j in range(per):
            emit(g * per + j)

    def summarise_group(g):
        for j in range(per):
            stage_summary(g * per + j)

    def steady(g, _):
        emit_group(g - 1)
        chain(g)
        summarise_group(g + 1)
        return 0

    st_s[0] = jnp.zeros((P, P), F32)
    summarise_group(0)
    chain(0)
    summarise_group(1)
    lax.fori_loop(1, n_groups - 1, steady, 0)
    emit_group(n_groups - 2)
    chain(n_groups - 1)
    emit_group(n_groups - 1)


def _rwkv_mixer(proj, p_off, bsz, seq, layer, v_first, shift_mu, w0, w2, a0, a2, g2, k_k, k_a, r_k, ln_w, ln_b,
                v0, v1, v2, tm):
    m = proj.shape[0]
    width = w0.shape[1]
    half = width // 2
    n_lora = w2.shape[1] + a2.shape[1] + g2.shape[1]
    assert n_lora <= half and p_off % half == 0 and 3 * width % half == 0
    has_vres = layer > 0
    cb0 = p_off // half
    rows8 = tm // SUBLANES
    operands, in_specs = [], []
    for c in range(7):
        operands.append(proj)
        in_specs.append(pl.BlockSpec((tm, half), lambda i, c=c: (i, cb0 + c)))
    for c in range(7):
        operands.append(proj)
        in_specs.append(pl.BlockSpec((SUBLANES, half), lambda i, c=c: (jnp.maximum(i * rows8 - 1, 0), cb0 + c)))

    def full2(x):
        operands.append(x)
        in_specs.append(pl.BlockSpec(x.shape, lambda i: (0, 0)))

    def lrow(x):
        a3, spec = _layer_row(x, layer)
        operands.append(a3)
        in_specs.append(spec)

    mu = jnp.pad(shift_mu[layer], (0, 7 * half - shift_mu.shape[1]))[None]
    full2(mu)
    head_of_lane = np.arange(LANES) // RWKV_HEAD_DIM
    full2(jnp.asarray(head_of_lane[:, None] == head_of_lane[None, :], dtype=F32))
    d_lo, a_lo = w2.shape[1], a2.shape[1]
    w2p = jnp.pad(w2[layer], ((0, half - d_lo), (0, 0)))
    a2p = jnp.pad(a2[layer], ((d_lo, half - d_lo - a_lo), (0, 0)))
    g2p = jnp.pad(g2[layer], ((d_lo + a_lo, half - n_lora), (0, 0)))
    lrow(w0)
    full2(w2p.astype(BF16))
    lrow(a0)
    full2(a2p.astype(BF16))
    full2(g2p.astype(BF16))
    lrow(k_k)
    lrow(k_a)
    if has_vres:
        operands.append(v_first)
        in_specs.append(pl.BlockSpec((tm, width), lambda i: (i, 0)))
        lrow_layer = layer - 1
        for x in (v0,):
            a3 = x.reshape(x.shape[0], 1, width)
            operands.append(a3)
            in_specs.append(pl.BlockSpec((None, 1, width), lambda i: (lrow_layer, 0, 0)))
        full2(v1[layer - 1].astype(BF16))
        full2(v2[layer - 1].astype(BF16))
    tile = pl.BlockSpec((tm, width), lambda i: (i, 0))
    out_dtypes = [BF16, F32, BF16, BF16, BF16, BF16, BF16] + ([] if has_vres else [F32])
    r, lw, k, v, a, b, g, *rest = pl.pallas_call(
        functools.partial(_rwkv_prep_kernel, has_vres=has_vres, tiles_per_seq=seq // tm, width=width, half=half),
        grid=(m // tm,),
        in_specs=in_specs,
        out_specs=[tile] * len(out_dtypes),
        out_shape=[jax.ShapeDtypeStruct((m, width), dt) for dt in out_dtypes],
        compiler_params=_cparams(("parallel",)),
        name="rwkv_prep",
    )(*operands)
    if not has_vres:
        v_first = rest[0]

    n_pairs = width // LANES
    seq_spec = pl.BlockSpec((None, seq, LANES), lambda bi, p: (bi, 0, p))
    row_spec = pl.BlockSpec((None, 1, LANES), lambda bi, p: (layer, 0, p))
    n_chunks = seq // WKV_CHUNK
    N = RWKV_HEAD_DIM
    y = pl.pallas_call(
        functools.partial(_wkv_kernel, seq=seq, group=min(16, n_chunks)),
        grid=(bsz, n_pairs),
        in_specs=[seq_spec] * 7 + [row_spec] * 3,
        out_specs=seq_spec,
        out_shape=jax.ShapeDtypeStruct((bsz, seq, width), BF16),
        scratch_shapes=(
            [pltpu.VMEM((seq, LANES), F32)] * 2
            + [pltpu.VMEM((n_chunks, LANES, LANES), F32)] * 2
            + [pltpu.VMEM((n_chunks + 1, LANES, LANES), F32)]
            + [pltpu.VMEM((n_chunks, LANES, LANES), BF16)]
            + [pltpu.VMEM((n_chunks, LANES, LANES), F32)]
            + [pltpu.VMEM((n_chunks, LANES, LANES), BF16)] * 7
            + [pltpu.VMEM((n_chunks, 2 * LANES, LANES), BF16)] * 2
            + [pltpu.VMEM((n_chunks, LANES, 2 * LANES), BF16)]
            + [pltpu.VMEM((n_chunks, WKV_CHUNK, LANES), F32)]
            + [pltpu.VMEM((n_chunks, SUBLANES, LANES), F32)]
        ),
        compiler_params=_cparams(("parallel", "parallel")),
        name="wkv7",
    )(*[x.reshape(bsz, seq, width) for x in (r, lw, k, v, a, b, g)],
      ln_w.reshape(-1, 1, width), ln_b.reshape(-1, 1, width), r_k.reshape(-1, 1, width))
    return y, v_first


def _row_tiles(m):
    plan = dict(ffn_up=8192, in_proj=4096, ffn_down=2048, narrow=4096, rwkv_prep=512)
    return {k: min(v, m) for k, v in plan.items()}


def _trunk(x, ffn1_norm, ffn1_up, ffn1_down, mix_norm, w_in, s5_a_re, s5_a_im, s5_b_re, s5_b_im, s5_c_re, s5_c_im,
           s5_log_dt, s5_d, s5_w_glu, attn_q_gain, attn_k_gain, rwkv_shift_mu, rwkv_w0, rwkv_w2, rwkv_a0, rwkv_a2,
           rwkv_g2, rwkv_k_k, rwkv_k_a, rwkv_r_k, rwkv_ln_w, rwkv_ln_b, rwkv_v0, rwkv_v1, rwkv_v2,
           w_branch_s5, w_branch_attn, w_branch_rwkv, w_out, ffn2_norm, ffn2_up, ffn2_down):
    bsz, seq, d_model = x.shape
    depth = w_in.shape[0]
    m = bsz * seq
    s5_width = s5_d.shape[1]
    attn_width = len(DIL_PATTERNS) * ATTN_HEADS * ATTN_HEAD_DIM
    rwkv_width = rwkv_w0.shape[1]
    rwkv_part = rwkv_shift_mu.shape[1]
    q_off = s5_width
    p_off = s5_width + 3 * attn_width
    gate_off = p_off + rwkv_part
    tn_in = 512
    tiles = _row_tiles(m)
    w_in_t = jnp.swapaxes(w_in, 1, 2)
    n_front = -(-gate_off // tn_in) * tn_in
    rk3 = rwkv_r_k.reshape(depth, rwkv_width)

    xf = x.reshape(m, d_model)
    v_first = None
    xg, rinv = _norm_parts(xf, ffn1_norm, 0)
    for l in range(depth):
        xf, xg, rinv = _swiglu_ffn(xf, xg, rinv, ffn1_up, ffn1_down, l, tiles["ffn_up"], tiles["ffn_down"],
                                   (mix_norm, l))
        proj = _mm("in_proj", [xg], [(w_in_t, l, ("T", 0), 0)], [], _ep_plain, n_front, F32, tiles["in_proj"], tn_in,
                   row_scale=rinv)
        gates = _mm("in_proj_gates", [xg], [(w_in_t, l, ("T", gate_off), 0)], [], _ep_plain,
                    3 * d_model, BF16, tiles["in_proj"], tn_in, row_scale=rinv)
        proj3 = proj.reshape(bsz, seq, n_front)

        z = _s5_mixer_pre_glu(proj3, l, s5_a_re, s5_a_im, s5_b_re, s5_b_im, s5_c_re, s5_c_im, s5_log_dt, s5_d)
        z = z.reshape(m, s5_width)
        y_s5 = _mm("s5_glu", [z], [(s5_w_glu, l, 0, 0)], [(z, 0)], _ep_glu, s5_width, BF16, tiles["narrow"], tn_in)

        y_attn = _attention(proj3, q_off, l, attn_q_gain, attn_k_gain).reshape(m, -1)

        y_rwkv, v_first = _rwkv_mixer(proj, p_off, bsz, seq, l, v_first, rwkv_shift_mu, rwkv_w0, rwkv_w2, rwkv_a0,
                                      rwkv_a2, rwkv_g2, rwkv_k_k, rwkv_k_a, rk3, rwkv_ln_w, rwkv_ln_b,
                                      rwkv_v0, rwkv_v1, rwkv_v2, tiles["rwkv_prep"])
        y_rwkv = y_rwkv.reshape(m, rwkv_width)

        tn = 256
        gb = d_model // tn
        merged = _mm("branch_merge", [y_s5, y_attn, y_rwkv],
                     [(w_branch_s5, l, 0, 0), (w_branch_attn, l, 0, 1), (w_branch_rwkv, l, 0, 2)],
                     [(gates, 0), (gates, gb), (gates, 2 * gb)], _ep_gated_sum, d_model, BF16, tiles["narrow"], tn)
        xf, xg, rinv = _mm("mix_out", [merged], [(w_out, l, 0, 0)], [(xf, 0)], _ep_residual, d_model, F32,
                           tiles["narrow"], tn, norm_out=(ffn2_norm, l))
        next_norm = (ffn1_norm, l + 1) if l + 1 < depth else None
        res = _swiglu_ffn(xf, xg, rinv, ffn2_up, ffn2_down, l, tiles["ffn_up"], tiles["ffn_down"], next_norm)
        xf, xg, rinv = res if next_norm is not None else (res, None, None)
    return xf.reshape(bsz, seq, d_model)


def kernel(x, ffn1_norm, ffn1_up, ffn1_down, mix_norm, w_in, s5_a_re, s5_a_im, s5_b_re, s5_b_im, s5_c_re, s5_c_im, s5_log_dt, s5_d, s5_w_glu, attn_q_gain, attn_k_gain, rwkv_shift_mu, rwkv_w0, rwkv_w2, rwkv_a0, rwkv_a2, rwkv_g2, rwkv_k_k, rwkv_k_a, rwkv_r_k, rwkv_ln_w, rwkv_ln_b, rwkv_v0, rwkv_v1, rwkv_v2, w_branch_s5, w_branch_attn, w_branch_rwkv, w_out, ffn2_norm, ffn2_up, ffn2_down):
    return _trunk(x, ffn1_norm, ffn1_up, ffn1_down, mix_norm, w_in, s5_a_re, s5_a_im, s5_b_re, s5_b_im, s5_c_re,
                  s5_c_im, s5_log_dt, s5_d, s5_w_glu, attn_q_gain, attn_k_gain, rwkv_shift_mu, rwkv_w0, rwkv_w2,
                  rwkv_a0, rwkv_a2, rwkv_g2, rwkv_k_k, rwkv_k_a, rwkv_r_k, rwkv_ln_w, rwkv_ln_b, rwkv_v0, rwkv_v1,
                  rwkv_v2, w_branch_s5, w_branch_attn, w_branch_rwkv, w_out, ffn2_norm, ffn2_up, ffn2_down)
```
